```python
import math
import jax, jax.numpy as jnp
from jax import lax
import numpy as np

D_MODEL = 1024
BATCH = 8
SEQ = 4096
DEPTH = 2

D_MIX = D_MODEL
W_CONF = D_MIX // 4
W_POOL = D_MIX // 4
W_DIFF = D_MIX // 4
W_SCONV = D_MIX // 4

CONF_KERNEL = 31
POOL_WINDOWS = (2, 4, 8, 16)
POOL_GROUPS = len(POOL_WINDOWS)
POOL_GROUP_DIM = W_POOL // POOL_GROUPS
DIFF_HEADS = 4
DIFF_HEAD_DIM = W_DIFF // (2 * DIFF_HEADS)
DIFF_V_DIM = 2 * DIFF_HEAD_DIM
ROPE_THETA = 10000.0
Q_BLOCK = 128
SCONV_KERNEL = 3

N_GROUPS = 4
EXPERTS_PER_GROUP = 4
N_EXPERTS = N_GROUPS * EXPERTS_PER_GROUP
TOP_K_IN_GROUP = 2
D_EXPERT = 256

PLE_DIM = 256
EPS = 1e-6

IN_WIDTHS = (W_CONF, W_CONF,
             W_POOL,
             W_DIFF, W_DIFF, W_DIFF,
             W_SCONV, W_SCONV, W_SCONV)
IN_COLS = sum(IN_WIDTHS)
IN_SPLITS = tuple(int(s) for s in np.cumsum(IN_WIDTHS)[:-1])

kernel_name = "hybrid_parallel_heads_hmoe_block"


def rmsnorm(x, g):
    xf = x.astype(jnp.float32)
    y = xf * lax.rsqrt(jnp.mean(xf * xf, axis=-1, keepdims=True) + EPS)
    return (y * g.astype(jnp.float32)).astype(x.dtype)


def layernorm(x, g, b):
    xf = x.astype(jnp.float32)
    mu = jnp.mean(xf, axis=-1, keepdims=True)
    var = jnp.mean(jnp.square(xf - mu), axis=-1, keepdims=True)
    y = (xf - mu) * lax.rsqrt(var + EPS)
    return (y * g.astype(jnp.float32) + b.astype(jnp.float32)).astype(x.dtype)


def causal_dwconv(u, w):
    k, c = w.shape
    return lax.conv_general_dilated(
        u, w.astype(u.dtype)[:, None, :], window_strides=(1,),
        padding=[(k - 1, 0)], dimension_numbers=("NWC", "WIO", "NWC"),
        feature_group_count=c)


def rope_tables(seq, dim):
    pos = jnp.arange(seq, dtype=jnp.float32)
    inv = ROPE_THETA ** (-jnp.arange(0, dim, 2, dtype=jnp.float32) / dim)
    ang = pos[:, None] * inv[None, :]
    ang = jnp.concatenate([ang, ang], axis=-1)
    return jnp.cos(ang), jnp.sin(ang)


def apply_rope(x, cos, sin):
    xf = x.astype(jnp.float32)
    half = xf.shape[-1] // 2
    rot = jnp.concatenate([-xf[..., half:], xf[..., :half]], axis=-1)
    c = cos[None, :, None, None, :]
    s = sin[None, :, None, None, :]
    return (xf * c + rot * s).astype(x.dtype)


def conformer_conv(a_val, a_gate, conv_w, conv_b, ln_g, ln_b):
    glu = a_val * jax.nn.sigmoid(a_gate)
    y = causal_dwconv(glu, conv_w) + conv_b.astype(glu.dtype)
    return jax.nn.silu(layernorm(y, ln_g, ln_b))


def pool_mixer(u, w, b, scale):
    bsz, s, _ = u.shape
    cs = jnp.cumsum(u.astype(jnp.float32), axis=1)
    t = jnp.arange(s)
    outs = []
    for g, win in enumerate(POOL_WINDOWS):
        sl = slice(g * POOL_GROUP_DIM, (g + 1) * POOL_GROUP_DIM)
        csg = cs[..., sl]
        shifted = jnp.pad(csg, ((0, 0), (win, 0), (0, 0)))[:, :s]
        cnt = jnp.minimum(t + 1, win).astype(jnp.float32)[None, :, None]
        outs.append((csg - shifted) / cnt)
    pooled = jnp.concatenate(outs, axis=-1).astype(u.dtype) - u
    pg = pooled.reshape(bsz, s, POOL_GROUPS, POOL_GROUP_DIM)
    y = jnp.einsum("bsgc,gcd->bsgd", pg, w) + b
    return y.reshape(bsz, s, W_POOL) * scale


def diff_attention(q, k, v, lam, subln_g, lam_init, cos, sin):
    bsz, s = q.shape[0], q.shape[1]
    q = apply_rope(q, cos, sin)
    k = apply_rope(k, cos, sin)
    scale = DIFF_HEAD_DIM ** -0.5
    key_pos = jnp.arange(s)

    def block(i):
        start = i * Q_BLOCK
        qb = lax.dynamic_slice_in_dim(q, start, Q_BLOCK, axis=1)
        sc = jnp.einsum("bqhcd,bkhcd->bhcqk", qb, k,
                        preferred_element_type=jnp.float32) * scale
        qpos = start + jnp.arange(Q_BLOCK)
        mask = key_pos[None, :] <= qpos[:, None]
        a = jax.nn.softmax(jnp.where(mask, sc, -jnp.inf), axis=-1)
        a = a[:, :, 0] - lam * a[:, :, 1]
        return jnp.einsum("bhqk,bkhe->bqhe", a.astype(v.dtype), v)

    o = lax.map(block, jnp.arange(s // Q_BLOCK))
    o = jnp.moveaxis(o, 0, 1).reshape(bsz, s, DIFF_HEADS, DIFF_V_DIM)
    o = rmsnorm(o, subln_g) * (1.0 - lam_init)
    return o.reshape(bsz, s, W_DIFF)


def short_gated_conv(gb, gc, val, w):
    return gb * causal_dwconv(gc * val, w)


def hier_moe(x, wg, bg, we, be, w_gate, w_up, w_down):
    bsz, s, _ = x.shape
    gl = jnp.einsum("bsd,dg->bsg", x, wg, preferred_element_type=jnp.float32) + bg
    pg = jax.nn.softmax(gl, axis=-1)
    c = jnp.argmax(gl, axis=-1)
    el = jnp.einsum("bsd,de->bse", x, we, preferred_element_type=jnp.float32) + be
    el = el.reshape(bsz, s, N_GROUPS, EXPERTS_PER_GROUP)
    el_c = jnp.take_along_axis(el, c[..., None, None], axis=2)[..., 0, :]
    pe = jax.nn.softmax(el_c, axis=-1)
    tv, ti = lax.top_k(pe, TOP_K_IN_GROUP)
    tv = tv / jnp.sum(tv, axis=-1, keepdims=True)
    gc = jnp.take_along_axis(pg, c[..., None], axis=-1)
    wts = gc * tv
    idx = c[..., None] * EXPERTS_PER_GROUP + ti
    combine = jnp.sum(jax.nn.one_hot(idx, N_EXPERTS, dtype=jnp.float32)
                      * wts[..., None], axis=-2)
    y = jnp.zeros(x.shape, jnp.float32)
    for e in range(N_EXPERTS):
        hdn = jax.nn.silu(x @ w_gate[e]) * (x @ w_up[e])
        y = y + combine[..., e:e + 1] * (hdn @ w_down[e]).astype(jnp.float32)
    return y.astype(x.dtype)


def setup_inputs(seed: int = 0) -> dict:
    key = jax.random.key(seed)
    ks = iter(jax.random.split(key, 40))

    def nrm(shape, scale):
        return scale * jax.random.normal(next(ks), shape, jnp.float32)

    D = D_MODEL
    return {
        "x": nrm((BATCH, SEQ, D), 1.0),
        "p": nrm((DEPTH, BATCH, SEQ, PLE_DIM), 1.0),
        "mix_norm": 1.0 + nrm((DEPTH, D), 0.05),
        "w_in": nrm((DEPTH, D, IN_COLS), D ** -0.5),
        "conf_conv_w": nrm((DEPTH, CONF_KERNEL, W_CONF), CONF_KERNEL ** -0.5),
        "conf_conv_b": nrm((DEPTH, W_CONF), 0.02),
        "conf_ln_g": 1.0 + nrm((DEPTH, W_CONF), 0.05),
        "conf_ln_b": nrm((DEPTH, W_CONF), 0.02),
        "pool_w": nrm((DEPTH, POOL_GROUPS, POOL_GROUP_DIM, POOL_GROUP_DIM), POOL_GROUP_DIM ** -0.5),
        "pool_b": nrm((DEPTH, POOL_GROUPS, POOL_GROUP_DIM), 0.02),
        "pool_scale": 1.0 + nrm((DEPTH, W_POOL), 0.05),
        "diff_lam_q1": nrm((DEPTH, DIFF_HEAD_DIM), 0.1),
        "diff_lam_k1": nrm((DEPTH, DIFF_HEAD_DIM), 0.1),
        "diff_lam_q2": nrm((DEPTH, DIFF_HEAD_DIM), 0.1),
        "diff_lam_k2": nrm((DEPTH, DIFF_HEAD_DIM), 0.1),
        "diff_subln_g": 1.0 + nrm((DEPTH, DIFF_V_DIM), 0.05),
        "sconv_w": nrm((DEPTH, SCONV_KERNEL, W_SCONV), SCONV_KERNEL ** -0.5),
        "w_out": nrm((DEPTH, D_MIX, D), D_MIX ** -0.5),
        "ffn_norm": 1.0 + nrm((DEPTH, D), 0.05),
        "router_group_w": nrm((DEPTH, D, N_GROUPS), D ** -0.5),
        "router_group_b": nrm((DEPTH, N_GROUPS), 0.01),
        "router_expert_w": nrm((DEPTH, D, N_EXPERTS), D ** -0.5),
        "router_expert_b": nrm((DEPTH, N_EXPERTS), 0.01),
        "expert_w_gate": nrm((DEPTH, N_EXPERTS, D, D_EXPERT), D ** -0.5),
        "expert_w_up": nrm((DEPTH, N_EXPERTS, D, D_EXPERT), D ** -0.5),
        "expert_w_down": nrm((DEPTH, N_EXPERTS, D_EXPERT, D), D_EXPERT ** -0.5),
        "ple_norm": 1.0 + nrm((DEPTH, D), 0.05),
        "ple_gate_w": nrm((DEPTH, D, D), D ** -0.5),
        "ple_gate_b": nrm((DEPTH, D), 0.02),
        "ple_proj": nrm((DEPTH, PLE_DIM, D), PLE_DIM ** -0.5),
        "final_norm": 1.0 + nrm((D,), 0.05),
    }


def reference(x, p, mix_norm, w_in, conf_conv_w, conf_conv_b, conf_ln_g, conf_ln_b,
              pool_w, pool_b, pool_scale, diff_lam_q1, diff_lam_k1, diff_lam_q2,
              diff_lam_k2, diff_subln_g, sconv_w, w_out, ffn_norm, router_group_w,
              router_group_b, router_expert_w, router_expert_b, expert_w_gate,
              expert_w_up, expert_w_down, ple_norm, ple_gate_w, ple_gate_b, ple_proj,
              final_norm):
    bsz, s, _ = x.shape
    cos, sin = rope_tables(s, DIFF_HEAD_DIM)
    h = x
    for i in range(DEPTH):
        n = rmsnorm(h, mix_norm[i])
        u = n @ w_in[i]
        a_val, a_gate, pool_in, q, k, v, gb, gc, sv = jnp.split(u, IN_SPLITS, axis=-1)

        y_a = conformer_conv(a_val, a_gate, conf_conv_w[i], conf_conv_b[i],
                             conf_ln_g[i], conf_ln_b[i])
        y_b = pool_mixer(pool_in, pool_w[i], pool_b[i], pool_scale[i])

        lam_init = 0.8 - 0.6 * math.exp(-0.3 * i)
        lam = (jnp.exp(jnp.sum(diff_lam_q1[i].astype(jnp.float32) * diff_lam_k1[i].astype(jnp.float32)))
               - jnp.exp(jnp.sum(diff_lam_q2[i].astype(jnp.float32) * diff_lam_k2[i].astype(jnp.float32)))
               + lam_init)
        y_c = diff_attention(q.reshape(bsz, s, DIFF_HEADS, 2, DIFF_HEAD_DIM),
                             k.reshape(bsz, s, DIFF_HEADS, 2, DIFF_HEAD_DIM),
                             v.reshape(bsz, s, DIFF_HEADS, DIFF_V_DIM),
                             lam, diff_subln_g[i], lam_init, cos, sin)
        y_d = short_gated_conv(gb, gc, sv, sconv_w[i])

        mixed = jnp.concatenate([y_a, y_b, y_c, y_d], axis=-1)
        h = h + mixed @ w_out[i]

        h = h + hier_moe(rmsnorm(h, ffn_norm[i]), router_group_w[i], router_group_b[i],
                         router_expert_w[i], router_expert_b[i], expert_w_gate[i],
                         expert_w_up[i], expert_w_down[i])

        gate = jax.nn.sigmoid(rmsnorm(h, ple_norm[i]) @ ple_gate_w[i] + ple_gate_b[i])
        h = h + gate * (p[i] @ ple_proj[i])
    return rmsnorm(h, final_norm)
```

```python
import functools
import math

import jax
import jax.numpy as jnp
from jax import lax
from jax.experimental import pallas as pl
from jax.experimental.pallas import tpu as pltpu

F32 = jnp.float32
BF16 = jnp.bfloat16

EPS = 1e-6
ROPE_THETA = 10000.0
W_GROUP = 256
CONF_KERNEL = 31
CONF_HIST = 32
POOL_HIST = 16
SCONV_KERNEL = 3
SCONV_HIST = 8
DIFF_HEADS = 4
DIFF_HEAD_DIM = 32
DIFF_V_DIM = 64
N_GROUPS = 4
EXPERTS_PER_GROUP = 4
N_EXPERTS = 16
D_EXPERT = 256
ROUTER_LANES = 128
ROUTER_LO = 32
NEG_BIG = -1e30
LOG2E = 1.4426950408889634

VMEM_LIMIT = 48 * 1024 * 1024

SEQ_TILE = 512
ROW_CHUNK = 64
ATT_TILE = 256
TOK_TILE = 512


def _cparams(sem):
    return pltpu.CompilerParams(dimension_semantics=sem, vmem_limit_bytes=VMEM_LIMIT)


def _rms_rows(x, g):
    ms = jnp.mean(x * x, axis=-1, keepdims=True)
    return x * lax.rsqrt(ms + EPS) * g


def _sigmoid(x):
    return 1.0 / (1.0 + jnp.exp(-x))


def _inproj_mix_kernel(h_ref, g_ref, w_ref, rope_ref, cw_ref, cb_ref, lg_ref, lb_ref,
                       pw_ref, pb_ref, ps_ref, sw_ref,
                       yabd_ref, qt_ref, k_ref, vt_ref,
                       gbuf, pbuf, zbuf, *, tm):
    s = pl.program_id(1)

    @pl.when(s == 0)
    def _():
        gbuf[0:CONF_HIST, :] = jnp.zeros((CONF_HIST, W_GROUP), F32)
        pbuf[0:POOL_HIST, :] = jnp.zeros((POOL_HIST, W_GROUP), F32)
        zbuf[0:SCONV_HIST, :] = jnp.zeros((SCONV_HIST, W_GROUP), F32)

    n = _rms_rows(h_ref[...], g_ref[...]).astype(BF16)

    def proj(lo, hi):
        return jnp.dot(n, w_ref[:, lo:hi], preferred_element_type=F32)

    a = proj(0, 2 * W_GROUP)
    gbuf[CONF_HIST:CONF_HIST + tm, :] = a[:, :W_GROUP] * _sigmoid(a[:, W_GROUP:])
    base = CONF_HIST - (CONF_KERNEL - 1)
    for c in range(tm // ROW_CHUNK):
        r0 = c * ROW_CHUNK
        acc = jnp.broadcast_to(cb_ref[...], (ROW_CHUNK, W_GROUP))
        for j in range(CONF_KERNEL):
            acc = acc + cw_ref[j:j + 1, :] * gbuf[r0 + base + j:r0 + base + j + ROW_CHUNK, :]
        mu = jnp.mean(acc, axis=-1, keepdims=True)
        d = acc - mu
        var = jnp.mean(d * d, axis=-1, keepdims=True)
        y = d * lax.rsqrt(var + EPS) * lg_ref[...] + lb_ref[...]
        yabd_ref[r0:r0 + ROW_CHUNK, 0:W_GROUP] = (y * _sigmoid(y)).astype(BF16)
    gbuf[0:CONF_HIST, :] = gbuf[tm:tm + CONF_HIST, :]

    pbuf[POOL_HIST:POOL_HIST + tm, :] = proj(2 * W_GROUP, 3 * W_GROUP)
    lane = lax.broadcasted_iota(jnp.int32, (ROW_CHUNK, 128), 1)
    first = lane < 64
    for c in range(tm // ROW_CHUNK):
        r0 = c * ROW_CHUNK
        tpos = (s * tm + r0 + 1 + lax.broadcasted_iota(jnp.int32, (ROW_CHUNK, 128), 0)).astype(F32)
        halves = []
        for half, (w_small, w_big) in enumerate(((2, 4), (8, 16))):
            l0 = half * 128

            def ld(j):
                return pbuf[r0 + POOL_HIST - j:r0 + POOL_HIST - j + ROW_CHUNK, l0:l0 + 128]

            cur = ld(0)
            run = cur
            for j in range(1, w_small):
                run = run + ld(j)
            small = run
            for j in range(w_small, w_big):
                run = run + ld(j)
            cnt = jnp.where(first, jnp.minimum(tpos, float(w_small)), jnp.minimum(tpos, float(w_big)))
            halves.append(jnp.where(first, small, run) / cnt - cur)
        pc = jnp.concatenate(halves, axis=1).astype(BF16)
        yb = (jnp.dot(pc, pw_ref[...], preferred_element_type=F32) + pb_ref[...]) * ps_ref[...]
        yabd_ref[r0:r0 + ROW_CHUNK, W_GROUP:2 * W_GROUP] = yb.astype(BF16)
    pbuf[0:POOL_HIST, :] = pbuf[tm:tm + POOL_HIST, :]

    sc = proj(6 * W_GROUP, 9 * W_GROUP)
    zbuf[SCONV_HIST:SCONV_HIST + tm, :] = sc[:, W_GROUP:2 * W_GROUP] * sc[:, 2 * W_GROUP:]
    conv = None
    for j in range(SCONV_KERNEL):
        off = SCONV_HIST - (SCONV_KERNEL - 1) + j
        term = sw_ref[j:j + 1, :] * zbuf[off:off + tm, :]
        conv = term if conv is None else conv + term
    yabd_ref[:, 2 * W_GROUP:3 * W_GROUP] = (sc[:, :W_GROUP] * conv).astype(BF16)
    zbuf[0:SCONV_HIST, :] = zbuf[tm:tm + SCONV_HIST, :]

    qk = proj(3 * W_GROUP, 5 * W_GROUP)
    tab = rope_ref[...]
    cos = jnp.concatenate([tab[:, 0:128]] * 4, axis=1)
    sin_lo = jnp.concatenate([tab[:, 128:256]] * 4, axis=1)
    sin_hi = jnp.concatenate([tab[:, 256:384]] * 4, axis=1)
    half = DIFF_HEAD_DIM // 2
    width = 2 * W_GROUP
    qk = qk * cos + pltpu.roll(qk, width - half, axis=1) * sin_lo + pltpu.roll(qk, half, axis=1) * sin_hi
    q = qk[:, :W_GROUP] * (DIFF_HEAD_DIM ** -0.5 * LOG2E)
    qt_ref[...] = q.T.astype(BF16)
    k_ref[...] = qk[:, W_GROUP:].astype(BF16)
    vt_ref[...] = proj(5 * W_GROUP, 6 * W_GROUP).T.astype(BF16)


def _inproj_mix(h, gain, w_in, rope, cw, cb, lg, lb, pw_bd, pb, ps, sw):
    bsz, seq, d = h.shape
    tm = min(SEQ_TILE, seq)
    assert seq % tm == 0 and tm % ROW_CHUNK == 0 and tm >= CONF_HIST
    ncol = w_in.shape[1]
    full = lambda shape: pl.BlockSpec(shape, lambda b, s: (0,) * len(shape))
    return pl.pallas_call(
        functools.partial(_inproj_mix_kernel, tm=tm),
        grid=(bsz, seq // tm),
        in_specs=[
            pl.BlockSpec((None, tm, d), lambda b, s: (b, s, 0)),
            full((1, d)),
            full((d, ncol)),
            pl.BlockSpec((tm, 384), lambda b, s: (s, 0)),
            full((CONF_KERNEL, W_GROUP)), full((1, W_GROUP)), full((1, W_GROUP)), full((1, W_GROUP)),
            full((W_GROUP, W_GROUP)), full((1, W_GROUP)), full((1, W_GROUP)),
            full((SCONV_KERNEL, W_GROUP)),
        ],
        out_specs=[
            pl.BlockSpec((None, tm, 3 * W_GROUP), lambda b, s: (b, s, 0)),
            pl.BlockSpec((None, W_GROUP, tm), lambda b, s: (b, 0, s)),
            pl.BlockSpec((None, tm, W_GROUP), lambda b, s: (b, s, 0)),
            pl.BlockSpec((None, W_GROUP, tm), lambda b, s: (b, 0, s)),
        ],
        out_shape=[
            jax.ShapeDtypeStruct((bsz, seq, 3 * W_GROUP), BF16),
            jax.ShapeDtypeStruct((bsz, W_GROUP, seq), BF16),
            jax.ShapeDtypeStruct((bsz, seq, W_GROUP), BF16),
            jax.ShapeDtypeStruct((bsz, W_GROUP, seq), BF16),
        ],
        scratch_shapes=[
            pltpu.VMEM((tm + CONF_HIST, W_GROUP), F32),
            pltpu.VMEM((tm + POOL_HIST, W_GROUP), F32),
            pltpu.VMEM((tm + SCONV_HIST, W_GROUP), F32),
        ],
        compiler_params=_cparams(("arbitrary", "arbitrary")),
        name="inproj_mix",
    )(h, gain, w_in, rope, cw, cb, lg, lb, pw_bd, pb, ps, sw)


def _diff_attn_kernel(qt_ref, k_ref, vt_ref, lam_ref, g_ref, o_ref,
                      qm_ref, m_ref, l_ref, acc_ref, *, t, lam_init):
    qi = pl.program_id(1)
    nhc = 2 * DIFF_HEADS

    qm_ref[...] = jnp.zeros(qm_ref.shape, BF16)
    for j in range(nhc):
        lo = j * DIFF_HEAD_DIM
        qm_ref[j, lo:lo + DIFF_HEAD_DIM, :] = qt_ref[lo:lo + DIFF_HEAD_DIM, :]
    m_ref[...] = jnp.full(m_ref.shape, NEG_BIG, F32)
    l_ref[...] = jnp.zeros(l_ref.shape, F32)
    acc_ref[...] = jnp.zeros(acc_ref.shape, F32)

    def tile(kt, masked):
        k0 = pl.multiple_of(kt * t, t)
        ktile = k_ref[pl.ds(k0, t), :]
        if masked:
            above = (lax.broadcasted_iota(jnp.int32, (t, t), 0) > lax.broadcasted_iota(jnp.int32, (t, t), 1))
        for j in range(nhc):
            hd = j // 2
            sc = jnp.dot(ktile, qm_ref[j], preferred_element_type=F32)
            if masked:
                sc = jnp.where(above, NEG_BIG, sc)
            m_old = m_ref[j]
            m_new = jnp.maximum(m_old, jnp.max(sc, axis=0, keepdims=True))
            alpha = jnp.exp2(m_old - m_new)
            p = jnp.exp2(sc - m_new)
            l_ref[j] = alpha * l_ref[j] + jnp.sum(p, axis=0, keepdims=True)
            m_ref[j] = m_new
            vth = vt_ref[hd * DIFF_V_DIM:(hd + 1) * DIFF_V_DIM, pl.ds(k0, t)]
            acc_ref[j] = alpha * acc_ref[j] + jnp.dot(vth, p.astype(BF16), preferred_element_type=F32)

    def body(kt, carry):
        tile(kt, False)
        return carry

    lax.fori_loop(0, qi, body, 0)
    tile(qi, True)

    lp = lam_ref[...]
    lam = (jnp.exp(jnp.sum(lp[0:1] * lp[1:2], axis=-1, keepdims=True))
           - jnp.exp(jnp.sum(lp[2:3] * lp[3:4], axis=-1, keepdims=True)) + lam_init)
    outs = []
    for hd in range(DIFF_HEADS):
        o1 = acc_ref[2 * hd] / l_ref[2 * hd]
        o2 = acc_ref[2 * hd + 1] / l_ref[2 * hd + 1]
        o = o1 - lam * o2
        ms = jnp.mean(o * o, axis=0, keepdims=True)
        outs.append(o * lax.rsqrt(ms + EPS) * g_ref[...] * (1.0 - lam_init))
    o_ref[...] = jnp.concatenate(outs, axis=0).T.astype(BF16)


def _diff_attn(qt, k, vt, lam_params, g_bcast, lam_init):
    bsz, seq, _ = k.shape
    t = min(ATT_TILE, seq)
    assert seq % t == 0
    nhc = 2 * DIFF_HEADS
    return pl.pallas_call(
        functools.partial(_diff_attn_kernel, t=t, lam_init=lam_init),
        grid=(bsz, seq // t),
        in_specs=[
            pl.BlockSpec((None, W_GROUP, t), lambda b, q: (b, 0, q)),
            pl.BlockSpec((None, seq, W_GROUP), lambda b, q: (b, 0, 0)),
            pl.BlockSpec((None, W_GROUP, seq), lambda b, q: (b, 0, 0)),
            pl.BlockSpec((4, DIFF_HEAD_DIM), lambda b, q: (0, 0)),
            pl.BlockSpec((DIFF_V_DIM, t), lambda b, q: (0, 0)),
        ],
        out_specs=pl.BlockSpec((None, t, W_GROUP), lambda b, q: (b, q, 0)),
        out_shape=jax.ShapeDtypeStruct((bsz, seq, W_GROUP), BF16),
        scratch_shapes=[
            pltpu.VMEM((nhc, W_GROUP, t), BF16),
            pltpu.VMEM((nhc, 1, t), F32),
            pltpu.VMEM((nhc, 1, t), F32),
            pltpu.VMEM((nhc, DIFF_V_DIM, t), F32),
        ],
        compiler_params=_cparams(("arbitrary", "arbitrary")),
        name="diff_attn",
    )(qt, k, vt, lam_params, g_bcast)


def _outproj_router_kernel(h_ref, yabd_ref, yc_ref, wabd_ref, wc_ref, g_ref, wr_ref, br_ref,
                           h1_ref, xm_ref, comb_ref, rt_ref, ct_ref):
    h1 = (h_ref[...]
          + jnp.dot(yabd_ref[...], wabd_ref[...], preferred_element_type=F32)
          + jnp.dot(yc_ref[...], wc_ref[...], preferred_element_type=F32))
    h1_ref[...] = h1
    xn = _rms_rows(h1, g_ref[...])
    x_hi = xn.astype(BF16)
    x_lo = (xn - x_hi.astype(F32)).astype(BF16)
    xm_ref[...] = x_hi
    r = (jnp.dot(x_hi, wr_ref[...], preferred_element_type=F32)
         + jnp.dot(x_lo, wr_ref[...], preferred_element_type=F32) + br_ref[...])
    rt_ref[...] = r.T

    def row(i):
        return rt_ref[i:i + 1, :] + rt_ref[ROUTER_LO + i:ROUTER_LO + i + 1, :]

    def first_hits(vals, target):
        hits, free = [], None
        for v in vals:
            f = jnp.where(v == target, 1.0, 0.0)
            hits.append(f if free is None else f * free)
            free = (1.0 - f) if free is None else free * (1.0 - f)
        return hits

    gl = [row(i) for i in range(N_GROUPS)]
    el = [row(8 + e) for e in range(N_EXPERTS)]
    gmax = functools.reduce(jnp.maximum, gl)
    gsel = first_hits(gl, gmax)
    pg_c = 1.0 / functools.reduce(lambda a, b: a + b, [jnp.exp(v - gmax) for v in gl])
    elc = []
    for kk in range(EXPERTS_PER_GROUP):
        v = gsel[0] * el[kk]
        for g in range(1, N_GROUPS):
            v = v + gsel[g] * el[g * EXPERTS_PER_GROUP + kk]
        elc.append(v)
    emax = functools.reduce(jnp.maximum, elc)
    ex = [jnp.exp(v - emax) for v in elc]
    se = functools.reduce(lambda a, b: a + b, ex)
    pe = [v / se for v in ex]
    p1 = functools.reduce(jnp.maximum, pe)
    t1 = first_hits(pe, p1)
    rest = [jnp.where(t > 0.0, -1.0, v) for t, v in zip(t1, pe)]
    p2 = functools.reduce(jnp.maximum, rest)
    t2 = first_hits(rest, p2)
    den = p1 + p2
    w1 = pg_c * (p1 / den)
    w2 = pg_c * (p2 / den)
    wk = [a * w1 + b * w2 for a, b in zip(t1, t2)]
    ct_ref[...] = jnp.zeros(ct_ref.shape, F32)
    for g in range(N_GROUPS):
        for kk in range(EXPERTS_PER_GROUP):
            e = g * EXPERTS_PER_GROUP + kk
            ct_ref[e:e + 1, :] = gsel[g] * wk[kk]
    comb_ref[...] = ct_ref[...].T


def _outproj_router(h2d, yabd, yc, wabd, wc, gain, wr, br):
    tkn, d = h2d.shape
    tm = min(TOK_TILE, tkn)
    assert tkn % tm == 0
    row = lambda w: pl.BlockSpec((tm, w), lambda i: (i, 0))
    full = lambda shape: pl.BlockSpec(shape, lambda i: (0,) * len(shape))
    return pl.pallas_call(
        _outproj_router_kernel,
        grid=(tkn // tm,),
        in_specs=[row(d), row(3 * W_GROUP), row(W_GROUP),
                  full((3 * W_GROUP, d)), full((W_GROUP, d)), full((1, d)),
                  full((d, ROUTER_LANES)), full((1, ROUTER_LANES))],
        out_specs=[row(d), row(d), row(ROUTER_LANES)],
        out_shape=[jax.ShapeDtypeStruct((tkn, d), F32),
                   jax.ShapeDtypeStruct((tkn, d), BF16),
                   jax.ShapeDtypeStruct((tkn, ROUTER_LANES), F32)],
        scratch_shapes=[pltpu.VMEM((ROUTER_LANES, tm), F32), pltpu.VMEM((ROUTER_LANES, tm), F32)],
        compiler_params=_cparams(("arbitrary",)),
        name="outproj_router",
    )(h2d, yabd, yc, wabd, wc, gain, wr, br)


def _moe_kernel(x_ref, h1_ref, comb_ref, wgu_ref, wd_ref, o_ref):
    e = pl.program_id(1)
    gu = jnp.dot(x_ref[...], wgu_ref[...], preferred_element_type=F32)
    gate = gu[:, :D_EXPERT]
    hdn = gate * _sigmoid(gate) * gu[:, D_EXPERT:]
    comb = comb_ref[...]
    lane = lax.broadcasted_iota(jnp.int32, comb.shape, 1)
    c_e = jnp.sum(jnp.where(lane == e, comb, 0.0), axis=-1, keepdims=True)
    contrib = jnp.dot((hdn * c_e).astype(BF16), wd_ref[...], preferred_element_type=F32)

    @pl.when(e == 0)
    def _():
        o_ref[...] = h1_ref[...] + contrib

    @pl.when(e > 0)
    def _():
        o_ref[...] = o_ref[...] + contrib


def _moe(xm, h1, comb, wgu, wd):
    tkn, d = h1.shape
    tm = min(TOK_TILE, tkn)
    assert tkn % tm == 0
    return pl.pallas_call(
        _moe_kernel,
        grid=(tkn // tm, N_EXPERTS),
        in_specs=[pl.BlockSpec((tm, d), lambda i, e: (i, 0)),
                  pl.BlockSpec((tm, d), lambda i, e: (i, 0)),
                  pl.BlockSpec((tm, ROUTER_LANES), lambda i, e: (i, 0)),
                  pl.BlockSpec((None, d, 2 * D_EXPERT), lambda i, e: (e, 0, 0)),
                  pl.BlockSpec((None, D_EXPERT, d), lambda i, e: (e, 0, 0))],
        out_specs=pl.BlockSpec((tm, d), lambda i, e: (i, 0)),
        out_shape=jax.ShapeDtypeStruct((tkn, d), F32),
        compiler_params=_cparams(("arbitrary", "arbitrary")),
        name="moe_experts",
    )(xm, h1, comb, wgu, wd)


def _ple_kernel(h_ref, p_ref, g_ref, wg_ref, bg_ref, wp_ref, fg_ref, o_ref, *, final):
    h = h_ref[...]
    xn = _rms_rows(h, g_ref[...]).astype(BF16)
    gate = _sigmoid(jnp.dot(xn, wg_ref[...], preferred_element_type=F32) + bg_ref[...])
    out = h + gate * jnp.dot(p_ref[...].astype(BF16), wp_ref[...], preferred_element_type=F32)
    if final:
        out = _rms_rows(out, fg_ref[...])
    o_ref[...] = out


def _ple(h2d, p2d, gain, wg, bg, wp, fgain, final):
    tkn, d = h2d.shape
    pd = p2d.shape[1]
    tm = min(TOK_TILE, tkn)
    assert tkn % tm == 0
    full = lambda shape: pl.BlockSpec(shape, lambda i: (0,) * len(shape))
    return pl.pallas_call(
        functools.partial(_ple_kernel, final=final),
        grid=(tkn // tm,),
        in_specs=[pl.BlockSpec((tm, d), lambda i: (i, 0)), pl.BlockSpec((tm, pd), lambda i: (i, 0)),
                  full((1, d)), full((d, d)), full((1, d)), full((pd, d)), full((1, d))],
        out_specs=pl.BlockSpec((tm, d), lambda i: (i, 0)),
        out_shape=jax.ShapeDtypeStruct((tkn, d), F32),
        compiler_params=_cparams(("arbitrary",)),
        name="ple_gate",
    )(h2d, p2d, gain, wg, bg, wp, fgain)


def _rope_table(seq):
    pos = jnp.arange(seq, dtype=F32)
    inv = ROPE_THETA ** (-jnp.arange(0, DIFF_HEAD_DIM, 2, dtype=F32) / DIFF_HEAD_DIM)
    ang = pos[:, None] * inv[None, :]
    ang = jnp.concatenate([ang, ang], axis=-1)
    cos, sin = jnp.cos(ang), jnp.sin(ang)
    lo = jnp.arange(DIFF_HEAD_DIM) < DIFF_HEAD_DIM // 2
    sin_lo = jnp.where(lo, -sin, 0.0)
    sin_hi = jnp.where(lo, 0.0, sin)
    rep = 128 // DIFF_HEAD_DIM
    return jnp.concatenate([jnp.tile(t, (1, rep)) for t in (cos, sin_lo, sin_hi)], axis=1)


def _block_diag(w):
    g, c, dd = w.shape
    out = jnp.zeros((g * c, g * dd), w.dtype)
    for i in range(g):
        out = out.at[i * c:(i + 1) * c, i * dd:(i + 1) * dd].set(w[i])
    return out


def _router_weights(wg, bg, we, be):
    d = wg.shape[0]
    w = jnp.zeros((d, ROUTER_LANES), F32)
    w = w.at[:, 0:N_GROUPS].set(wg).at[:, 8:8 + N_EXPERTS].set(we)
    w_hi = w.astype(BF16)
    w_lo = (w - w_hi.astype(F32)).astype(BF16)
    wr = w_hi.at[:, ROUTER_LO:ROUTER_LO + 8 + N_EXPERTS].set(w_lo[:, 0:8 + N_EXPERTS])
    br = jnp.zeros((1, ROUTER_LANES), F32)
    br = br.at[0, 0:N_GROUPS].set(bg).at[0, 8:8 + N_EXPERTS].set(be)
    return wr, br


def kernel(x, p, mix_norm, w_in, conf_conv_w, conf_conv_b, conf_ln_g, conf_ln_b, pool_w, pool_b, pool_scale, diff_lam_q1, diff_lam_k1, diff_lam_q2, diff_lam_k2, diff_subln_g, sconv_w, w_out, ffn_norm, router_group_w, router_group_b, router_expert_w, router_expert_b, expert_w_gate, expert_w_up, expert_w_down, ple_norm, ple_gate_w, ple_gate_b, ple_proj, final_norm):
    bsz, seq, d = x.shape
    depth = w_in.shape[0]
    tkn = bsz * seq
    rope = _rope_table(seq)
    att_t = min(ATT_TILE, seq)
    row = lambda v: v.reshape(1, -1).astype(F32)

    h = x
    for i in range(depth):
        lam_init = 0.8 - 0.6 * math.exp(-0.3 * i)
        yabd, qt, k, vt = _inproj_mix(
            h, row(mix_norm[i]), w_in[i].astype(BF16), rope,
            conf_conv_w[i], row(conf_conv_b[i]), row(conf_ln_g[i]), row(conf_ln_b[i]),
            _block_diag(pool_w[i]).astype(BF16), row(pool_b[i]), row(pool_scale[i]), sconv_w[i])
        lam_params = jnp.stack([diff_lam_q1[i], diff_lam_k1[i], diff_lam_q2[i], diff_lam_k2[i]]).astype(F32)
        g_bcast = jnp.broadcast_to(diff_subln_g[i].astype(F32)[:, None], (DIFF_V_DIM, att_t))
        yc = _diff_attn(qt, k, vt, lam_params, g_bcast, lam_init)

        wo = w_out[i].astype(BF16)
        wabd = jnp.concatenate([wo[0:2 * W_GROUP], wo[3 * W_GROUP:]], axis=0)
        wc = wo[2 * W_GROUP:3 * W_GROUP]
        wr, br = _router_weights(router_group_w[i], router_group_b[i], router_expert_w[i], router_expert_b[i])
        h1, xm, comb = _outproj_router(
            h.reshape(tkn, d), yabd.reshape(tkn, 3 * W_GROUP), yc.reshape(tkn, W_GROUP),
            wabd, wc, row(ffn_norm[i]), wr, br)

        wgu = jnp.concatenate([expert_w_gate[i], expert_w_up[i]], axis=-1).astype(BF16)
        h2 = _moe(xm, h1, comb, wgu, expert_w_down[i].astype(BF16))

        h3 = _ple(h2, p[i].reshape(tkn, -1), row(ple_norm[i]), ple_gate_w[i].astype(BF16),
                  row(ple_gate_b[i]), ple_proj[i].astype(BF16), row(final_norm), final=(i == depth - 1))
        h = h3.reshape(bsz, seq, d)
    return h
```

```python
import functools
import math

import jax
import jax.numpy as jnp
from jax import lax
from jax.experimental import pallas as pl
from jax.experimental.pallas import tpu as pltpu

F32 = jnp.float32
BF16 = jnp.bfloat16

EPS = 1e-6
ROPE_THETA = 10000.0
W_GROUP = 256
CONF_KERNEL = 31
CONF_HIST = 32
POOL_HIST = 16
SCONV_KERNEL = 3
SCONV_HIST = 8
DIFF_HEADS = 4
DIFF_HEAD_DIM = 32
DIFF_V_DIM = 64
N_GROUPS = 4
EXPERTS_PER_GROUP = 4
N_EXPERTS = 16
D_EXPERT = 256
ROUTER_LANES = 128
ROUTER_LO = 32
NEG_BIG = -1e30
LOG2E = 1.4426950408889634

VMEM_LIMIT = 48 * 1024 * 1024

SEQ_TILE = 512
ROW_CHUNK = 64
ATT_TILE = 256
TOK_TILE = 512


def _cparams(sem):
    return pltpu.CompilerParams(dimension_semantics=sem, vmem_limit_bytes=VMEM_LIMIT)


def _rms_rows(x, g):
    ms = jnp.mean(x * x, axis=-1, keepdims=True)
    return x * lax.rsqrt(ms + EPS) * g


def _sigmoid(x):
    return 1.0 / (1.0 + jnp.exp(-x))


def _inproj_mix_kernel(h_ref, g_ref, w_ref, rope_ref, cw_ref, cb_ref, lg_ref, lb_ref,
                       pw_ref, pb_ref, ps_ref, sw_ref,
                       yabd_ref, qt_ref, k_ref, vt_ref,
                       gbuf, pbuf, zbuf, *, tm):
    s = pl.program_id(1)

    @pl.when(s == 0)
    def _():
        gbuf[0:CONF_HIST, :] = jnp.zeros((CONF_HIST, W_GROUP), F32)
        pbuf[0:POOL_HIST, :] = jnp.zeros((POOL_HIST, W_GROUP), F32)
        zbuf[0:SCONV_HIST, :] = jnp.zeros((SCONV_HIST, W_GROUP), F32)

    n = _rms_rows(h_ref[...], g_ref[...]).astype(BF16)

    def proj(lo, hi):
        return jnp.dot(n, w_ref[:, lo:hi], preferred_element_type=F32)

    a = proj(0, 2 * W_GROUP)
    gbuf[CONF_HIST:CONF_HIST + tm, :] = a[:, :W_GROUP] * _sigmoid(a[:, W_GROUP:])
    base = CONF_HIST - (CONF_KERNEL - 1)
    for c in range(tm // ROW_CHUNK):
        r0 = c * ROW_CHUNK
        acc = jnp.broadcast_to(cb_ref[...], (ROW_CHUNK, W_GROUP))
        for j in range(CONF_KERNEL):
            acc = acc + cw_ref[j:j + 1, :] * gbuf[r0 + base + j:r0 + base + j + ROW_CHUNK, :]
        mu = jnp.mean(acc, axis=-1, keepdims=True)
        d = acc - mu
        var = jnp.mean(d * d, axis=-1, keepdims=True)
        y = d * lax.rsqrt(var + EPS) * lg_ref[...] + lb_ref[...]
        yabd_ref[r0:r0 + ROW_CHUNK, 0:W_GROUP] = (y * _sigmoid(y)).astype(BF16)
    gbuf[0:CONF_HIST, :] = gbuf[tm:tm + CONF_HIST, :]

    pbuf[POOL_HIST:POOL_HIST + tm, :] = proj(2 * W_GROUP, 3 * W_GROUP)
    lane = lax.broadcasted_iota(jnp.int32, (ROW_CHUNK, 128), 1)
    first = lane < 64
    for c in range(tm // ROW_CHUNK):
        r0 = c * ROW_CHUNK
        tpos = (s * tm + r0 + 1 + lax.broadcasted_iota(jnp.int32, (ROW_CHUNK, 128), 0)).astype(F32)
        halves = []
        for half, (w_small, w_big) in enumerate(((2, 4), (8, 16))):
            l0 = half * 128

            def ld(j):
                return pbuf[r0 + POOL_HIST - j:r0 + POOL_HIST - j + ROW_CHUNK, l0:l0 + 128]

            cur = ld(0)
            run = cur
            for j in range(1, w_small):
                run = run + ld(j)
            small = run
            for j in range(w_small, w_big):
                run = run + ld(j)
            cnt = jnp.where(first, jnp.minimum(tpos, float(w_small)), jnp.minimum(tpos, float(w_big)))
            halves.append(jnp.where(first, small, run) / cnt - cur)
        pc = jnp.concatenate(halves, axis=1).astype(BF16)
        yb = (jnp.dot(pc, pw_ref[...], preferred_element_type=F32) + pb_ref[...]) * ps_ref[...]
        yabd_ref[r0:r0 + ROW_CHUNK, W_GROUP:2 * W_GROUP] = yb.astype(BF16)
    pbuf[0:POOL_HIST, :] = pbuf[tm:tm + POOL_HIST, :]

    sc = proj(6 * W_GROUP, 9 * W_GROUP)
    zbuf[SCONV_HIST:SCONV_HIST + tm, :] = sc[:, W_GROUP:2 * W_GROUP] * sc[:, 2 * W_GROUP:]
    conv = None
    for j in range(SCONV_KERNEL):
        off = SCONV_HIST - (SCONV_KERNEL - 1) + j
        term = sw_ref[j:j + 1, :] * zbuf[off:off + tm, :]
        conv = term if conv is None else conv + term
    yabd_ref[:, 2 * W_GROUP:3 * W_GROUP] = (sc[:, :W_GROUP] * conv).astype(BF16)
    zbuf[0:SCONV_HIST, :] = zbuf[tm:tm + SCONV_HIST, :]

    qk = proj(3 * W_GROUP, 5 * W_GROUP)
    tab = rope_ref[...]
    cos = jnp.concatenate([tab[:, 0:128]] * 4, axis=1)
    sin_lo = jnp.concatenate([tab[:, 128:256]] * 4, axis=1)
    sin_hi = jnp.concatenate([tab[:, 256:384]] * 4, axis=1)
    half = DIFF_HEAD_DIM // 2
    width = 2 * W_GROUP
    qk = qk * cos + pltpu.roll(qk, width - half, axis=1) * sin_lo + pltpu.roll(qk, half, axis=1) * sin_hi
    q = qk[:, :W_GROUP] * (DIFF_HEAD_DIM ** -0.5 * LOG2E)
    qt_ref[...] = q.T.astype(BF16)
    k_ref[...] = qk[:, W_GROUP:].astype(BF16)
    vt_ref[...] = proj(5 * W_GROUP, 6 * W_GROUP).T.astype(BF16)


def _inproj_mix(h, gain, w_in, rope, cw, cb, lg, lb, pw_bd, pb, ps, sw):
    bsz, seq, d = h.shape
    tm = min(SEQ_TILE, seq)
    assert seq % tm == 0 and tm % ROW_CHUNK == 0 and tm >= CONF_HIST
    ncol = w_in.shape[1]
    full = lambda shape: pl.BlockSpec(shape, lambda b, s: (0,) * len(shape))
    return pl.pallas_call(
        functools.partial(_inproj_mix_kernel, tm=tm),
        grid=(bsz, seq // tm),
        in_specs=[
            pl.BlockSpec((None, tm, d), lambda b, s: (b, s, 0)),
            full((1, d)),
            full((d, ncol)),
            pl.BlockSpec((tm, 384), lambda b, s: (s, 0)),
            full((CONF_KERNEL, W_GROUP)), full((1, W_GROUP)), full((1, W_GROUP)), full((1, W_GROUP)),
            full((W_GROUP, W_GROUP)), full((1, W_GROUP)), full((1, W_GROUP)),
            full((SCONV_KERNEL, W_GROUP)),
        ],
        out_specs=[
            pl.BlockSpec((None, tm, 3 * W_GROUP), lambda b, s: (b, s, 0)),
            pl.BlockSpec((None, W_GROUP, tm), lambda b, s: (b, 0, s)),
            pl.BlockSpec((None, tm, W_GROUP), lambda b, s: (b, s, 0)),
            pl.BlockSpec((None, W_GROUP, tm), lambda b, s: (b, 0, s)),
        ],
        out_shape=[
            jax.ShapeDtypeStruct((bsz, seq, 3 * W_GROUP), BF16),
            jax.ShapeDtypeStruct((bsz, W_GROUP, seq), BF16),
            jax.ShapeDtypeStruct((bsz, seq, W_GROUP), BF16),
            jax.ShapeDtypeStruct((bsz, W_GROUP, seq), BF16),
        ],
        scratch_shapes=[
            pltpu.VMEM((tm + CONF_HIST, W_GROUP), F32),
            pltpu.VMEM((tm + POOL_HIST, W_GROUP), F32),
            pltpu.VMEM((tm + SCONV_HIST, W_GROUP), F32),
        ],
        compiler_params=_cparams(("arbitrary", "arbitrary")),
        name="inproj_mix",
    )(h, gain, w_in, rope, cw, cb, lg, lb, pw_bd, pb, ps, sw)


def _diff_attn_kernel(qt_ref, k_ref, vt_ref, lam_ref, g_ref, o_ref,
                      qm_ref, p_ref, m_ref, l_ref, acc_ref, *, t, lam_init):
    qi = pl.program_id(1)
    nhc = 2 * DIFF_HEADS

    qm_ref[...] = jnp.zeros(qm_ref.shape, BF16)
    for j in range(nhc):
        lo = j * DIFF_HEAD_DIM
        qm_ref[lo:lo + DIFF_HEAD_DIM, j * t:(j + 1) * t] = qt_ref[lo:lo + DIFF_HEAD_DIM, :]
    m_ref[...] = jnp.full(m_ref.shape, NEG_BIG, F32)
    l_ref[...] = jnp.zeros(l_ref.shape, F32)
    acc_ref[...] = jnp.zeros(acc_ref.shape, F32)

    def tile(kt, masked):
        k0 = pl.multiple_of(kt * t, t)
        s_all = jnp.dot(k_ref[pl.ds(k0, t), :], qm_ref[...], preferred_element_type=F32)
        if masked:
            above = (lax.broadcasted_iota(jnp.int32, (t, t), 0) > lax.broadcasted_iota(jnp.int32, (t, t), 1))
        alphas = []
        for j in range(nhc):
            sc = s_all[:, j * t:(j + 1) * t]
            if masked:
                sc = jnp.where(above, NEG_BIG, sc)
            m_old = m_ref[j]
            m_new = jnp.maximum(m_old, jnp.max(sc, axis=0, keepdims=True))
            alpha = jnp.exp2(m_old - m_new)
            p = jnp.exp2(sc - m_new)
            l_ref[j] = alpha * l_ref[j] + jnp.sum(p, axis=0, keepdims=True)
            m_ref[j] = m_new
            alphas.append(alpha)
            p_ref[:, j * t:(j + 1) * t] = p.astype(BF16)
        for hd in range(DIFF_HEADS):
            vth = vt_ref[hd * DIFF_V_DIM:(hd + 1) * DIFF_V_DIM, pl.ds(k0, t)]
            o = jnp.dot(vth, p_ref[:, 2 * hd * t:(2 * hd + 2) * t], preferred_element_type=F32)
            acc_ref[2 * hd] = alphas[2 * hd] * acc_ref[2 * hd] + o[:, :t]
            acc_ref[2 * hd + 1] = alphas[2 * hd + 1] * acc_ref[2 * hd + 1] + o[:, t:]

    def body(kt, carry):
        tile(kt, False)
        return carry

    lax.fori_loop(0, qi, body, 0)
    tile(qi, True)

    lp = lam_ref[...]
    lam = (jnp.exp(jnp.sum(lp[0:1] * lp[1:2], axis=-1, keepdims=True))
           - jnp.exp(jnp.sum(lp[2:3] * lp[3:4], axis=-1, keepdims=True)) + lam_init)
    outs = []
    for hd in range(DIFF_HEADS):
        o1 = acc_ref[2 * hd] / l_ref[2 * hd]
        o2 = acc_ref[2 * hd + 1] / l_ref[2 * hd + 1]
        o = o1 - lam * o2
        ms = jnp.mean(o * o, axis=0, keepdims=True)
        outs.append(o * lax.rsqrt(ms + EPS) * g_ref[...] * (1.0 - lam_init))
    o_ref[...] = jnp.concatenate(outs, axis=0).T.astype(BF16)


def _diff_attn(qt, k, vt, lam_params, g_bcast, lam_init):
    bsz, seq, _ = k.shape
    t = min(ATT_TILE, seq)
    assert seq % t == 0
    nhc = 2 * DIFF_HEADS
    return pl.pallas_call(
        functools.partial(_diff_attn_kernel, t=t, lam_init=lam_init),
        grid=(bsz, seq // t),
        in_specs=[
            pl.BlockSpec((None, W_GROUP, t), lambda b, q: (b, 0, q)),
            pl.BlockSpec((None, seq, W_GROUP), lambda b, q: (b, 0, 0)),
            pl.BlockSpec((None, W_GROUP, seq), lambda b, q: (b, 0, 0)),
            pl.BlockSpec((4, DIFF_HEAD_DIM), lambda b, q: (0, 0)),
            pl.BlockSpec((DIFF_V_DIM, t), lambda b, q: (0, 0)),
        ],
        out_specs=pl.BlockSpec((None, t, W_GROUP), lambda b, q: (b, q, 0)),
        out_shape=jax.ShapeDtypeStruct((bsz, seq, W_GROUP), BF16),
        scratch_shapes=[
            pltpu.VMEM((W_GROUP, nhc * t), BF16),
            pltpu.VMEM((t, nhc * t), BF16),
            pltpu.VMEM((nhc, 1, t), F32),
            pltpu.VMEM((nhc, 1, t), F32),
            pltpu.VMEM((nhc, DIFF_V_DIM, t), F32),
        ],
        compiler_params=_cparams(("arbitrary", "arbitrary")),
        name="diff_attn",
    )(qt, k, vt, lam_params, g_bcast)


def _outproj_router_kernel(h_ref, yabd_ref, yc_ref, wabd_ref, wc_ref, g_ref, wr_ref, br_ref,
                           h1_ref, xm_ref, comb_ref, rt_ref, ct_ref):
    h1 = (h_ref[...]
          + jnp.dot(yabd_ref[...], wabd_ref[...], preferred_element_type=F32)
          + jnp.dot(yc_ref[...], wc_ref[...], preferred_element_type=F32))
    h1_ref[...] = h1
    xn = _rms_rows(h1, g_ref[...])
    x_hi = xn.astype(BF16)
    x_lo = (xn - x_hi.astype(F32)).astype(BF16)
    xm_ref[...] = x_hi
    r = (jnp.dot(x_hi, wr_ref[...], preferred_element_type=F32)
         + jnp.dot(x_lo, wr_ref[...], preferred_element_type=F32) + br_ref[...])
    rt_ref[...] = r.T

    def row(i):
        return rt_ref[i:i + 1, :] + rt_ref[ROUTER_LO + i:ROUTER_LO + i + 1, :]

    def first_hits(vals, target):
        hits, free = [], None
        for v in vals:
            f = jnp.where(v == target, 1.0, 0.0)
            hits.append(f if free is None else f * free)
            free = (1.0 - f) if free is None else free * (1.0 - f)
        return hits

    gl = [row(i) for i in range(N_GROUPS)]
    el = [row(8 + e) for e in range(N_EXPERTS)]
    gmax = functools.reduce(jnp.maximum, gl)
    gsel = first_hits(gl, gmax)
    pg_c = 1.0 / functools.reduce(lambda a, b: a + b, [jnp.exp(v - gmax) for v in gl])
    elc = []
    for kk in range(EXPERTS_PER_GROUP):
        v = gsel[0] * el[kk]
        for g in range(1, N_GROUPS):
            v = v + gsel[g] * el[g * EXPERTS_PER_GROUP + kk]
        elc.append(v)
    emax = functools.reduce(jnp.maximum, elc)
    ex = [jnp.exp(v - emax) for v in elc]
    se = functools.reduce(lambda a, b: a + b, ex)
    pe = [v / se for v in ex]
    p1 = functools.reduce(jnp.maximum, pe)
    t1 = first_hits(pe, p1)
    rest = [jnp.where(t > 0.0, -1.0, v) for t, v in zip(t1, pe)]
    p2 = functools.reduce(jnp.maximum, rest)
    t2 = first_hits(rest, p2)
    den = p1 + p2
    w1 = pg_c * (p1 / den)
    w2 = pg_c * (p2 / den)
    wk = [a * w1 + b * w2 for a, b in zip(t1, t2)]
    ct_ref[...] = jnp.zeros(ct_ref.shape, F32)
    for g in range(N_GROUPS):
        for kk in range(EXPERTS_PER_GROUP):
            e = g * EXPERTS_PER_GROUP + kk
            ct_ref[e:e + 1, :] = gsel[g] * wk[kk]
    comb_ref[...] = ct_ref[...].T


def _outproj_router(h2d, yabd, yc, wabd, wc, gain, wr, br):
    tkn, d = h2d.shape
    tm = min(TOK_TILE, tkn)
    assert tkn % tm == 0
    row = lambda w: pl.BlockSpec((tm, w), lambda i: (i, 0))
    full = lambda shape: pl.BlockSpec(shape, lambda i: (0,) * len(shape))
    return pl.pallas_call(
        _outproj_router_kernel,
        grid=(tkn // tm,),
        in_specs=[row(d), row(3 * W_GROUP), row(W_GROUP),
                  full((3 * W_GROUP, d)), full((W_GROUP, d)), full((1, d)),
                  full((d, ROUTER_LANES)), full((1, ROUTER_LANES))],
        out_specs=[row(d), row(d), row(ROUTER_LANES)],
        out_shape=[jax.ShapeDtypeStruct((tkn, d), F32),
                   jax.ShapeDtypeStruct((tkn, d), BF16),
                   jax.ShapeDtypeStruct((tkn, ROUTER_LANES), F32)],
        scratch_shapes=[pltpu.VMEM((ROUTER_LANES, tm), F32), pltpu.VMEM((ROUTER_LANES, tm), F32)],
        compiler_params=_cparams(("arbitrary",)),
        name="outproj_router",
    )(h2d, yabd, yc, wabd, wc, gain, wr, br)


def _moe_kernel(x_ref, h1_ref, comb_ref, wgu_ref, wd_ref, o_ref):
    e = pl.program_id(1)
    gu = jnp.dot(x_ref[...], wgu_ref[...], preferred_element_type=F32)
    gate = gu[:, :D_EXPERT]
    hdn = gate * _sigmoid(gate) * gu[:, D_EXPERT:]
    comb = comb_ref[...]
    lane = lax.broadcasted_iota(jnp.int32, comb.shape, 1)
    c_e = jnp.sum(jnp.where(lane == e, comb, 0.0), axis=-1, keepdims=True)
    contrib = jnp.dot((hdn * c_e).astype(BF16), wd_ref[...], preferred_element_type=F32)

    @pl.when(e == 0)
    def _():
        o_ref[...] = h1_ref[...] + contrib

    @pl.when(e > 0)
    def _():
        o_ref[...] = o_ref[...] + contrib


def _moe(xm, h1, comb, wgu, wd):
    tkn, d = h1.shape
    tm = min(TOK_TILE, tkn)
    assert tkn % tm == 0
    return pl.pallas_call(
        _moe_kernel,
        grid=(tkn // tm, N_EXPERTS),
        in_specs=[pl.BlockSpec((tm, d), lambda i, e: (i, 0)),
                  pl.BlockSpec((tm, d), lambda i, e: (i, 0)),
                  pl.BlockSpec((tm, ROUTER_LANES), lambda i, e: (i, 0)),
                  pl.BlockSpec((None, d, 2 * D_EXPERT), lambda i, e: (e, 0, 0)),
                  pl.BlockSpec((None, D_EXPERT, d), lambda i, e: (e, 0, 0))],
        out_specs=pl.BlockSpec((tm, d), lambda i, e: (i, 0)),
        out_shape=jax.ShapeDtypeStruct((tkn, d), F32),
        compiler_params=_cparams(("arbitrary", "arbitrary")),
        name="moe_experts",
    )(xm, h1, comb, wgu, wd)


def _ple_kernel(h_ref, p_ref, g_ref, wg_ref, bg_ref, wp_ref, fg_ref, o_ref, *, final):
    h = h_ref[...]
    xn = _rms_rows(h, g_ref[...]).astype(BF16)
    gate = _sigmoid(jnp.dot(xn, wg_ref[...], preferred_element_type=F32) + bg_ref[...])
    out = h + gate * jnp.dot(p_ref[...].astype(BF16), wp_ref[...], preferred_element_type=F32)
    if final:
        out = _rms_rows(out, fg_ref[...])
    o_ref[...] = out


def _ple(h2d, p2d, gain, wg, bg, wp, fgain, final):
    tkn, d = h2d.shape
    pd = p2d.shape[1]
    tm = min(TOK_TILE, tkn)
    assert tkn % tm == 0
    full = lambda shape: pl.BlockSpec(shape, lambda i: (0,) * len(shape))
    return pl.pallas_call(
        functools.partial(_ple_kernel, final=final),
        grid=(tkn // tm,),
        in_specs=[pl.BlockSpec((tm, d), lambda i: (i, 0)), pl.BlockSpec((tm, pd), lambda i: (i, 0)),
                  full((1, d)), full((d, d)), full((1, d)), full((pd, d)), full((1, d))],
        out_specs=pl.BlockSpec((tm, d), lambda i: (i, 0)),
        out_shape=jax.ShapeDtypeStruct((tkn, d), F32),
        compiler_params=_cparams(("arbitrary",)),
        name="ple_gate",
    )(h2d, p2d, gain, wg, bg, wp, fgain)


def _rope_table(seq):
    pos = jnp.arange(seq, dtype=F32)
    inv = ROPE_THETA ** (-jnp.arange(0, DIFF_HEAD_DIM, 2, dtype=F32) / DIFF_HEAD_DIM)
    ang = pos[:, None] * inv[None, :]
    ang = jnp.concatenate([ang, ang], axis=-1)
    cos, sin = jnp.cos(ang), jnp.sin(ang)
    lo = jnp.arange(DIFF_HEAD_DIM) < DIFF_HEAD_DIM // 2
    sin_lo = jnp.where(lo, -sin, 0.0)
    sin_hi = jnp.where(lo, 0.0, sin)
    rep = 128 // DIFF_HEAD_DIM
    return jnp.concatenate([jnp.tile(t, (1, rep)) for t in (cos, sin_lo, sin_hi)], axis=1)


def _block_diag(w):
    g, c, dd = w.shape
    out = jnp.zeros((g * c, g * dd), w.dtype)
    for i in range(g):
        out = out.at[i * c:(i + 1) * c, i * dd:(i + 1) * dd].set(w[i])
    return out


def _router_weights(wg, bg, we, be):
    d = wg.shape[0]
    w = jnp.zeros((d, ROUTER_LANES), F32)
    w = w.at[:, 0:N_GROUPS].set(wg).at[:, 8:8 + N_EXPERTS].set(we)
    w_hi = w.astype(BF16)
    w_lo = (w - w_hi.astype(F32)).astype(BF16)
    wr = w_hi.at[:, ROUTER_LO:ROUTER_LO + 8 + N_EXPERTS].set(w_lo[:, 0:8 + N_EXPERTS])
    br = jnp.zeros((1, ROUTER_LANES), F32)
    br = br.at[0, 0:N_GROUPS].set(bg).at[0, 8:8 + N_EXPERTS].set(be)
    return wr, br


def kernel(x, p, mix_norm, w_in, conf_conv_w, conf_conv_b, conf_ln_g, conf_ln_b, pool_w, pool_b, pool_scale, diff_lam_q1, diff_lam_k1, diff_lam_q2, diff_lam_k2, diff_subln_g, sconv_w, w_out, ffn_norm, router_group_w, router_group_b, router_expert_w, router_expert_b, expert_w_gate, expert_w_up, expert_w_down, ple_norm, ple_gate_w, ple_gate_b, ple_proj, final_norm):
    bsz, seq, d = x.shape
    depth = w_in.shape[0]
    tkn = bsz * seq
    rope = _rope_table(seq)
    att_t = min(ATT_TILE, seq)
    row = lambda v: v.reshape(1, -1).astype(F32)

    h = x
    for i in range(depth):
        lam_init = 0.8 - 0.6 * math.exp(-0.3 * i)
        yabd, qt, k, vt = _inproj_mix(
            h, row(mix_norm[i]), w_in[i].astype(BF16), rope,
            conf_conv_w[i], row(conf_conv_b[i]), row(conf_ln_g[i]), row(conf_ln_b[i]),
            _block_diag(pool_w[i]).astype(BF16), row(pool_b[i]), row(pool_scale[i]), sconv_w[i])
        lam_params = jnp.stack([diff_lam_q1[i], diff_lam_k1[i], diff_lam_q2[i], diff_lam_k2[i]]).astype(F32)
        g_bcast = jnp.broadcast_to(diff_subln_g[i].astype(F32)[:, None], (DIFF_V_DIM, att_t))
        yc = _diff_attn(qt, k, vt, lam_params, g_bcast, lam_init)

        wo = w_out[i].astype(BF16)
        wabd = jnp.concatenate([wo[0:2 * W_GROUP], wo[3 * W_GROUP:]], axis=0)
        wc = wo[2 * W_GROUP:3 * W_GROUP]
        wr, br = _router_weights(router_group_w[i], router_group_b[i], router_expert_w[i], router_expert_b[i])
        h1, xm, comb = _outproj_router(
            h.reshape(tkn, d), yabd.reshape(tkn, 3 * W_GROUP), yc.reshape(tkn, W_GROUP),
            wabd, wc, row(ffn_norm[i]), wr, br)

        wgu = jnp.concatenate([expert_w_gate[i], expert_w_up[i]], axis=-1).astype(BF16)
        h2 = _moe(xm, h1, comb, wgu, expert_w_down[i].astype(BF16))

        h3 = _ple(h2, p[i].reshape(tkn, -1), row(ple_norm[i]), ple_gate_w[i].astype(BF16),
                  row(ple_gate_b[i]), ple_proj[i].astype(BF16), row(final_norm), final=(i == depth - 1))
        h = h3.reshape(bsz, seq, d)
    return h
```

```python
import functools
import math

import jax
import jax.numpy as jnp
from jax import lax
from jax.experimental import pallas as pl
from jax.experimental.pallas import tpu as pltpu

F32 = jnp.float32
BF16 = jnp.bfloat16

EPS = 1e-6
ROPE_THETA = 10000.0
W_GROUP = 256
CONF_KERNEL = 31
CONF_HIST = 32
POOL_HIST = 16
SCONV_KERNEL = 3
SCONV_HIST = 8
DIFF_HEADS = 4
DIFF_HEAD_DIM = 32
DIFF_V_DIM = 64
N_GROUPS = 4
EXPERTS_PER_GROUP = 4
N_EXPERTS = 16
D_EXPERT = 256
ROUTER_LANES = 128
ROUTER_LO = 32
NEG_BIG = -1e30
LOG2E = 1.4426950408889634

VMEM_LIMIT = 48 * 1024 * 1024

SEQ_TILE = 512
ROW_CHUNK = 64
ATT_TILE = 256
TOK_TILE = 512
MOE_TILE = 256
N_BUCKETS = 24
N_BUCKET_ROWS = 32


def _cparams(sem):
    return pltpu.CompilerParams(dimension_semantics=sem, vmem_limit_bytes=VMEM_LIMIT)


def _rms_rows(x, g):
    ms = jnp.mean(x * x, axis=-1, keepdims=True)
    return x * lax.rsqrt(ms + EPS) * g


def _sigmoid(x):
    return 1.0 / (1.0 + jnp.exp(-x))


def _inproj_mix_kernel(h_ref, g_ref, w_ref, rope_ref, cw_ref, cb_ref, lg_ref, lb_ref,
                       pw_ref, pb_ref, ps_ref, sw_ref,
                       yabd_ref, qt_ref, k_ref, vt_ref,
                       gbuf, pbuf, zbuf, *, tm):
    s = pl.program_id(1)

    @pl.when(s == 0)
    def _():
        gbuf[0:CONF_HIST, :] = jnp.zeros((CONF_HIST, W_GROUP), F32)
        pbuf[0:POOL_HIST, :] = jnp.zeros((POOL_HIST, W_GROUP), F32)
        zbuf[0:SCONV_HIST, :] = jnp.zeros((SCONV_HIST, W_GROUP), F32)

    n = _rms_rows(h_ref[...], g_ref[...]).astype(BF16)

    def proj(lo, hi):
        return jnp.dot(n, w_ref[:, lo:hi], preferred_element_type=F32)

    a = proj(0, 2 * W_GROUP)
    gbuf[CONF_HIST:CONF_HIST + tm, :] = a[:, :W_GROUP] * _sigmoid(a[:, W_GROUP:])
    base = CONF_HIST - (CONF_KERNEL - 1)
    for c in range(tm // ROW_CHUNK):
        r0 = c * ROW_CHUNK
        acc = jnp.broadcast_to(cb_ref[...], (ROW_CHUNK, W_GROUP))
        for j in range(CONF_KERNEL):
            acc = acc + cw_ref[j:j + 1, :] * gbuf[r0 + base + j:r0 + base + j + ROW_CHUNK, :]
        mu = jnp.mean(acc, axis=-1, keepdims=True)
        d = acc - mu
        var = jnp.mean(d * d, axis=-1, keepdims=True)
        y = d * lax.rsqrt(var + EPS) * lg_ref[...] + lb_ref[...]
        yabd_ref[r0:r0 + ROW_CHUNK, 0:W_GROUP] = (y * _sigmoid(y)).astype(BF16)
    gbuf[0:CONF_HIST, :] = gbuf[tm:tm + CONF_HIST, :]

    pbuf[POOL_HIST:POOL_HIST + tm, :] = proj(2 * W_GROUP, 3 * W_GROUP)
    lane = lax.broadcasted_iota(jnp.int32, (ROW_CHUNK, 128), 1)
    first = lane < 64
    for c in range(tm // ROW_CHUNK):
        r0 = c * ROW_CHUNK
        tpos = (s * tm + r0 + 1 + lax.broadcasted_iota(jnp.int32, (ROW_CHUNK, 128), 0)).astype(F32)
        halves = []
        for half, (w_small, w_big) in enumerate(((2, 4), (8, 16))):
            l0 = half * 128

            def ld(j):
                return pbuf[r0 + POOL_HIST - j:r0 + POOL_HIST - j + ROW_CHUNK, l0:l0 + 128]

            cur = ld(0)
            run = cur
            for j in range(1, w_small):
                run = run + ld(j)
            small = run
            for j in range(w_small, w_big):
                run = run + ld(j)
            cnt = jnp.where(first, jnp.minimum(tpos, float(w_small)), jnp.minimum(tpos, float(w_big)))
            halves.append(jnp.where(first, small, run) / cnt - cur)
        pc = jnp.concatenate(halves, axis=1).astype(BF16)
        yb = (jnp.dot(pc, pw_ref[...], preferred_element_type=F32) + pb_ref[...]) * ps_ref[...]
        yabd_ref[r0:r0 + ROW_CHUNK, W_GROUP:2 * W_GROUP] = yb.astype(BF16)
    pbuf[0:POOL_HIST, :] = pbuf[tm:tm + POOL_HIST, :]

    sc = proj(6 * W_GROUP, 9 * W_GROUP)
    zbuf[SCONV_HIST:SCONV_HIST + tm, :] = sc[:, W_GROUP:2 * W_GROUP] * sc[:, 2 * W_GROUP:]
    conv = None
    for j in range(SCONV_KERNEL):
        off = SCONV_HIST - (SCONV_KERNEL - 1) + j
        term = sw_ref[j:j + 1, :] * zbuf[off:off + tm, :]
        conv = term if conv is None else conv + term
    yabd_ref[:, 2 * W_GROUP:3 * W_GROUP] = (sc[:, :W_GROUP] * conv).astype(BF16)
    zbuf[0:SCONV_HIST, :] = zbuf[tm:tm + SCONV_HIST, :]

    qk = proj(3 * W_GROUP, 5 * W_GROUP)
    tab = rope_ref[...]
    cos = jnp.concatenate([tab[:, 0:128]] * 4, axis=1)
    sin_lo = jnp.concatenate([tab[:, 128:256]] * 4, axis=1)
    sin_hi = jnp.concatenate([tab[:, 256:384]] * 4, axis=1)
    half = DIFF_HEAD_DIM // 2
    width = 2 * W_GROUP
    qk = qk * cos + pltpu.roll(qk, width - half, axis=1) * sin_lo + pltpu.roll(qk, half, axis=1) * sin_hi
    q = qk[:, :W_GROUP] * (DIFF_HEAD_DIM ** -0.5 * LOG2E)
    qt_ref[...] = q.T.astype(BF16)
    k_ref[...] = qk[:, W_GROUP:].astype(BF16)
    vt_ref[...] = proj(5 * W_GROUP, 6 * W_GROUP).T.astype(BF16)


def _inproj_mix(h, gain, w_in, rope, cw, cb, lg, lb, pw_bd, pb, ps, sw):
    bsz, seq, d = h.shape
    tm = min(SEQ_TILE, seq)
    assert seq % tm == 0 and tm % ROW_CHUNK == 0 and tm >= CONF_HIST
    ncol = w_in.shape[1]
    full = lambda shape: pl.BlockSpec(shape, lambda b, s: (0,) * len(shape))
    return pl.pallas_call(
        functools.partial(_inproj_mix_kernel, tm=tm),
        grid=(bsz, seq // tm),
        in_specs=[
            pl.BlockSpec((None, tm, d), lambda b, s: (b, s, 0)),
            full((1, d)),
            full((d, ncol)),
            pl.BlockSpec((tm, 384), lambda b, s: (s, 0)),
            full((CONF_KERNEL, W_GROUP)), full((1, W_GROUP)), full((1, W_GROUP)), full((1, W_GROUP)),
            full((W_GROUP, W_GROUP)), full((1, W_GROUP)), full((1, W_GROUP)),
            full((SCONV_KERNEL, W_GROUP)),
        ],
        out_specs=[
            pl.BlockSpec((None, tm, 3 * W_GROUP), lambda b, s: (b, s, 0)),
            pl.BlockSpec((None, W_GROUP, tm), lambda b, s: (b, 0, s)),
            pl.BlockSpec((None, tm, W_GROUP), lambda b, s: (b, s, 0)),
            pl.BlockSpec((None, W_GROUP, tm), lambda b, s: (b, 0, s)),
        ],
        out_shape=[
            jax.ShapeDtypeStruct((bsz, seq, 3 * W_GROUP), BF16),
            jax.ShapeDtypeStruct((bsz, W_GROUP, seq), BF16),
            jax.ShapeDtypeStruct((bsz, seq, W_GROUP), BF16),
            jax.ShapeDtypeStruct((bsz, W_GROUP, seq), BF16),
        ],
        scratch_shapes=[
            pltpu.VMEM((tm + CONF_HIST, W_GROUP), F32),
            pltpu.VMEM((tm + POOL_HIST, W_GROUP), F32),
            pltpu.VMEM((tm + SCONV_HIST, W_GROUP), F32),
        ],
        compiler_params=_cparams(("arbitrary", "arbitrary")),
        name="inproj_mix",
    )(h, gain, w_in, rope, cw, cb, lg, lb, pw_bd, pb, ps, sw)


def _diff_attn_kernel(qt_ref, k_ref, vt_ref, lam_ref, g_ref, o_ref,
                      qm_ref, p_ref, m_ref, l_ref, acc_ref, *, t, lam_init):
    qi = pl.program_id(1)
    nhc = 2 * DIFF_HEADS

    qm_ref[...] = jnp.zeros(qm_ref.shape, BF16)
    for j in range(nhc):
        lo = j * DIFF_HEAD_DIM
        qm_ref[lo:lo + DIFF_HEAD_DIM, j * t:(j + 1) * t] = qt_ref[lo:lo + DIFF_HEAD_DIM, :]
    m_ref[...] = jnp.full(m_ref.shape, NEG_BIG, F32)
    l_ref[...] = jnp.zeros(l_ref.shape, F32)
    acc_ref[...] = jnp.zeros(acc_ref.shape, F32)

    def tile(kt, masked):
        k0 = pl.multiple_of(kt * t, t)
        s_all = jnp.dot(k_ref[pl.ds(k0, t), :], qm_ref[...], preferred_element_type=F32)
        if masked:
            above = (lax.broadcasted_iota(jnp.int32, (t, t), 0) > lax.broadcasted_iota(jnp.int32, (t, t), 1))
        alphas = []
        for j in range(nhc):
            sc = s_all[:, j * t:(j + 1) * t]
            if masked:
                sc = jnp.where(above, NEG_BIG, sc)
            m_old = m_ref[j]
            m_new = jnp.maximum(m_old, jnp.max(sc, axis=0, keepdims=True))
            alpha = jnp.exp2(m_old - m_new)
            p = jnp.exp2(sc - m_new)
            l_ref[j] = alpha * l_ref[j] + jnp.sum(p, axis=0, keepdims=True)
            m_ref[j] = m_new
            alphas.append(alpha)
            p_ref[:, j * t:(j + 1) * t] = p.astype(BF16)
        for hd in range(DIFF_HEADS):
            vth = vt_ref[hd * DIFF_V_DIM:(hd + 1) * DIFF_V_DIM, pl.ds(k0, t)]
            o = jnp.dot(vth, p_ref[:, 2 * hd * t:(2 * hd + 2) * t], preferred_element_type=F32)
            acc_ref[2 * hd] = alphas[2 * hd] * acc_ref[2 * hd] + o[:, :t]
            acc_ref[2 * hd + 1] = alphas[2 * hd + 1] * acc_ref[2 * hd + 1] + o[:, t:]

    def body(kt, carry):
        tile(kt, False)
        return carry

    lax.fori_loop(0, qi, body, 0)
    tile(qi, True)

    lp = lam_ref[...]
    lam = (jnp.exp(jnp.sum(lp[0:1] * lp[1:2], axis=-1, keepdims=True))
           - jnp.exp(jnp.sum(lp[2:3] * lp[3:4], axis=-1, keepdims=True)) + lam_init)
    outs = []
    for hd in range(DIFF_HEADS):
        o1 = acc_ref[2 * hd] / l_ref[2 * hd]
        o2 = acc_ref[2 * hd + 1] / l_ref[2 * hd + 1]
        o = o1 - lam * o2
        ms = jnp.mean(o * o, axis=0, keepdims=True)
        outs.append(o * lax.rsqrt(ms + EPS) * g_ref[...] * (1.0 - lam_init))
    o_ref[...] = jnp.concatenate(outs, axis=0).T.astype(BF16)


def _diff_attn(qt, k, vt, lam_params, g_bcast, lam_init):
    bsz, seq, _ = k.shape
    t = min(ATT_TILE, seq)
    assert seq % t == 0
    nhc = 2 * DIFF_HEADS
    return pl.pallas_call(
        functools.partial(_diff_attn_kernel, t=t, lam_init=lam_init),
        grid=(bsz, seq // t),
        in_specs=[
            pl.BlockSpec((None, W_GROUP, t), lambda b, q: (b, 0, q)),
            pl.BlockSpec((None, seq, W_GROUP), lambda b, q: (b, 0, 0)),
            pl.BlockSpec((None, W_GROUP, seq), lambda b, q: (b, 0, 0)),
            pl.BlockSpec((4, DIFF_HEAD_DIM), lambda b, q: (0, 0)),
            pl.BlockSpec((DIFF_V_DIM, t), lambda b, q: (0, 0)),
        ],
        out_specs=pl.BlockSpec((None, t, W_GROUP), lambda b, q: (b, q, 0)),
        out_shape=jax.ShapeDtypeStruct((bsz, seq, W_GROUP), BF16),
        scratch_shapes=[
            pltpu.VMEM((W_GROUP, nhc * t), BF16),
            pltpu.VMEM((t, nhc * t), BF16),
            pltpu.VMEM((nhc, 1, t), F32),
            pltpu.VMEM((nhc, 1, t), F32),
            pltpu.VMEM((nhc, DIFF_V_DIM, t), F32),
        ],
        compiler_params=_cparams(("arbitrary", "arbitrary")),
        name="diff_attn",
    )(qt, k, vt, lam_params, g_bcast)


def _outproj_router_kernel(h_ref, yabd_ref, yc_ref, wabd_ref, wc_ref, g_ref, wr_ref, br_ref,
                           h1_ref, xs_ref, meta_ref, cnt_ref, rt_ref, ct_ref, oh_ref, carry_ref, *, tm):
    d = h_ref.shape[1]

    @pl.when(pl.program_id(0) == 0)
    def _():
        carry_ref[...] = jnp.zeros(carry_ref.shape, F32)

    h1 = (h_ref[...]
          + jnp.dot(yabd_ref[...], wabd_ref[...], preferred_element_type=F32)
          + jnp.dot(yc_ref[...], wc_ref[...], preferred_element_type=F32))
    h1_ref[...] = h1
    xn = _rms_rows(h1, g_ref[...])
    x_hi = xn.astype(BF16)
    x_lo = (xn - x_hi.astype(F32)).astype(BF16)
    xs_ref[:, 0:d] = xn
    r = (jnp.dot(x_hi, wr_ref[...], preferred_element_type=F32)
         + jnp.dot(x_lo, wr_ref[...], preferred_element_type=F32) + br_ref[...])
    rt_ref[...] = r.T

    def row(i):
        return rt_ref[i:i + 1, :] + rt_ref[ROUTER_LO + i:ROUTER_LO + i + 1, :]

    def first_hits(vals, target):
        hits, free = [], None
        for v in vals:
            f = jnp.where(v == target, 1.0, 0.0)
            hits.append(f if free is None else f * free)
            free = (1.0 - f) if free is None else free * (1.0 - f)
        return hits

    add = lambda a, b: a + b
    gl = [row(i) for i in range(N_GROUPS)]
    el = [row(8 + e) for e in range(N_EXPERTS)]
    gmax = functools.reduce(jnp.maximum, gl)
    gsel = first_hits(gl, gmax)
    pg_c = 1.0 / functools.reduce(add, [jnp.exp(v - gmax) for v in gl])
    elc = []
    for kk in range(EXPERTS_PER_GROUP):
        v = gsel[0] * el[kk]
        for g in range(1, N_GROUPS):
            v = v + gsel[g] * el[g * EXPERTS_PER_GROUP + kk]
        elc.append(v)
    emax = functools.reduce(jnp.maximum, elc)
    ex = [jnp.exp(v - emax) for v in elc]
    se = functools.reduce(add, ex)
    pe = [v / se for v in ex]
    p1 = functools.reduce(jnp.maximum, pe)
    t1 = first_hits(pe, p1)
    rest = [jnp.where(t > 0.0, -1.0, v) for t, v in zip(t1, pe)]
    p2 = functools.reduce(jnp.maximum, rest)
    t2 = first_hits(rest, p2)
    den = p1 + p2
    w1 = pg_c * (p1 / den)
    w2 = pg_c * (p2 / den)
    wk = [a * w1 + b * w2 for a, b in zip(t1, t2)]
    sel = [a + b for a, b in zip(t1, t2)]
    lower = first_hits(sel, 1.0)
    c_a = functools.reduce(add, [f * w for f, w in zip(lower, wk)])
    c_b = functools.reduce(add, [(s - f) * w for s, f, w in zip(sel, lower, wk)])
    pair = (sel[0] * sel[2] + 2.0 * sel[0] * sel[3] + 3.0 * sel[1] * sel[2]
            + 4.0 * sel[1] * sel[3] + 5.0 * sel[2] * sel[3])
    bucket = 6.0 * (gsel[1] + 2.0 * gsel[2] + 3.0 * gsel[3]) + pair

    ct_ref[...] = jnp.zeros(ct_ref.shape, F32)
    ct_ref[0:1, :] = c_a
    ct_ref[1:2, :] = c_b
    xs_ref[:, d:d + ROUTER_LANES] = ct_ref[...].T

    for b in range(N_BUCKET_ROWS):
        oh_ref[b:b + 1, :] = jnp.where(bucket == float(b), 1.0, 0.0)
    oh = oh_ref[...]
    upper = (lax.broadcasted_iota(jnp.int32, (tm, tm), 0) <= lax.broadcasted_iota(jnp.int32, (tm, tm), 1))
    cum = jnp.dot(oh.astype(BF16), jnp.where(upper, 1.0, 0.0).astype(BF16), preferred_element_type=F32)
    carry = carry_ref[...]
    rank = jnp.sum(oh * (cum - 1.0 + carry), axis=0, keepdims=True)
    carry_new = carry + jnp.sum(oh, axis=1, keepdims=True)
    carry_ref[...] = carry_new
    cnt_ref[...] = carry_new[:, 0:ROUTER_LANES]
    meta_ref[...] = jnp.zeros(meta_ref.shape, jnp.int32)
    meta_ref[0:1, :] = bucket.astype(jnp.int32)
    meta_ref[1:2, :] = rank.astype(jnp.int32)


def _outproj_router(h2d, yabd, yc, wabd, wc, gain, wr, br):
    tkn, d = h2d.shape
    tm = min(TOK_TILE, tkn)
    assert tkn % tm == 0
    row = lambda w: pl.BlockSpec((tm, w), lambda i: (i, 0))
    full = lambda shape: pl.BlockSpec(shape, lambda i: (0,) * len(shape))
    return pl.pallas_call(
        functools.partial(_outproj_router_kernel, tm=tm),
        grid=(tkn // tm,),
        in_specs=[row(d), row(3 * W_GROUP), row(W_GROUP),
                  full((3 * W_GROUP, d)), full((W_GROUP, d)), full((1, d)),
                  full((d, ROUTER_LANES)), full((1, ROUTER_LANES))],
        out_specs=[row(d), row(d + ROUTER_LANES),
                   pl.BlockSpec((None, 8, tm), lambda i: (i, 0, 0)),
                   full((N_BUCKET_ROWS, ROUTER_LANES))],
        out_shape=[jax.ShapeDtypeStruct((tkn, d), F32),
                   jax.ShapeDtypeStruct((tkn, d + ROUTER_LANES), F32),
                   jax.ShapeDtypeStruct((tkn // tm, 8, tm), jnp.int32),
                   jax.ShapeDtypeStruct((N_BUCKET_ROWS, ROUTER_LANES), F32)],
        scratch_shapes=[pltpu.VMEM((ROUTER_LANES, tm), F32), pltpu.VMEM((ROUTER_LANES, tm), F32),
                        pltpu.VMEM((N_BUCKET_ROWS, tm), F32), pltpu.VMEM((N_BUCKET_ROWS, tm), F32)],
        compiler_params=_cparams(("arbitrary",)),
        name="outproj_router",
    )(h2d, yabd, yc, wabd, wc, gain, wr, br)


def _moe_kernel(vt_ref, vea_ref, veb_ref, vlo_ref, vhi_ref, vfl_ref,
                tok_ref, tokn_ref, xs_hbm, wgua_ref, wgub_ref, wda_ref, wdb_ref,
                y_hbm, xbuf, ybuf, gsem, ssem, *, tm, n_tiles, n_visits):
    v = pl.program_id(0)
    tile = vt_ref[v]
    flags = vfl_ref[v]
    first = (flags & 1) != 0
    last = (flags & 2) != 0
    valid = (flags & 4) != 0
    slot = tile % 2
    d = y_hbm.shape[1]

    def row_gather(ids_ref, s, i):
        return pltpu.make_async_copy(xs_hbm.at[pl.ds(ids_ref[0, i], 1), :], xbuf.at[s, pl.ds(i, 1), :], gsem.at[s])

    def row_scatter(ids_ref, s, i):
        return pltpu.make_async_copy(ybuf.at[s, pl.ds(i, 1), :], y_hbm.at[pl.ds(ids_ref[0, i], 1), :], ssem.at[s])

    def start_all(mk, ids_ref, s):
        def body(i, c):
            mk(ids_ref, s, i).start()
            return c
        lax.fori_loop(0, tm, body, 0, unroll=8)

    def wait_gather(s):
        pltpu.make_async_copy(xs_hbm.at[pl.ds(0, tm), :], xbuf.at[s], gsem.at[s]).wait()

    def wait_scatter(s):
        pltpu.make_async_copy(ybuf.at[s], y_hbm.at[pl.ds(0, tm), :], ssem.at[s]).wait()

    @pl.when(v == 0)
    def _():
        start_all(row_gather, tok_ref, 0)

    @pl.when(jnp.logical_and(first, valid))
    def _():
        wait_gather(slot)

        @pl.when(tile + 1 < n_tiles)
        def _():
            start_all(row_gather, tokn_ref, 1 - slot)

        @pl.when(tile >= 2)
        def _():
            wait_scatter(slot)

    @pl.when(valid)
    def _():
        rows = tile * tm + lax.broadcasted_iota(jnp.int32, (tm, 1), 0)
        inb = jnp.where(jnp.logical_and(rows >= vlo_ref[v], rows < vhi_ref[v]), 1.0, 0.0)
        xt = xbuf[slot]
        x = xt[:, 0:d].astype(BF16)
        c_a = xt[:, d:d + 1] * inb
        c_b = xt[:, d + 1:d + 2] * inb

        def expert(wgu_ref, wd_ref, c):
            gu = jnp.dot(x, wgu_ref[...], preferred_element_type=F32)
            gate = gu[:, :D_EXPERT]
            hdn = gate * _sigmoid(gate) * gu[:, D_EXPERT:] * c
            return jnp.dot(hdn.astype(BF16), wd_ref[...], preferred_element_type=F32)

        y = expert(wgua_ref, wda_ref, c_a) + expert(wgub_ref, wdb_ref, c_b)

        @pl.when(first)
        def _():
            ybuf[slot] = y

        @pl.when(jnp.logical_not(first))
        def _():
            ybuf[slot] = ybuf[slot] + y

    @pl.when(jnp.logical_and(last, valid))
    def _():
        start_all(row_scatter, tok_ref, slot)

    @pl.when(v == n_visits - 1)
    def _():
        wait_scatter(0)
        wait_scatter(1)


def _moe(xs, tok_of_slot, sched, wgu, wd):
    tkn, dx = xs.shape
    d = dx - ROUTER_LANES
    tm = MOE_TILE
    assert tkn % tm == 0 and tkn // tm >= 2
    n_tiles = tkn // tm
    n_visits = sched[0].shape[0]
    tok3 = tok_of_slot.reshape(n_tiles, 1, tm)
    wspec = lambda shape, which: pl.BlockSpec(
        (None,) + shape, lambda v, vt, vea, veb, vlo, vhi, vfl: ((vea, veb)[which][v], 0, 0))
    grid_spec = pltpu.PrefetchScalarGridSpec(
        num_scalar_prefetch=6,
        grid=(n_visits,),
        in_specs=[
            pl.BlockSpec((None, 1, tm), lambda v, vt, *_: (vt[v], 0, 0), memory_space=pltpu.SMEM),
            pl.BlockSpec((None, 1, tm), lambda v, vt, *_: (jnp.minimum(vt[v] + 1, n_tiles - 1), 0, 0),
                         memory_space=pltpu.SMEM),
            pl.BlockSpec(memory_space=pl.ANY),
            wspec((d, 2 * D_EXPERT), 0), wspec((d, 2 * D_EXPERT), 1),
            wspec((D_EXPERT, d), 0), wspec((D_EXPERT, d), 1),
        ],
        out_specs=pl.BlockSpec(memory_space=pl.ANY),
        scratch_shapes=[pltpu.VMEM((2, tm, dx), F32), pltpu.VMEM((2, tm, d), F32),
                        pltpu.SemaphoreType.DMA((2,)), pltpu.SemaphoreType.DMA((2,))],
    )
    return pl.pallas_call(
        functools.partial(_moe_kernel, tm=tm, n_tiles=n_tiles, n_visits=n_visits),
        grid_spec=grid_spec,
        out_shape=jax.ShapeDtypeStruct((tkn, d), F32),
        compiler_params=_cparams(("arbitrary",)),
        name="moe_experts",
    )(*sched, tok3, tok3, xs, wgu, wgu, wd, wd)


def _moe_schedule(bucket, rank, counts, tm):
    tkn = bucket.shape[0]
    n_tiles = tkn // tm
    n_visits = n_tiles + N_BUCKETS
    ends = jnp.cumsum(counts)
    offs = ends - counts
    slot = offs[bucket] + rank
    tok_of_slot = jnp.zeros((tkn,), jnp.int32).at[slot].set(jnp.arange(tkn, dtype=jnp.int32))
    t_first = offs // tm
    t_last = jnp.maximum(ends - 1, 0) // tm
    nv = jnp.where(counts > 0, t_last - t_first + 1, 0)
    v_end = jnp.cumsum(nv)
    v_start = v_end - nv
    total = v_end[-1]
    v = jnp.arange(n_visits, dtype=jnp.int32)
    vc = jnp.minimum(v, total - 1)
    b = jnp.searchsorted(v_end, vc, side="right").astype(jnp.int32)
    tile = (t_first[b] + vc - v_start[b]).astype(jnp.int32)
    valid = v < total
    prev_tile = jnp.concatenate([jnp.full((1,), -1, jnp.int32), tile[:-1]])
    next_tile = jnp.concatenate([tile[1:], jnp.full((1,), -1, jnp.int32)])
    first = tile != prev_tile
    last = jnp.logical_or(tile != next_tile, v == total - 1)
    flags = (first.astype(jnp.int32) + 2 * last.astype(jnp.int32) + 4) * valid.astype(jnp.int32)
    g, pr = b // 6, b % 6
    lo_e = jnp.array([0, 0, 0, 1, 1, 2], jnp.int32)[pr]
    hi_e = jnp.array([1, 2, 3, 2, 3, 3], jnp.int32)[pr]
    ea = g * EXPERTS_PER_GROUP + lo_e
    eb = g * EXPERTS_PER_GROUP + hi_e
    sched = (tile, ea.astype(jnp.int32), eb.astype(jnp.int32),
             offs[b].astype(jnp.int32), ends[b].astype(jnp.int32), flags.astype(jnp.int32))
    return tok_of_slot, sched


def _ple_kernel(h_ref, y_ref, p_ref, g_ref, wg_ref, bg_ref, wp_ref, fg_ref, o_ref, *, final):
    h = h_ref[...] + y_ref[...]
    xn = _rms_rows(h, g_ref[...]).astype(BF16)
    gate = _sigmoid(jnp.dot(xn, wg_ref[...], preferred_element_type=F32) + bg_ref[...])
    out = h + gate * jnp.dot(p_ref[...].astype(BF16), wp_ref[...], preferred_element_type=F32)
    if final:
        out = _rms_rows(out, fg_ref[...])
    o_ref[...] = out


def _ple(h2d, y2d, p2d, gain, wg, bg, wp, fgain, final):
    tkn, d = h2d.shape
    pd = p2d.shape[1]
    tm = min(TOK_TILE, tkn)
    assert tkn % tm == 0
    full = lambda shape: pl.BlockSpec(shape, lambda i: (0,) * len(shape))
    return pl.pallas_call(
        functools.partial(_ple_kernel, final=final),
        grid=(tkn // tm,),
        in_specs=[pl.BlockSpec((tm, d), lambda i: (i, 0)), pl.BlockSpec((tm, d), lambda i: (i, 0)),
                  pl.BlockSpec((tm, pd), lambda i: (i, 0)),
                  full((1, d)), full((d, d)), full((1, d)), full((pd, d)), full((1, d))],
        out_specs=pl.BlockSpec((tm, d), lambda i: (i, 0)),
        out_shape=jax.ShapeDtypeStruct((tkn, d), F32),
        compiler_params=_cparams(("arbitrary",)),
        name="ple_gate",
    )(h2d, y2d, p2d, gain, wg, bg, wp, fgain)


def _rope_table(seq):
    pos = jnp.arange(seq, dtype=F32)
    inv = ROPE_THETA ** (-jnp.arange(0, DIFF_HEAD_DIM, 2, dtype=F32) / DIFF_HEAD_DIM)
    ang = pos[:, None] * inv[None, :]
    ang = jnp.concatenate([ang, ang], axis=-1)
    cos, sin = jnp.cos(ang), jnp.sin(ang)
    lo = jnp.arange(DIFF_HEAD_DIM) < DIFF_HEAD_DIM // 2
    sin_lo = jnp.where(lo, -sin, 0.0)
    sin_hi = jnp.where(lo, 0.0, sin)
    rep = 128 // DIFF_HEAD_DIM
    return jnp.concatenate([jnp.tile(t, (1, rep)) for t in (cos, sin_lo, sin_hi)], axis=1)


def _block_diag(w):
    g, c, dd = w.shape
    out = jnp.zeros((g * c, g * dd), w.dtype)
    for i in range(g):
        out = out.at[i * c:(i + 1) * c, i * dd:(i + 1) * dd].set(w[i])
    return out


def _router_weights(wg, bg, we, be):
    d = wg.shape[0]
    w = jnp.zeros((d, ROUTER_LANES), F32)
    w = w.at[:, 0:N_GROUPS].set(wg).at[:, 8:8 + N_EXPERTS].set(we)
    w_hi = w.astype(BF16)
    w_lo = (w - w_hi.astype(F32)).astype(BF16)
    wr = w_hi.at[:, ROUTER_LO:ROUTER_LO + 8 + N_EXPERTS].set(w_lo[:, 0:8 + N_EXPERTS])
    br = jnp.zeros((1, ROUTER_LANES), F32)
    br = br.at[0, 0:N_GROUPS].set(bg).at[0, 8:8 + N_EXPERTS].set(be)
    return wr, br


def kernel(x, p, mix_norm, w_in, conf_conv_w, conf_conv_b, conf_ln_g, conf_ln_b, pool_w, pool_b, pool_scale, diff_lam_q1, diff_lam_k1, diff_lam_q2, diff_lam_k2, diff_subln_g, sconv_w, w_out, ffn_norm, router_group_w, router_group_b, router_expert_w, router_expert_b, expert_w_gate, expert_w_up, expert_w_down, ple_norm, ple_gate_w, ple_gate_b, ple_proj, final_norm):
    bsz, seq, d = x.shape
    depth = w_in.shape[0]
    tkn = bsz * seq
    rope = _rope_table(seq)
    att_t = min(ATT_TILE, seq)
    row = lambda v: v.reshape(1, -1).astype(F32)

    h = x
    for i in range(depth):
        lam_init = 0.8 - 0.6 * math.exp(-0.3 * i)
        yabd, qt, k, vt = _inproj_mix(
            h, row(mix_norm[i]), w_in[i].astype(BF16), rope,
            conf_conv_w[i], row(conf_conv_b[i]), row(conf_ln_g[i]), row(conf_ln_b[i]),
            _block_diag(pool_w[i]).astype(BF16), row(pool_b[i]), row(pool_scale[i]), sconv_w[i])
        lam_params = jnp.stack([diff_lam_q1[i], diff_lam_k1[i], diff_lam_q2[i], diff_lam_k2[i]]).astype(F32)
        g_bcast = jnp.broadcast_to(diff_subln_g[i].astype(F32)[:, None], (DIFF_V_DIM, att_t))
        yc = _diff_attn(qt, k, vt, lam_params, g_bcast, lam_init)

        wo = w_out[i].astype(BF16)
        wabd = jnp.concatenate([wo[0:2 * W_GROUP], wo[3 * W_GROUP:]], axis=0)
        wc = wo[2 * W_GROUP:3 * W_GROUP]
        wr, br = _router_weights(router_group_w[i], router_group_b[i], router_expert_w[i], router_expert_b[i])
        h1, xs, meta, cnt = _outproj_router(
            h.reshape(tkn, d), yabd.reshape(tkn, 3 * W_GROUP), yc.reshape(tkn, W_GROUP),
            wabd, wc, row(ffn_norm[i]), wr, br)

        tok_of_slot, sched = _moe_schedule(
            meta[:, 0, :].reshape(tkn), meta[:, 1, :].reshape(tkn),
            cnt[:N_BUCKETS, 0].astype(jnp.int32), MOE_TILE)
        wgu = jnp.concatenate([expert_w_gate[i], expert_w_up[i]], axis=-1).astype(BF16)
        y = _moe(xs, tok_of_slot, sched, wgu, expert_w_down[i].astype(BF16))

        h3 = _ple(h1, y, p[i].reshape(tkn, -1), row(ple_norm[i]), ple_gate_w[i].astype(BF16),
                  row(ple_gate_b[i]), ple_proj[i].astype(BF16), row(final_norm), final=(i == depth - 1))
        h = h3.reshape(bsz, seq, d)
    return h
```

```python
import functools
import math

import jax
import jax.numpy as jnp
from jax import lax
from jax.experimental import pallas as pl
from jax.experimental.pallas import tpu as pltpu

F32 = jnp.float32
BF16 = jnp.bfloat16

EPS = 1e-6
ROPE_THETA = 10000.0
W_GROUP = 256
CONF_KERNEL = 31
CONF_HIST = 32
POOL_HIST = 16
SCONV_KERNEL = 3
SCONV_HIST = 8
DIFF_HEADS = 4
DIFF_HEAD_DIM = 32
DIFF_V_DIM = 64
V_AUG_ROWS = 80
N_GROUPS = 4
EXPERTS_PER_GROUP = 4
N_EXPERTS = 16
D_EXPERT = 256
ROUTER_LANES = 128
ROUTER_LO = 32
NEG_BIG = -1e30
LOG2E = 1.4426950408889634

VMEM_LIMIT = 48 * 1024 * 1024

SEQ_TILE = 512
ROW_CHUNK = 64
ATT_TILE = 256
TOK_TILE = 512
MOE_TILE = 256
N_BUCKETS = 24
N_BUCKET_ROWS = 32


def _cparams(sem):
    return pltpu.CompilerParams(dimension_semantics=sem, vmem_limit_bytes=VMEM_LIMIT)


def _rms_rows(x, g):
    ms = jnp.mean(x * x, axis=-1, keepdims=True)
    return x * lax.rsqrt(ms + EPS) * g


def _sigmoid(x):
    return 1.0 / (1.0 + jnp.exp(-x))


def _inproj_mix_kernel(h_ref, g_ref, w_ref, rope_ref, cw_ref, cb_ref, lg_ref, lb_ref,
                       pw_ref, pb_ref, ps_ref, sw_ref,
                       yabd_ref, qt_ref, k_ref, vt_ref,
                       gbuf, pbuf, zbuf, *, tm):
    s = pl.program_id(1)

    @pl.when(s == 0)
    def _():
        gbuf[0:CONF_HIST, :] = jnp.zeros((CONF_HIST, W_GROUP), F32)
        pbuf[0:POOL_HIST, :] = jnp.zeros((POOL_HIST, W_GROUP), F32)
        zbuf[0:SCONV_HIST, :] = jnp.zeros((SCONV_HIST, W_GROUP), F32)

    n = _rms_rows(h_ref[...], g_ref[...]).astype(BF16)

    def proj(lo, hi):
        return jnp.dot(n, w_ref[:, lo:hi], preferred_element_type=F32)

    a = proj(0, 2 * W_GROUP)
    gbuf[CONF_HIST:CONF_HIST + tm, :] = a[:, :W_GROUP] * _sigmoid(a[:, W_GROUP:])
    base = CONF_HIST - (CONF_KERNEL - 1)
    for c in range(tm // ROW_CHUNK):
        r0 = c * ROW_CHUNK
        acc = jnp.broadcast_to(cb_ref[...], (ROW_CHUNK, W_GROUP))
        for j in range(CONF_KERNEL):
            acc = acc + cw_ref[j:j + 1, :] * gbuf[r0 + base + j:r0 + base + j + ROW_CHUNK, :]
        mu = jnp.mean(acc, axis=-1, keepdims=True)
        d = acc - mu
        var = jnp.mean(d * d, axis=-1, keepdims=True)
        y = d * lax.rsqrt(var + EPS) * lg_ref[...] + lb_ref[...]
        yabd_ref[r0:r0 + ROW_CHUNK, 0:W_GROUP] = (y * _sigmoid(y)).astype(BF16)
    gbuf[0:CONF_HIST, :] = gbuf[tm:tm + CONF_HIST, :]

    pbuf[POOL_HIST:POOL_HIST + tm, :] = proj(2 * W_GROUP, 3 * W_GROUP)
    lane = lax.broadcasted_iota(jnp.int32, (ROW_CHUNK, 128), 1)
    first = lane < 64
    for c in range(tm // ROW_CHUNK):
        r0 = c * ROW_CHUNK
        tpos = (s * tm + r0 + 1 + lax.broadcasted_iota(jnp.int32, (ROW_CHUNK, 128), 0)).astype(F32)
        halves = []
        for half, (w_small, w_big) in enumerate(((2, 4), (8, 16))):
            l0 = half * 128

            def ld(j):
                return pbuf[r0 + POOL_HIST - j:r0 + POOL_HIST - j + ROW_CHUNK, l0:l0 + 128]

            cur = ld(0)
            run = cur
            for j in range(1, w_small):
                run = run + ld(j)
            small = run
            for j in range(w_small, w_big):
                run = run + ld(j)
            cnt = jnp.where(first, jnp.minimum(tpos, float(w_small)), jnp.minimum(tpos, float(w_big)))
            halves.append(jnp.where(first, small, run) / cnt - cur)
        pc = jnp.concatenate(halves, axis=1).astype(BF16)
        yb = (jnp.dot(pc, pw_ref[...], preferred_element_type=F32) + pb_ref[...]) * ps_ref[...]
        yabd_ref[r0:r0 + ROW_CHUNK, W_GROUP:2 * W_GROUP] = yb.astype(BF16)
    pbuf[0:POOL_HIST, :] = pbuf[tm:tm + POOL_HIST, :]

    sc = proj(6 * W_GROUP, 9 * W_GROUP)
    zbuf[SCONV_HIST:SCONV_HIST + tm, :] = sc[:, W_GROUP:2 * W_GROUP] * sc[:, 2 * W_GROUP:]
    conv = None
    for j in range(SCONV_KERNEL):
        off = SCONV_HIST - (SCONV_KERNEL - 1) + j
        term = sw_ref[j:j + 1, :] * zbuf[off:off + tm, :]
        conv = term if conv is None else conv + term
    yabd_ref[:, 2 * W_GROUP:3 * W_GROUP] = (sc[:, :W_GROUP] * conv).astype(BF16)
    zbuf[0:SCONV_HIST, :] = zbuf[tm:tm + SCONV_HIST, :]

    qk = proj(3 * W_GROUP, 5 * W_GROUP)
    tab = rope_ref[...]
    cos = jnp.concatenate([tab[:, 0:128]] * 4, axis=1)
    sin_lo = jnp.concatenate([tab[:, 128:256]] * 4, axis=1)
    sin_hi = jnp.concatenate([tab[:, 256:384]] * 4, axis=1)
    half = DIFF_HEAD_DIM // 2
    width = 2 * W_GROUP
    qk = qk * cos + pltpu.roll(qk, width - half, axis=1) * sin_lo + pltpu.roll(qk, half, axis=1) * sin_hi
    q = qk[:, :W_GROUP] * (DIFF_HEAD_DIM ** -0.5 * LOG2E)
    qt_ref[...] = q.T.astype(BF16)
    k_ref[...] = qk[:, W_GROUP:].astype(BF16)
    vt = proj(5 * W_GROUP, 6 * W_GROUP).T.astype(BF16)
    for hd in range(DIFF_HEADS):
        r0 = hd * V_AUG_ROWS
        vt_ref[r0:r0 + DIFF_V_DIM, :] = vt[hd * DIFF_V_DIM:(hd + 1) * DIFF_V_DIM, :]
        vt_ref[r0 + DIFF_V_DIM:r0 + V_AUG_ROWS, :] = jnp.ones((V_AUG_ROWS - DIFF_V_DIM, tm), BF16)


def _inproj_mix(h, gain, w_in, rope, cw, cb, lg, lb, pw_bd, pb, ps, sw):
    bsz, seq, d = h.shape
    tm = min(SEQ_TILE, seq)
    assert seq % tm == 0 and tm % ROW_CHUNK == 0 and tm >= CONF_HIST
    ncol = w_in.shape[1]
    full = lambda shape: pl.BlockSpec(shape, lambda b, s: (0,) * len(shape))
    return pl.pallas_call(
        functools.partial(_inproj_mix_kernel, tm=tm),
        grid=(bsz, seq // tm),
        in_specs=[
            pl.BlockSpec((None, tm, d), lambda b, s: (b, s, 0)),
            full((1, d)),
            full((d, ncol)),
            pl.BlockSpec((tm, 384), lambda b, s: (s, 0)),
            full((CONF_KERNEL, W_GROUP)), full((1, W_GROUP)), full((1, W_GROUP)), full((1, W_GROUP)),
            full((W_GROUP, W_GROUP)), full((1, W_GROUP)), full((1, W_GROUP)),
            full((SCONV_KERNEL, W_GROUP)),
        ],
        out_specs=[
            pl.BlockSpec((None, tm, 3 * W_GROUP), lambda b, s: (b, s, 0)),
            pl.BlockSpec((None, W_GROUP, tm), lambda b, s: (b, 0, s)),
            pl.BlockSpec((None, tm, W_GROUP), lambda b, s: (b, s, 0)),
            pl.BlockSpec((None, DIFF_HEADS * V_AUG_ROWS, tm), lambda b, s: (b, 0, s)),
        ],
        out_shape=[
            jax.ShapeDtypeStruct((bsz, seq, 3 * W_GROUP), BF16),
            jax.ShapeDtypeStruct((bsz, W_GROUP, seq), BF16),
            jax.ShapeDtypeStruct((bsz, seq, W_GROUP), BF16),
            jax.ShapeDtypeStruct((bsz, DIFF_HEADS * V_AUG_ROWS, seq), BF16),
        ],
        scratch_shapes=[
            pltpu.VMEM((tm + CONF_HIST, W_GROUP), F32),
            pltpu.VMEM((tm + POOL_HIST, W_GROUP), F32),
            pltpu.VMEM((tm + SCONV_HIST, W_GROUP), F32),
        ],
        compiler_params=_cparams(("arbitrary", "arbitrary")),
        name="inproj_mix",
    )(h, gain, w_in, rope, cw, cb, lg, lb, pw_bd, pb, ps, sw)


def _diff_attn_kernel(qt_ref, k_ref, vt_ref, lam_ref, g_ref, o_ref,
                      qm_ref, s0_ref, s1_ref, p0_ref, p1_ref, al0_ref, al1_ref, m_ref, l_ref, acc_ref,
                      *, t, lam_init):
    qi = pl.program_id(1)
    nhc = 2 * DIFF_HEADS

    qm_ref[...] = jnp.zeros(qm_ref.shape, BF16)
    for j in range(nhc):
        lo = j * DIFF_HEAD_DIM
        qm_ref[lo:lo + DIFF_HEAD_DIM, j * t:(j + 1) * t] = qt_ref[lo:lo + DIFF_HEAD_DIM, :]
    m_ref[...] = jnp.full(m_ref.shape, NEG_BIG, F32)
    l_ref[...] = jnp.zeros(l_ref.shape, F32)
    acc_ref[...] = jnp.zeros(acc_ref.shape, F32)
    p1_ref[...] = jnp.zeros(p1_ref.shape, BF16)
    al1_ref[...] = jnp.ones(al1_ref.shape, F32)

    def scores(kt, s_ref):
        k0 = pl.multiple_of(kt * t, t)
        s_ref[...] = jnp.dot(k_ref[pl.ds(k0, t), :], qm_ref[...], preferred_element_type=F32)

    def softmax(s_ref, p_ref, al_ref, masked):
        if masked:
            above = (lax.broadcasted_iota(jnp.int32, (t, t), 0) > lax.broadcasted_iota(jnp.int32, (t, t), 1))
        for j in range(nhc):
            sc = s_ref[:, j * t:(j + 1) * t]
            if masked:
                sc = jnp.where(above, NEG_BIG, sc)
            m_old = m_ref[j]
            m_new = jnp.maximum(m_old, jnp.max(sc, axis=0, keepdims=True))
            al_ref[j] = jnp.exp2(m_old - m_new)
            m_ref[j] = m_new
            p_ref[:, j * t:(j + 1) * t] = jnp.exp2(sc - m_new).astype(BF16)

    def weighted_values(p_ref, al_ref, kt):
        k0 = pl.multiple_of(jnp.maximum(kt, 0) * t, t)
        for hd in range(DIFF_HEADS):
            vth = vt_ref[hd * V_AUG_ROWS:(hd + 1) * V_AUG_ROWS, pl.ds(k0, t)]
            o = jnp.dot(vth, p_ref[:, 2 * hd * t:(2 * hd + 2) * t], preferred_element_type=F32)
            for c in range(2):
                j = 2 * hd + c
                alpha = al_ref[j]
                acc_ref[j] = alpha * acc_ref[j] + o[0:DIFF_V_DIM, c * t:(c + 1) * t]
                l_ref[j] = alpha * l_ref[j] + o[DIFF_V_DIM:DIFF_V_DIM + 1, c * t:(c + 1) * t]

    scores(0, s0_ref)

    def pair(i, carry):
        kt = 2 * i
        scores(kt + 1, s1_ref)
        softmax(s0_ref, p0_ref, al0_ref, False)
        weighted_values(p1_ref, al1_ref, kt - 1)
        scores(kt + 2, s0_ref)
        softmax(s1_ref, p1_ref, al1_ref, False)
        weighted_values(p0_ref, al0_ref, kt)
        return carry

    lax.fori_loop(0, qi // 2, pair, 0)

    @pl.when(qi % 2 == 0)
    def _():
        softmax(s0_ref, p0_ref, al0_ref, True)
        weighted_values(p1_ref, al1_ref, qi - 1)
        weighted_values(p0_ref, al0_ref, qi)

    @pl.when(qi % 2 == 1)
    def _():
        scores(qi, s1_ref)
        softmax(s0_ref, p0_ref, al0_ref, False)
        weighted_values(p1_ref, al1_ref, qi - 2)
        softmax(s1_ref, p1_ref, al1_ref, True)
        weighted_values(p0_ref, al0_ref, qi - 1)
        weighted_values(p1_ref, al1_ref, qi)

    lp = lam_ref[...]
    lam = (jnp.exp(jnp.sum(lp[0:1] * lp[1:2], axis=-1, keepdims=True))
           - jnp.exp(jnp.sum(lp[2:3] * lp[3:4], axis=-1, keepdims=True)) + lam_init)
    outs = []
    for hd in range(DIFF_HEADS):
        o1 = acc_ref[2 * hd] / l_ref[2 * hd]
        o2 = acc_ref[2 * hd + 1] / l_ref[2 * hd + 1]
        o = o1 - lam * o2
        ms = jnp.mean(o * o, axis=0, keepdims=True)
        outs.append(o * lax.rsqrt(ms + EPS) * g_ref[...] * (1.0 - lam_init))
    o_ref[...] = jnp.concatenate(outs, axis=0).T.astype(BF16)


def _diff_attn(qt, k, vt, lam_params, g_bcast, lam_init):
    bsz, seq, _ = k.shape
    t = min(ATT_TILE, seq)
    assert seq % t == 0
    nhc = 2 * DIFF_HEADS
    return pl.pallas_call(
        functools.partial(_diff_attn_kernel, t=t, lam_init=lam_init),
        grid=(bsz, seq // t),
        in_specs=[
            pl.BlockSpec((None, W_GROUP, t), lambda b, q: (b, 0, q)),
            pl.BlockSpec((None, seq, W_GROUP), lambda b, q: (b, 0, 0)),
            pl.BlockSpec((None, DIFF_HEADS * V_AUG_ROWS, seq), lambda b, q: (b, 0, 0)),
            pl.BlockSpec((4, DIFF_HEAD_DIM), lambda b, q: (0, 0)),
            pl.BlockSpec((DIFF_V_DIM, t), lambda b, q: (0, 0)),
        ],
        out_specs=pl.BlockSpec((None, t, W_GROUP), lambda b, q: (b, q, 0)),
        out_shape=jax.ShapeDtypeStruct((bsz, seq, W_GROUP), BF16),
        scratch_shapes=[
            pltpu.VMEM((W_GROUP, nhc * t), BF16),
            pltpu.VMEM((t, nhc * t), F32), pltpu.VMEM((t, nhc * t), F32),
            pltpu.VMEM((t, nhc * t), BF16), pltpu.VMEM((t, nhc * t), BF16),
            pltpu.VMEM((nhc, 1, t), F32), pltpu.VMEM((nhc, 1, t), F32),
            pltpu.VMEM((nhc, 1, t), F32),
            pltpu.VMEM((nhc, 1, t), F32),
            pltpu.VMEM((nhc, DIFF_V_DIM, t), F32),
        ],
        compiler_params=_cparams(("arbitrary", "arbitrary")),
        name="diff_attn",
    )(qt, k, vt, lam_params, g_bcast)


def _outproj_router_kernel(h_ref, yabd_ref, yc_ref, wabd_ref, wc_ref, g_ref, wr_ref, br_ref,
                           h1_ref, xs_ref, meta_ref, cnt_ref, rt_ref, ct_ref, oh_ref, carry_ref, *, tm):
    d = h_ref.shape[1]

    @pl.when(pl.program_id(0) == 0)
    def _():
        carry_ref[...] = jnp.zeros(carry_ref.shape, F32)

    h1 = (h_ref[...]
          + jnp.dot(yabd_ref[...], wabd_ref[...], preferred_element_type=F32)
          + jnp.dot(yc_ref[...], wc_ref[...], preferred_element_type=F32))
    h1_ref[...] = h1
    xn = _rms_rows(h1, g_ref[...])
    x_hi = xn.astype(BF16)
    x_lo = (xn - x_hi.astype(F32)).astype(BF16)
    xs_ref[:, 0:d] = xn
    r = (jnp.dot(x_hi, wr_ref[...], preferred_element_type=F32)
         + jnp.dot(x_lo, wr_ref[...], preferred_element_type=F32) + br_ref[...])
    rt_ref[...] = r.T

    def row(i):
        return rt_ref[i:i + 1, :] + rt_ref[ROUTER_LO + i:ROUTER_LO + i + 1, :]

    def first_hits(vals, target):
        hits, free = [], None
        for v in vals:
            f = jnp.where(v == target, 1.0, 0.0)
            hits.append(f if free is None else f * free)
            free = (1.0 - f) if free is None else free * (1.0 - f)
        return hits

    add = lambda a, b: a + b
    gl = [row(i) for i in range(N_GROUPS)]
    el = [row(8 + e) for e in range(N_EXPERTS)]
    gmax = functools.reduce(jnp.maximum, gl)
    gsel = first_hits(gl, gmax)
    pg_c = 1.0 / functools.reduce(add, [jnp.exp(v - gmax) for v in gl])
    elc = []
    for kk in range(EXPERTS_PER_GROUP):
        v = gsel[0] * el[kk]
        for g in range(1, N_GROUPS):
            v = v + gsel[g] * el[g * EXPERTS_PER_GROUP + kk]
        elc.append(v)
    emax = functools.reduce(jnp.maximum, elc)
    ex = [jnp.exp(v - emax) for v in elc]
    se = functools.reduce(add, ex)
    pe = [v / se for v in ex]
    p1 = functools.reduce(jnp.maximum, pe)
    t1 = first_hits(pe, p1)
    rest = [jnp.where(t > 0.0, -1.0, v) for t, v in zip(t1, pe)]
    p2 = functools.reduce(jnp.maximum, rest)
    t2 = first_hits(rest, p2)
    den = p1 + p2
    w1 = pg_c * (p1 / den)
    w2 = pg_c * (p2 / den)
    wk = [a * w1 + b * w2 for a, b in zip(t1, t2)]
    sel = [a + b for a, b in zip(t1, t2)]
    lower = first_hits(sel, 1.0)
    c_a = functools.reduce(add, [f * w for f, w in zip(lower, wk)])
    c_b = functools.reduce(add, [(s - f) * w for s, f, w in zip(sel, lower, wk)])
    pair = (sel[0] * sel[2] + 2.0 * sel[0] * sel[3] + 3.0 * sel[1] * sel[2]
            + 4.0 * sel[1] * sel[3] + 5.0 * sel[2] * sel[3])
    bucket = 6.0 * (gsel[1] + 2.0 * gsel[2] + 3.0 * gsel[3]) + pair

    ct_ref[...] = jnp.zeros(ct_ref.shape, F32)
    ct_ref[0:1, :] = c_a
    ct_ref[1:2, :] = c_b
    xs_ref[:, d:d + ROUTER_LANES] = ct_ref[...].T

    for b in range(N_BUCKET_ROWS):
        oh_ref[b:b + 1, :] = jnp.where(bucket == float(b), 1.0, 0.0)
    oh = oh_ref[...]
    upper = (lax.broadcasted_iota(jnp.int32, (tm, tm), 0) <= lax.broadcasted_iota(jnp.int32, (tm, tm), 1))
    cum = jnp.dot(oh.astype(BF16), jnp.where(upper, 1.0, 0.0).astype(BF16), preferred_element_type=F32)
    carry = carry_ref[...]
    rank = jnp.sum(oh * (cum - 1.0 + carry), axis=0, keepdims=True)
    carry_new = carry + jnp.sum(oh, axis=1, keepdims=True)
    carry_ref[...] = carry_new
    cnt_ref[...] = carry_new[:, 0:ROUTER_LANES]
    meta_ref[...] = jnp.zeros(meta_ref.shape, jnp.int32)
    meta_ref[0:1, :] = bucket.astype(jnp.int32)
    meta_ref[1:2, :] = rank.astype(jnp.int32)


def _outproj_router(h2d, yabd, yc, wabd, wc, gain, wr, br):
    tkn, d = h2d.shape
    tm = min(TOK_TILE, tkn)
    assert tkn % tm == 0
    row = lambda w: pl.BlockSpec((tm, w), lambda i: (i, 0))
    full = lambda shape: pl.BlockSpec(shape, lambda i: (0,) * len(shape))
    return pl.pallas_call(
        functools.partial(_outproj_router_kernel, tm=tm),
        grid=(tkn // tm,),
        in_specs=[row(d), row(3 * W_GROUP), row(W_GROUP),
                  full((3 * W_GROUP, d)), full((W_GROUP, d)), full((1, d)),
                  full((d, ROUTER_LANES)), full((1, ROUTER_LANES))],
        out_specs=[row(d), row(d + ROUTER_LANES),
                   pl.BlockSpec((None, 8, tm), lambda i: (i, 0, 0)),
                   full((N_BUCKET_ROWS, ROUTER_LANES))],
        out_shape=[jax.ShapeDtypeStruct((tkn, d), F32),
                   jax.ShapeDtypeStruct((tkn, d + ROUTER_LANES), F32),
                   jax.ShapeDtypeStruct((tkn // tm, 8, tm), jnp.int32),
                   jax.ShapeDtypeStruct((N_BUCKET_ROWS, ROUTER_LANES), F32)],
        scratch_shapes=[pltpu.VMEM((ROUTER_LANES, tm), F32), pltpu.VMEM((ROUTER_LANES, tm), F32),
                        pltpu.VMEM((N_BUCKET_ROWS, tm), F32), pltpu.VMEM((N_BUCKET_ROWS, tm), F32)],
        compiler_params=_cparams(("arbitrary",)),
        name="outproj_router",
    )(h2d, yabd, yc, wabd, wc, gain, wr, br)


def _moe_kernel(vt_ref, vea_ref, veb_ref, vlo_ref, vhi_ref, vfl_ref,
                tok_ref, tokn_ref, xs_hbm, wgua_ref, wgub_ref, wda_ref, wdb_ref,
                y_hbm, xbuf, ybuf, gsem, ssem, *, tm, n_tiles, n_visits):
    v = pl.program_id(0)
    tile = vt_ref[v]
    flags = vfl_ref[v]
    first = (flags & 1) != 0
    last = (flags & 2) != 0
    valid = (flags & 4) != 0
    slot = tile % 2
    d = y_hbm.shape[1]

    def row_gather(ids_ref, s, i):
        return pltpu.make_async_copy(xs_hbm.at[pl.ds(ids_ref[0, i], 1), :], xbuf.at[s, pl.ds(i, 1), :], gsem.at[s])

    def row_scatter(ids_ref, s, i):
        return pltpu.make_async_copy(ybuf.at[s, pl.ds(i, 1), :], y_hbm.at[pl.ds(ids_ref[0, i], 1), :], ssem.at[s])

    def start_all(mk, ids_ref, s):
        def body(i, c):
            mk(ids_ref, s, i).start()
            return c
        lax.fori_loop(0, tm, body, 0, unroll=8)

    def wait_gather(s):
        pltpu.make_async_copy(xs_hbm.at[pl.ds(0, tm), :], xbuf.at[s], gsem.at[s]).wait()

    def wait_scatter(s):
        pltpu.make_async_copy(ybuf.at[s], y_hbm.at[pl.ds(0, tm), :], ssem.at[s]).wait()

    @pl.when(v == 0)
    def _():
        start_all(row_gather, tok_ref, 0)

    @pl.when(jnp.logical_and(first, valid))
    def _():
        wait_gather(slot)

        @pl.when(tile + 1 < n_tiles)
        def _():
            start_all(row_gather, tokn_ref, 1 - slot)

        @pl.when(tile >= 2)
        def _():
            wait_scatter(slot)

    @pl.when(valid)
    def _():
        rows = tile * tm + lax.broadcasted_iota(jnp.int32, (tm, 1), 0)
        inb = jnp.where(jnp.logical_and(rows >= vlo_ref[v], rows < vhi_ref[v]), 1.0, 0.0)
        xt = xbuf[slot]
        x = xt[:, 0:d].astype(BF16)
        c_a = xt[:, d:d + 1] * inb
        c_b = xt[:, d + 1:d + 2] * inb

        def expert(wgu_ref, wd_ref, c):
            gu = jnp.dot(x, wgu_ref[...], preferred_element_type=F32)
            gate = gu[:, :D_EXPERT]
            hdn = gate * _sigmoid(gate) * gu[:, D_EXPERT:] * c
            return jnp.dot(hdn.astype(BF16), wd_ref[...], preferred_element_type=F32)

        y = expert(wgua_ref, wda_ref, c_a) + expert(wgub_ref, wdb_ref, c_b)

        @pl.when(first)
        def _():
            ybuf[slot] = y

        @pl.when(jnp.logical_not(first))
        def _():
            ybuf[slot] = ybuf[slot] + y

    @pl.when(jnp.logical_and(last, valid))
    def _():
        start_all(row_scatter, tok_ref, slot)

    @pl.when(v == n_visits - 1)
    def _():
        wait_scatter(0)
        wait_scatter(1)


def _moe(xs, tok_of_slot, sched, wgu, wd):
    tkn, dx = xs.shape
    d = dx - ROUTER_LANES
    tm = MOE_TILE
    assert tkn % tm == 0 and tkn // tm >= 2
    n_tiles = tkn // tm
    n_visits = sched[0].shape[0]
    tok3 = tok_of_slot.reshape(n_tiles, 1, tm)
    wspec = lambda shape, which: pl.BlockSpec(
        (None,) + shape, lambda v, vt, vea, veb, vlo, vhi, vfl: ((vea, veb)[which][v], 0, 0))
    grid_spec = pltpu.PrefetchScalarGridSpec(
        num_scalar_prefetch=6,
        grid=(n_visits,),
        in_specs=[
            pl.BlockSpec((None, 1, tm), lambda v, vt, *_: (vt[v], 0, 0), memory_space=pltpu.SMEM),
            pl.BlockSpec((None, 1, tm), lambda v, vt, *_: (jnp.minimum(vt[v] + 1, n_tiles - 1), 0, 0),
                         memory_space=pltpu.SMEM),
            pl.BlockSpec(memory_space=pl.ANY),
            wspec((d, 2 * D_EXPERT), 0), wspec((d, 2 * D_EXPERT), 1),
            wspec((D_EXPERT, d), 0), wspec((D_EXPERT, d), 1),
        ],
        out_specs=pl.BlockSpec(memory_space=pl.ANY),
        scratch_shapes=[pltpu.VMEM((2, tm, dx), F32), pltpu.VMEM((2, tm, d), F32),
                        pltpu.SemaphoreType.DMA((2,)), pltpu.SemaphoreType.DMA((2,))],
    )
    return pl.pallas_call(
        functools.partial(_moe_kernel, tm=tm, n_tiles=n_tiles, n_visits=n_visits),
        grid_spec=grid_spec,
        out_shape=jax.ShapeDtypeStruct((tkn, d), F32),
        compiler_params=_cparams(("arbitrary",)),
        name="moe_experts",
    )(*sched, tok3, tok3, xs, wgu, wgu, wd, wd)


def _moe_schedule(bucket, rank, counts, tm):
    tkn = bucket.size
    n_tiles = tkn // tm
    n_visits = n_tiles + N_BUCKETS
    ends = jnp.cumsum(counts)
    offs = ends - counts
    slot = rank
    for bb in range(N_BUCKETS):
        slot = slot + jnp.where(bucket == bb, offs[bb], 0)
    tok_of_slot = jnp.zeros((tkn,), jnp.int32).at[slot.reshape(tkn)].set(jnp.arange(tkn, dtype=jnp.int32))
    t_first = offs // tm
    t_last = jnp.maximum(ends - 1, 0) // tm
    nv = jnp.where(counts > 0, t_last - t_first + 1, 0)
    v_end = jnp.cumsum(nv)
    v_start = v_end - nv
    total = v_end[-1]
    v = jnp.arange(n_visits, dtype=jnp.int32)
    vc = jnp.minimum(v, total - 1)
    b = jnp.sum((v_end[None, :] <= vc[:, None]).astype(jnp.int32), axis=1)
    onehot = b[:, None] == jnp.arange(N_BUCKETS, dtype=jnp.int32)[None, :]
    pick = lambda table: jnp.sum(jnp.where(onehot, table[None, :], 0), axis=1)
    tile = (pick(t_first) + vc - pick(v_start)).astype(jnp.int32)
    valid = v < total
    prev_tile = jnp.concatenate([jnp.full((1,), -1, jnp.int32), tile[:-1]])
    next_tile = jnp.concatenate([tile[1:], jnp.full((1,), -1, jnp.int32)])
    first = tile != prev_tile
    last = jnp.logical_or(tile != next_tile, v == total - 1)
    flags = (first.astype(jnp.int32) + 2 * last.astype(jnp.int32) + 4) * valid.astype(jnp.int32)
    g, pr = b // 6, b % 6
    lo_e = (pr >= 3).astype(jnp.int32) + (pr >= 5).astype(jnp.int32)
    hi_e = pr + 1 - 2 * (pr >= 3).astype(jnp.int32) - (pr >= 5).astype(jnp.int32)
    ea = g * EXPERTS_PER_GROUP + lo_e
    eb = g * EXPERTS_PER_GROUP + hi_e
    sched = (tile, ea.astype(jnp.int32), eb.astype(jnp.int32),
             pick(offs).astype(jnp.int32), pick(ends).astype(jnp.int32), flags.astype(jnp.int32))
    return tok_of_slot, sched


def _ple_kernel(h_ref, y_ref, p_ref, g_ref, wg_ref, bg_ref, wp_ref, fg_ref, o_ref, *, final):
    h = h_ref[...] + y_ref[...]
    xn = _rms_rows(h, g_ref[...]).astype(BF16)
    gate = _sigmoid(jnp.dot(xn, wg_ref[...], preferred_element_type=F32) + bg_ref[...])
    out = h + gate * jnp.dot(p_ref[...].astype(BF16), wp_ref[...], preferred_element_type=F32)
    if final:
        out = _rms_rows(out, fg_ref[...])
    o_ref[...] = out


def _ple(h2d, y2d, p2d, gain, wg, bg, wp, fgain, final):
    tkn, d = h2d.shape
    pd = p2d.shape[1]
    tm = min(TOK_TILE, tkn)
    assert tkn % tm == 0
    full = lambda shape: pl.BlockSpec(shape, lambda i: (0,) * len(shape))
    return pl.pallas_call(
        functools.partial(_ple_kernel, final=final),
        grid=(tkn // tm,),
        in_specs=[pl.BlockSpec((tm, d), lambda i: (i, 0)), pl.BlockSpec((tm, d), lambda i: (i, 0)),
                  pl.BlockSpec((tm, pd), lambda i: (i, 0)),
                  full((1, d)), full((d, d)), full((1, d)), full((pd, d)), full((1, d))],
        out_specs=pl.BlockSpec((tm, d), lambda i: (i, 0)),
        out_shape=jax.ShapeDtypeStruct((tkn, d), F32),
        compiler_params=_cparams(("arbitrary",)),
        name="ple_gate",
    )(h2d, y2d, p2d, gain, wg, bg, wp, fgain)


def _rope_table(seq):
    pos = jnp.arange(seq, dtype=F32)
    inv = ROPE_THETA ** (-jnp.arange(0, DIFF_HEAD_DIM, 2, dtype=F32) / DIFF_HEAD_DIM)
    ang = pos[:, None] * inv[None, :]
    ang = jnp.concatenate([ang, ang], axis=-1)
    cos, sin = jnp.cos(ang), jnp.sin(ang)
    lo = jnp.arange(DIFF_HEAD_DIM) < DIFF_HEAD_DIM // 2
    sin_lo = jnp.where(lo, -sin, 0.0)
    sin_hi = jnp.where(lo, 0.0, sin)
    rep = 128 // DIFF_HEAD_DIM
    return jnp.concatenate([jnp.tile(t, (1, rep)) for t in (cos, sin_lo, sin_hi)], axis=1)


def _block_diag(w):
    g, c, dd = w.shape
    out = jnp.zeros((g * c, g * dd), w.dtype)
    for i in range(g):
        out = out.at[i * c:(i + 1) * c, i * dd:(i + 1) * dd].set(w[i])
    return out


def _router_weights(wg, bg, we, be):
    d = wg.shape[0]
    w = jnp.zeros((d, ROUTER_LANES), F32)
    w = w.at[:, 0:N_GROUPS].set(wg).at[:, 8:8 + N_EXPERTS].set(we)
    w_hi = w.astype(BF16)
    w_lo = (w - w_hi.astype(F32)).astype(BF16)
    wr = w_hi.at[:, ROUTER_LO:ROUTER_LO + 8 + N_EXPERTS].set(w_lo[:, 0:8 + N_EXPERTS])
    br = jnp.zeros((1, ROUTER_LANES), F32)
    br = br.at[0, 0:N_GROUPS].set(bg).at[0, 8:8 + N_EXPERTS].set(be)
    return wr, br


def kernel(x, p, mix_norm, w_in, conf_conv_w, conf_conv_b, conf_ln_g, conf_ln_b, pool_w, pool_b, pool_scale, diff_lam_q1, diff_lam_k1, diff_lam_q2, diff_lam_k2, diff_subln_g, sconv_w, w_out, ffn_norm, router_group_w, router_group_b, router_expert_w, router_expert_b, expert_w_gate, expert_w_up, expert_w_down, ple_norm, ple_gate_w, ple_gate_b, ple_proj, final_norm):
    bsz, seq, d = x.shape
    depth = w_in.shape[0]
    tkn = bsz * seq
    rope = _rope_table(seq)
    att_t = min(ATT_TILE, seq)
    row = lambda v: v.reshape(1, -1).astype(F32)

    h = x
    for i in range(depth):
        lam_init = 0.8 - 0.6 * math.exp(-0.3 * i)
        yabd, qt, k, vt = _inproj_mix(
            h, row(mix_norm[i]), w_in[i].astype(BF16), rope,
            conf_conv_w[i], row(conf_conv_b[i]), row(conf_ln_g[i]), row(conf_ln_b[i]),
            _block_diag(pool_w[i]).astype(BF16), row(pool_b[i]), row(pool_scale[i]), sconv_w[i])
        lam_params = jnp.stack([diff_lam_q1[i], diff_lam_k1[i], diff_lam_q2[i], diff_lam_k2[i]]).astype(F32)
        g_bcast = jnp.broadcast_to(diff_subln_g[i].astype(F32)[:, None], (DIFF_V_DIM, att_t))
        yc = _diff_attn(qt, k, vt, lam_params, g_bcast, lam_init)

        wo = w_out[i].astype(BF16)
        wabd = jnp.concatenate([wo[0:2 * W_GROUP], wo[3 * W_GROUP:]], axis=0)
        wc = wo[2 * W_GROUP:3 * W_GROUP]
        wr, br = _router_weights(router_group_w[i], router_group_b[i], router_expert_w[i], router_expert_b[i])
        h1, xs, meta, cnt = _outproj_router(
            h.reshape(tkn, d), yabd.reshape(tkn, 3 * W_GROUP), yc.reshape(tkn, W_GROUP),
            wabd, wc, row(ffn_norm[i]), wr, br)

        tok_of_slot, sched = _moe_schedule(
            meta[:, 0, :], meta[:, 1, :],
            cnt[:N_BUCKETS, 0].astype(jnp.int32), MOE_TILE)
        wgu = jnp.concatenate([expert_w_gate[i], expert_w_up[i]], axis=-1).astype(BF16)
        y = _moe(xs, tok_of_slot, sched, wgu, expert_w_down[i].astype(BF16))

        h3 = _ple(h1, y, p[i].reshape(tkn, -1), row(ple_norm[i]), ple_gate_w[i].astype(BF16),
                  row(ple_gate_b[i]), ple_proj[i].astype(BF16), row(final_norm), final=(i == depth - 1))
        h = h3.reshape(bsz, seq, d)
    return h
```

```python
import functools
import math

import jax
import jax.numpy as jnp
from jax import lax
from jax.experimental import pallas as pl
from jax.experimental.pallas import tpu as pltpu

F32 = jnp.float32
BF16 = jnp.bfloat16

EPS = 1e-6
ROPE_THETA = 10000.0
W_GROUP = 256
CONF_KERNEL = 31
CONF_HIST = 32
SUBLANES = 8
POOL_HIST = 16
SCONV_KERNEL = 3
SCONV_HIST = 8
DIFF_HEADS = 4
DIFF_HEAD_DIM = 32
DIFF_V_DIM = 64
V_AUG_ROWS = 80
N_GROUPS = 4
EXPERTS_PER_GROUP = 4
N_EXPERTS = 16
D_EXPERT = 256
ROUTER_LANES = 128
ROUTER_LO = 32
NEG_BIG = -1e30
LOG2E = 1.4426950408889634

VMEM_LIMIT = 48 * 1024 * 1024

SEQ_TILE = 512
ROW_CHUNK = 64
ATT_TILE = 256
TOK_TILE = 512
MOE_TILE = 256
N_BUCKETS = 24
N_BUCKET_ROWS = 32


def _cparams(sem):
    return pltpu.CompilerParams(dimension_semantics=sem, vmem_limit_bytes=VMEM_LIMIT)


def _rms_rows(x, g):
    ms = jnp.mean(x * x, axis=-1, keepdims=True)
    return x * lax.rsqrt(ms + EPS) * g


def _sigmoid(x):
    return 1.0 / (1.0 + jnp.exp(-x))


def _inproj_mix_kernel(h_ref, g_ref, w_ref, rope_ref, cw_ref, cb_ref, lg_ref, lb_ref,
                       pw_ref, pb_ref, ps_ref, sw_ref,
                       yabd_ref, qt_ref, k_ref, vt_ref,
                       gbuf, pbuf, zbuf, gshift, *, tm):
    s = pl.program_id(1)

    @pl.when(s == 0)
    def _():
        gbuf[0:CONF_HIST, :] = jnp.zeros((CONF_HIST, W_GROUP), F32)
        pbuf[0:POOL_HIST, :] = jnp.zeros((POOL_HIST, W_GROUP), F32)
        zbuf[0:SCONV_HIST, :] = jnp.zeros((SCONV_HIST, W_GROUP), F32)

    n = _rms_rows(h_ref[...], g_ref[...]).astype(BF16)

    def proj(lo, hi):
        return jnp.dot(n, w_ref[:, lo:hi], preferred_element_type=F32)

    a = proj(0, 2 * W_GROUP)
    gbuf[CONF_HIST:CONF_HIST + tm, :] = a[:, :W_GROUP] * _sigmoid(a[:, W_GROUP:])
    span = tm + CONF_HIST - SUBLANES
    for sh in range(1, SUBLANES):
        gshift[sh - 1, 0:span, :] = gbuf[sh:sh + span, :]
    base = CONF_HIST - (CONF_KERNEL - 1)
    for c in range(tm // ROW_CHUNK):
        r0 = c * ROW_CHUNK
        acc = jnp.broadcast_to(cb_ref[...], (ROW_CHUNK, W_GROUP))
        for j in range(CONF_KERNEL):
            sh = (base + j) % SUBLANES
            q0 = r0 + base + j - sh
            win = gbuf[q0:q0 + ROW_CHUNK, :] if sh == 0 else gshift[sh - 1, q0:q0 + ROW_CHUNK, :]
            acc = acc + cw_ref[j:j + 1, :] * win
        mu = jnp.mean(acc, axis=-1, keepdims=True)
        d = acc - mu
        var = jnp.mean(d * d, axis=-1, keepdims=True)
        y = d * lax.rsqrt(var + EPS) * lg_ref[...] + lb_ref[...]
        yabd_ref[r0:r0 + ROW_CHUNK, 0:W_GROUP] = (y * _sigmoid(y)).astype(BF16)
    gbuf[0:CONF_HIST, :] = gbuf[tm:tm + CONF_HIST, :]

    pbuf[POOL_HIST:POOL_HIST + tm, :] = proj(2 * W_GROUP, 3 * W_GROUP)
    lane = lax.broadcasted_iota(jnp.int32, (ROW_CHUNK, 128), 1)
    first = lane < 64
    for c in range(tm // ROW_CHUNK):
        r0 = c * ROW_CHUNK
        tpos = (s * tm + r0 + 1 + lax.broadcasted_iota(jnp.int32, (ROW_CHUNK, 128), 0)).astype(F32)
        halves = []
        for half, (w_small, w_big) in enumerate(((2, 4), (8, 16))):
            l0 = half * 128

            def ld(j):
                return pbuf[r0 + POOL_HIST - j:r0 + POOL_HIST - j + ROW_CHUNK, l0:l0 + 128]

            cur = ld(0)
            run = cur
            for j in range(1, w_small):
                run = run + ld(j)
            small = run
            for j in range(w_small, w_big):
                run = run + ld(j)
            cnt = jnp.where(first, jnp.minimum(tpos, float(w_small)), jnp.minimum(tpos, float(w_big)))
            halves.append(jnp.where(first, small, run) / cnt - cur)
        pc = jnp.concatenate(halves, axis=1).astype(BF16)
        yb = (jnp.dot(pc, pw_ref[...], preferred_element_type=F32) + pb_ref[...]) * ps_ref[...]
        yabd_ref[r0:r0 + ROW_CHUNK, W_GROUP:2 * W_GROUP] = yb.astype(BF16)
    pbuf[0:POOL_HIST, :] = pbuf[tm:tm + POOL_HIST, :]

    sc = proj(6 * W_GROUP, 9 * W_GROUP)
    zbuf[SCONV_HIST:SCONV_HIST + tm, :] = sc[:, W_GROUP:2 * W_GROUP] * sc[:, 2 * W_GROUP:]
    conv = None
    for j in range(SCONV_KERNEL):
        off = SCONV_HIST - (SCONV_KERNEL - 1) + j
        term = sw_ref[j:j + 1, :] * zbuf[off:off + tm, :]
        conv = term if conv is None else conv + term
    yabd_ref[:, 2 * W_GROUP:3 * W_GROUP] = (sc[:, :W_GROUP] * conv).astype(BF16)
    zbuf[0:SCONV_HIST, :] = zbuf[tm:tm + SCONV_HIST, :]

    qk = proj(3 * W_GROUP, 5 * W_GROUP)
    tab = rope_ref[...]
    cos = jnp.concatenate([tab[:, 0:128]] * 4, axis=1)
    sin_lo = jnp.concatenate([tab[:, 128:256]] * 4, axis=1)
    sin_hi = jnp.concatenate([tab[:, 256:384]] * 4, axis=1)
    half = DIFF_HEAD_DIM // 2
    width = 2 * W_GROUP
    qk = qk * cos + pltpu.roll(qk, width - half, axis=1) * sin_lo + pltpu.roll(qk, half, axis=1) * sin_hi
    q = qk[:, :W_GROUP] * (DIFF_HEAD_DIM ** -0.5 * LOG2E)
    qt_ref[...] = q.T.astype(BF16)
    k_ref[...] = qk[:, W_GROUP:].astype(BF16)
    vt = proj(5 * W_GROUP, 6 * W_GROUP).T.astype(BF16)
    for hd in range(DIFF_HEADS):
        r0 = hd * V_AUG_ROWS
        vt_ref[r0:r0 + DIFF_V_DIM, :] = vt[hd * DIFF_V_DIM:(hd + 1) * DIFF_V_DIM, :]
        vt_ref[r0 + DIFF_V_DIM:r0 + V_AUG_ROWS, :] = jnp.ones((V_AUG_ROWS - DIFF_V_DIM, tm), BF16)


def _inproj_mix(h, gain, w_in, rope, cw, cb, lg, lb, pw_bd, pb, ps, sw):
    bsz, seq, d = h.shape
    tm = min(SEQ_TILE, seq)
    assert seq % tm == 0 and tm % ROW_CHUNK == 0 and tm >= CONF_HIST
    ncol = w_in.shape[1]
    full = lambda shape: pl.BlockSpec(shape, lambda b, s: (0,) * len(shape))
    return pl.pallas_call(
        functools.partial(_inproj_mix_kernel, tm=tm),
        grid=(bsz, seq // tm),
        in_specs=[
            pl.BlockSpec((None, tm, d), lambda b, s: (b, s, 0)),
            full((1, d)),
            full((d, ncol)),
            pl.BlockSpec((tm, 384), lambda b, s: (s, 0)),
            full((CONF_KERNEL, W_GROUP)), full((1, W_GROUP)), full((1, W_GROUP)), full((1, W_GROUP)),
            full((W_GROUP, W_GROUP)), full((1, W_GROUP)), full((1, W_GROUP)),
            full((SCONV_KERNEL, W_GROUP)),
        ],
        out_specs=[
            pl.BlockSpec((None, tm, 3 * W_GROUP), lambda b, s: (b, s, 0)),
            pl.BlockSpec((None, W_GROUP, tm), lambda b, s: (b, 0, s)),
            pl.BlockSpec((None, tm, W_GROUP), lambda b, s: (b, s, 0)),
            pl.BlockSpec((None, DIFF_HEADS * V_AUG_ROWS, tm), lambda b, s: (b, 0, s)),
        ],
        out_shape=[
            jax.ShapeDtypeStruct((bsz, seq, 3 * W_GROUP), BF16),
            jax.ShapeDtypeStruct((bsz, W_GROUP, seq), BF16),
            jax.ShapeDtypeStruct((bsz, seq, W_GROUP), BF16),
            jax.ShapeDtypeStruct((bsz, DIFF_HEADS * V_AUG_ROWS, seq), BF16),
        ],
        scratch_shapes=[
            pltpu.VMEM((tm + CONF_HIST, W_GROUP), F32),
            pltpu.VMEM((tm + POOL_HIST, W_GROUP), F32),
            pltpu.VMEM((tm + SCONV_HIST, W_GROUP), F32),
            pltpu.VMEM((SUBLANES - 1, tm + CONF_HIST - SUBLANES, W_GROUP), F32),
        ],
        compiler_params=_cparams(("arbitrary", "arbitrary")),
        name="inproj_mix",
    )(h, gain, w_in, rope, cw, cb, lg, lb, pw_bd, pb, ps, sw)


def _diff_attn_kernel(qt_ref, k_ref, vt_ref, lam_ref, g_ref, o_ref,
                      qm_ref, s0_ref, s1_ref, p0_ref, p1_ref, al0_ref, al1_ref, m_ref, l_ref, acc_ref,
                      *, t, lam_init):
    qi = pl.program_id(1)
    nhc = 2 * DIFF_HEADS

    qm_ref[...] = jnp.zeros(qm_ref.shape, BF16)
    for j in range(nhc):
        lo = j * DIFF_HEAD_DIM
        qm_ref[lo:lo + DIFF_HEAD_DIM, j * t:(j + 1) * t] = qt_ref[lo:lo + DIFF_HEAD_DIM, :]
    m_ref[...] = jnp.full(m_ref.shape, NEG_BIG, F32)
    l_ref[...] = jnp.zeros(l_ref.shape, F32)
    acc_ref[...] = jnp.zeros(acc_ref.shape, F32)
    p1_ref[...] = jnp.zeros(p1_ref.shape, BF16)
    al1_ref[...] = jnp.ones(al1_ref.shape, F32)

    def scores(kt, s_ref):
        k0 = pl.multiple_of(kt * t, t)
        s_ref[...] = jnp.dot(k_ref[pl.ds(k0, t), :], qm_ref[...], preferred_element_type=F32)

    def softmax(s_ref, p_ref, al_ref, masked):
        if masked:
            above = (lax.broadcasted_iota(jnp.int32, (t, t), 0) > lax.broadcasted_iota(jnp.int32, (t, t), 1))
        for j in range(nhc):
            sc = s_ref[:, j * t:(j + 1) * t]
            if masked:
                sc = jnp.where(above, NEG_BIG, sc)
            m_old = m_ref[j]
            m_new = jnp.maximum(m_old, jnp.max(sc, axis=0, keepdims=True))
            al_ref[j] = jnp.exp2(m_old - m_new)
            m_ref[j] = m_new
            p_ref[:, j * t:(j + 1) * t] = jnp.exp2(sc - m_new).astype(BF16)

    def weighted_values(p_ref, al_ref, kt):
        k0 = pl.multiple_of(jnp.maximum(kt, 0) * t, t)
        for hd in range(DIFF_HEADS):
            vth = vt_ref[hd * V_AUG_ROWS:(hd + 1) * V_AUG_ROWS, pl.ds(k0, t)]
            o = jnp.dot(vth, p_ref[:, 2 * hd * t:(2 * hd + 2) * t], preferred_element_type=F32)
            for c in range(2):
                j = 2 * hd + c
                alpha = al_ref[j]
                acc_ref[j] = alpha * acc_ref[j] + o[0:DIFF_V_DIM, c * t:(c + 1) * t]
                l_ref[j] = alpha * l_ref[j] + o[DIFF_V_DIM:DIFF_V_DIM + 1, c * t:(c + 1) * t]

    scores(0, s0_ref)

    def pair(i, carry):
        kt = 2 * i
        scores(kt + 1, s1_ref)
        softmax(s0_ref, p0_ref, al0_ref, False)
        weighted_values(p1_ref, al1_ref, kt - 1)
        scores(kt + 2, s0_ref)
        softmax(s1_ref, p1_ref, al1_ref, False)
        weighted_values(p0_ref, al0_ref, kt)
        return carry

    lax.fori_loop(0, qi // 2, pair, 0)

    @pl.when(qi % 2 == 0)
    def _():
        softmax(s0_ref, p0_ref, al0_ref, True)
        weighted_values(p1_ref, al1_ref, qi - 1)
        weighted_values(p0_ref, al0_ref, qi)

    @pl.when(qi % 2 == 1)
    def _():
        scores(qi, s1_ref)
        softmax(s0_ref, p0_ref, al0_ref, False)
        weighted_values(p1_ref, al1_ref, qi - 2)
        softmax(s1_ref, p1_ref, al1_ref, True)
        weighted_values(p0_ref, al0_ref, qi - 1)
        weighted_values(p1_ref, al1_ref, qi)

    lp = lam_ref[...]
    lam = (jnp.exp(jnp.sum(lp[0:1] * lp[1:2], axis=-1, keepdims=True))
           - jnp.exp(jnp.sum(lp[2:3] * lp[3:4], axis=-1, keepdims=True)) + lam_init)
    outs = []
    for hd in range(DIFF_HEADS):
        o1 = acc_ref[2 * hd] / l_ref[2 * hd]
        o2 = acc_ref[2 * hd + 1] / l_ref[2 * hd + 1]
        o = o1 - lam * o2
        ms = jnp.mean(o * o, axis=0, keepdims=True)
        outs.append(o * lax.rsqrt(ms + EPS) * g_ref[...] * (1.0 - lam_init))
    o_ref[...] = jnp.concatenate(outs, axis=0).T.astype(BF16)


def _diff_attn(qt, k, vt, lam_params, g_bcast, lam_init):
    bsz, seq, _ = k.shape
    t = min(ATT_TILE, seq)
    assert seq % t == 0
    nhc = 2 * DIFF_HEADS
    return pl.pallas_call(
        functools.partial(_diff_attn_kernel, t=t, lam_init=lam_init),
        grid=(bsz, seq // t),
        in_specs=[
            pl.BlockSpec((None, W_GROUP, t), lambda b, q: (b, 0, q)),
            pl.BlockSpec((None, seq, W_GROUP), lambda b, q: (b, 0, 0)),
            pl.BlockSpec((None, DIFF_HEADS * V_AUG_ROWS, seq), lambda b, q: (b, 0, 0)),
            pl.BlockSpec((4, DIFF_HEAD_DIM), lambda b, q: (0, 0)),
            pl.BlockSpec((DIFF_V_DIM, t), lambda b, q: (0, 0)),
        ],
        out_specs=pl.BlockSpec((None, t, W_GROUP), lambda b, q: (b, q, 0)),
        out_shape=jax.ShapeDtypeStruct((bsz, seq, W_GROUP), BF16),
        scratch_shapes=[
            pltpu.VMEM((W_GROUP, nhc * t), BF16),
            pltpu.VMEM((t, nhc * t), F32), pltpu.VMEM((t, nhc * t), F32),
            pltpu.VMEM((t, nhc * t), BF16), pltpu.VMEM((t, nhc * t), BF16),
            pltpu.VMEM((nhc, 1, t), F32), pltpu.VMEM((nhc, 1, t), F32),
            pltpu.VMEM((nhc, 1, t), F32),
            pltpu.VMEM((nhc, 1, t), F32),
            pltpu.VMEM((nhc, DIFF_V_DIM, t), F32),
        ],
        compiler_params=_cparams(("arbitrary", "arbitrary")),
        name="diff_attn",
    )(qt, k, vt, lam_params, g_bcast)


def _outproj_router_kernel(h_ref, yabd_ref, yc_ref, wabd_ref, wc_ref, g_ref, wr_ref, br_ref,
                           h1_ref, xs_ref, meta_ref, cnt_ref, rt_ref, ct_ref, oh_ref, carry_ref, *, tm):
    d = h_ref.shape[1]

    @pl.when(pl.program_id(0) == 0)
    def _():
        carry_ref[...] = jnp.zeros(carry_ref.shape, F32)

    h1 = (h_ref[...]
          + jnp.dot(yabd_ref[...], wabd_ref[...], preferred_element_type=F32)
          + jnp.dot(yc_ref[...], wc_ref[...], preferred_element_type=F32))
    h1_ref[...] = h1
    xn = _rms_rows(h1, g_ref[...])
    x_hi = xn.astype(BF16)
    x_lo = (xn - x_hi.astype(F32)).astype(BF16)
    xs_ref[:, 0:d] = xn
    r = (jnp.dot(x_hi, wr_ref[...], preferred_element_type=F32)
         + jnp.dot(x_lo, wr_ref[...], preferred_element_type=F32) + br_ref[...])
    rt_ref[...] = r.T

    def row(i):
        return rt_ref[i:i + 1, :] + rt_ref[ROUTER_LO + i:ROUTER_LO + i + 1, :]

    def first_hits(vals, target):
        hits, free = [], None
        for v in vals:
            f = jnp.where(v == target, 1.0, 0.0)
            hits.append(f if free is None else f * free)
            free = (1.0 - f) if free is None else free * (1.0 - f)
        return hits

    add = lambda a, b: a + b
    gl = [row(i) for i in range(N_GROUPS)]
    el = [row(8 + e) for e in range(N_EXPERTS)]
    gmax = functools.reduce(jnp.maximum, gl)
    gsel = first_hits(gl, gmax)
    pg_c = 1.0 / functools.reduce(add, [jnp.exp(v - gmax) for v in gl])
    elc = []
    for kk in range(EXPERTS_PER_GROUP):
        v = gsel[0] * el[kk]
        for g in range(1, N_GROUPS):
            v = v + gsel[g] * el[g * EXPERTS_PER_GROUP + kk]
        elc.append(v)
    emax = functools.reduce(jnp.maximum, elc)
    ex = [jnp.exp(v - emax) for v in elc]
    se = functools.reduce(add, ex)
    pe = [v / se for v in ex]
    p1 = functools.reduce(jnp.maximum, pe)
    t1 = first_hits(pe, p1)
    rest = [jnp.where(t > 0.0, -1.0, v) for t, v in zip(t1, pe)]
    p2 = functools.reduce(jnp.maximum, rest)
    t2 = first_hits(rest, p2)
    den = p1 + p2
    w1 = pg_c * (p1 / den)
    w2 = pg_c * (p2 / den)
    wk = [a * w1 + b * w2 for a, b in zip(t1, t2)]
    sel = [a + b for a, b in zip(t1, t2)]
    lower = first_hits(sel, 1.0)
    c_a = functools.reduce(add, [f * w for f, w in zip(lower, wk)])
    c_b = functools.reduce(add, [(s - f) * w for s, f, w in zip(sel, lower, wk)])
    pair = (sel[0] * sel[2] + 2.0 * sel[0] * sel[3] + 3.0 * sel[1] * sel[2]
            + 4.0 * sel[1] * sel[3] + 5.0 * sel[2] * sel[3])
    bucket = 6.0 * (gsel[1] + 2.0 * gsel[2] + 3.0 * gsel[3]) + pair

    ct_ref[...] = jnp.zeros(ct_ref.shape, F32)
    ct_ref[0:1, :] = c_a
    ct_ref[1:2, :] = c_b
    xs_ref[:, d:d + ROUTER_LANES] = ct_ref[...].T

    for b in range(N_BUCKET_ROWS):
        oh_ref[b:b + 1, :] = jnp.where(bucket == float(b), 1.0, 0.0)
    oh = oh_ref[...]
    upper = (lax.broadcasted_iota(jnp.int32, (tm, tm), 0) <= lax.broadcasted_iota(jnp.int32, (tm, tm), 1))
    cum = jnp.dot(oh.astype(BF16), jnp.where(upper, 1.0, 0.0).astype(BF16), preferred_element_type=F32)
    carry = carry_ref[...]
    rank = jnp.sum(oh * (cum - 1.0 + carry), axis=0, keepdims=True)
    carry_new = carry + jnp.sum(oh, axis=1, keepdims=True)
    carry_ref[...] = carry_new
    cnt_ref[...] = carry_new[:, 0:ROUTER_LANES]
    meta_ref[...] = jnp.zeros(meta_ref.shape, jnp.int32)
    meta_ref[0:1, :] = bucket.astype(jnp.int32)
    meta_ref[1:2, :] = rank.astype(jnp.int32)


def _outproj_router(h2d, yabd, yc, wabd, wc, gain, wr, br):
    tkn, d = h2d.shape
    tm = min(TOK_TILE, tkn)
    assert tkn % tm == 0
    row = lambda w: pl.BlockSpec((tm, w), lambda i: (i, 0))
    full = lambda shape: pl.BlockSpec(shape, lambda i: (0,) * len(shape))
    return pl.pallas_call(
        functools.partial(_outproj_router_kernel, tm=tm),
        grid=(tkn // tm,),
        in_specs=[row(d), row(3 * W_GROUP), row(W_GROUP),
                  full((3 * W_GROUP, d)), full((W_GROUP, d)), full((1, d)),
                  full((d, ROUTER_LANES)), full((1, ROUTER_LANES))],
        out_specs=[row(d), row(d + ROUTER_LANES),
                   pl.BlockSpec((None, 8, tm), lambda i: (i, 0, 0)),
                   full((N_BUCKET_ROWS, ROUTER_LANES))],
        out_shape=[jax.ShapeDtypeStruct((tkn, d), F32),
                   jax.ShapeDtypeStruct((tkn, d + ROUTER_LANES), F32),
                   jax.ShapeDtypeStruct((tkn // tm, 8, tm), jnp.int32),
                   jax.ShapeDtypeStruct((N_BUCKET_ROWS, ROUTER_LANES), F32)],
        scratch_shapes=[pltpu.VMEM((ROUTER_LANES, tm), F32), pltpu.VMEM((ROUTER_LANES, tm), F32),
                        pltpu.VMEM((N_BUCKET_ROWS, tm), F32), pltpu.VMEM((N_BUCKET_ROWS, tm), F32)],
        compiler_params=_cparams(("arbitrary",)),
        name="outproj_router",
    )(h2d, yabd, yc, wabd, wc, gain, wr, br)


def _moe_kernel(vt_ref, vea_ref, veb_ref, vlo_ref, vhi_ref, vfl_ref,
                tok_ref, tokn_ref, xs_hbm, wgua_ref, wgub_ref, wda_ref, wdb_ref,
                y_hbm, xbuf, ybuf, gsem, ssem, *, tm, n_tiles, n_visits):
    v = pl.program_id(0)
    tile = vt_ref[v]
    flags = vfl_ref[v]
    first = (flags & 1) != 0
    last = (flags & 2) != 0
    valid = (flags & 4) != 0
    slot = tile % 2
    d = y_hbm.shape[1]

    def row_gather(ids_ref, s, k, u):
        return pltpu.make_async_copy(xs_hbm.at[pl.ds(ids_ref[0, k * SUBLANES + u], 1), :],
                                     xbuf.at[s, k, pl.ds(u, 1), :], gsem.at[s])

    def row_scatter(ids_ref, s, k, u):
        return pltpu.make_async_copy(ybuf.at[s, k, pl.ds(u, 1), :],
                                     y_hbm.at[pl.ds(ids_ref[0, k * SUBLANES + u], 1), :], ssem.at[s])

    def start_all(mk, ids_ref, s):
        def body(k, c):
            for u in range(SUBLANES):
                mk(ids_ref, s, k, u).start(priority=u % 2)
            return c
        lax.fori_loop(0, tm // SUBLANES, body, 0)

    def wait_gather(s):
        pltpu.make_async_copy(xbuf.at[s], xbuf.at[s], gsem.at[s]).wait()

    def wait_scatter(s):
        pltpu.make_async_copy(ybuf.at[s], ybuf.at[s], ssem.at[s]).wait()

    @pl.when(v == 0)
    def _():
        start_all(row_gather, tok_ref, 0)

    @pl.when(jnp.logical_and(first, valid))
    def _():
        wait_gather(slot)

        @pl.when(tile + 1 < n_tiles)
        def _():
            start_all(row_gather, tokn_ref, 1 - slot)

        @pl.when(tile >= 2)
        def _():
            wait_scatter(slot)

    @pl.when(valid)
    def _():
        rows = tile * tm + lax.broadcasted_iota(jnp.int32, (tm, 1), 0)
        inb = jnp.where(jnp.logical_and(rows >= vlo_ref[v], rows < vhi_ref[v]), 1.0, 0.0)
        xt = xbuf[slot].reshape(tm, xs_hbm.shape[1])
        x = xt[:, 0:d].astype(BF16)
        c_a = xt[:, d:d + 1] * inb
        c_b = xt[:, d + 1:d + 2] * inb

        def expert(wgu_ref, wd_ref, c):
            gu = jnp.dot(x, wgu_ref[...], preferred_element_type=F32)
            gate = gu[:, :D_EXPERT]
            hdn = gate * _sigmoid(gate) * gu[:, D_EXPERT:] * c
            return jnp.dot(hdn.astype(BF16), wd_ref[...], preferred_element_type=F32)

        y = expert(wgua_ref, wda_ref, c_a) + expert(wgub_ref, wdb_ref, c_b)

        y = y.reshape(tm // SUBLANES, SUBLANES, d)

        @pl.when(first)
        def _():
            ybuf[slot] = y

        @pl.when(jnp.logical_not(first))
        def _():
            ybuf[slot] = ybuf[slot] + y

    @pl.when(jnp.logical_and(last, valid))
    def _():
        start_all(row_scatter, tok_ref, slot)

    @pl.when(v == n_visits - 1)
    def _():
        wait_scatter(0)
        wait_scatter(1)


def _moe(xs, tok_of_slot, sched, wgu, wd):
    tkn, dx = xs.shape
    d = dx - ROUTER_LANES
    tm = MOE_TILE
    assert tkn % tm == 0 and tkn // tm >= 2
    n_tiles = tkn // tm
    n_visits = sched[0].shape[0]
    tok3 = tok_of_slot.reshape(n_tiles, 1, tm)
    wspec = lambda shape, which: pl.BlockSpec(
        (None,) + shape, lambda v, vt, vea, veb, vlo, vhi, vfl: ((vea, veb)[which][v], 0, 0))
    grid_spec = pltpu.PrefetchScalarGridSpec(
        num_scalar_prefetch=6,
        grid=(n_visits,),
        in_specs=[
            pl.BlockSpec((None, 1, tm), lambda v, vt, *_: (vt[v], 0, 0), memory_space=pltpu.SMEM),
            pl.BlockSpec((None, 1, tm), lambda v, vt, *_: (jnp.minimum(vt[v] + 1, n_tiles - 1), 0, 0),
                         memory_space=pltpu.SMEM),
            pl.BlockSpec(memory_space=pl.ANY),
            wspec((d, 2 * D_EXPERT), 0), wspec((d, 2 * D_EXPERT), 1),
            wspec((D_EXPERT, d), 0), wspec((D_EXPERT, d), 1),
        ],
        out_specs=pl.BlockSpec(memory_space=pl.ANY),
        scratch_shapes=[pltpu.VMEM((2, tm // SUBLANES, SUBLANES, dx), F32),
                        pltpu.VMEM((2, tm // SUBLANES, SUBLANES, d), F32),
                        pltpu.SemaphoreType.DMA((2,)), pltpu.SemaphoreType.DMA((2,))],
    )
    return pl.pallas_call(
        functools.partial(_moe_kernel, tm=tm, n_tiles=n_tiles, n_visits=n_visits),
        grid_spec=grid_spec,
        out_shape=jax.ShapeDtypeStruct((tkn, d), F32),
        compiler_params=_cparams(("arbitrary",)),
        name="moe_experts",
    )(*sched, tok3, tok3, xs, wgu, wgu, wd, wd)


def _moe_schedule(bucket, rank, counts, tm):
    tkn = bucket.size
    n_tiles = tkn // tm
    n_visits = n_tiles + N_BUCKETS
    ends = jnp.cumsum(counts)
    offs = ends - counts
    slot = rank
    for bb in range(N_BUCKETS):
        slot = slot + jnp.where(bucket == bb, offs[bb], 0)
    tok_of_slot = jnp.zeros((tkn,), jnp.int32).at[slot.reshape(tkn)].set(jnp.arange(tkn, dtype=jnp.int32))
    t_first = offs // tm
    t_last = jnp.maximum(ends - 1, 0) // tm
    nv = jnp.where(counts > 0, t_last - t_first + 1, 0)
    v_end = jnp.cumsum(nv)
    v_start = v_end - nv
    total = v_end[-1]
    v = jnp.arange(n_visits, dtype=jnp.int32)
    vc = jnp.minimum(v, total - 1)
    b = jnp.sum((v_end[None, :] <= vc[:, None]).astype(jnp.int32), axis=1)
    onehot = b[:, None] == jnp.arange(N_BUCKETS, dtype=jnp.int32)[None, :]
    pick = lambda table: jnp.sum(jnp.where(onehot, table[None, :], 0), axis=1)
    tile = (pick(t_first) + vc - pick(v_start)).astype(jnp.int32)
    valid = v < total
    prev_tile = jnp.concatenate([jnp.full((1,), -1, jnp.int32), tile[:-1]])
    next_tile = jnp.concatenate([tile[1:], jnp.full((1,), -1, jnp.int32)])
    first = tile != prev_tile
    last = jnp.logical_or(tile != next_tile, v == total - 1)
    flags = (first.astype(jnp.int32) + 2 * last.astype(jnp.int32) + 4) * valid.astype(jnp.int32)
    g, pr = b // 6, b % 6
    lo_e = (pr >= 3).astype(jnp.int32) + (pr >= 5).astype(jnp.int32)
    hi_e = pr + 1 - 2 * (pr >= 3).astype(jnp.int32) - (pr >= 5).astype(jnp.int32)
    ea = g * EXPERTS_PER_GROUP + lo_e
    eb = g * EXPERTS_PER_GROUP + hi_e
    sched = (tile, ea.astype(jnp.int32), eb.astype(jnp.int32),
             pick(offs).astype(jnp.int32), pick(ends).astype(jnp.int32), flags.astype(jnp.int32))
    return tok_of_slot, sched


def _ple_kernel(h_ref, y_ref, p_ref, g_ref, wg_ref, bg_ref, wp_ref, fg_ref, o_ref, *, final):
    h = h_ref[...] + y_ref[...]
    xn = _rms_rows(h, g_ref[...]).astype(BF16)
    gate = _sigmoid(jnp.dot(xn, wg_ref[...], preferred_element_type=F32) + bg_ref[...])
    out = h + gate * jnp.dot(p_ref[...].astype(BF16), wp_ref[...], preferred_element_type=F32)
    if final:
        out = _rms_rows(out, fg_ref[...])
    o_ref[...] = out


def _ple(h2d, y2d, p2d, gain, wg, bg, wp, fgain, final):
    tkn, d = h2d.shape
    pd = p2d.shape[1]
    tm = min(TOK_TILE, tkn)
    assert tkn % tm == 0
    full = lambda shape: pl.BlockSpec(shape, lambda i: (0,) * len(shape))
    return pl.pallas_call(
        functools.partial(_ple_kernel, final=final),
        grid=(tkn // tm,),
        in_specs=[pl.BlockSpec((tm, d), lambda i: (i, 0)), pl.BlockSpec((tm, d), lambda i: (i, 0)),
                  pl.BlockSpec((tm, pd), lambda i: (i, 0)),
                  full((1, d)), full((d, d)), full((1, d)), full((pd, d)), full((1, d))],
        out_specs=pl.BlockSpec((tm, d), lambda i: (i, 0)),
        out_shape=jax.ShapeDtypeStruct((tkn, d), F32),
        compiler_params=_cparams(("arbitrary",)),
        name="ple_gate",
    )(h2d, y2d, p2d, gain, wg, bg, wp, fgain)


def _rope_table(seq):
    pos = jnp.arange(seq, dtype=F32)
    inv = ROPE_THETA ** (-jnp.arange(0, DIFF_HEAD_DIM, 2, dtype=F32) / DIFF_HEAD_DIM)
    ang = pos[:, None] * inv[None, :]
    ang = jnp.concatenate([ang, ang], axis=-1)
    cos, sin = jnp.cos(ang), jnp.sin(ang)
    lo = jnp.arange(DIFF_HEAD_DIM) < DIFF_HEAD_DIM // 2
    sin_lo = jnp.where(lo, -sin, 0.0)
    sin_hi = jnp.where(lo, 0.0, sin)
    rep = 128 // DIFF_HEAD_DIM
    return jnp.concatenate([jnp.tile(t, (1, rep)) for t in (cos, sin_lo, sin_hi)], axis=1)


def _block_diag(w):
    g, c, dd = w.shape
    out = jnp.zeros((g * c, g * dd), w.dtype)
    for i in range(g):
        out = out.at[i * c:(i + 1) * c, i * dd:(i + 1) * dd].set(w[i])
    return out


def _router_weights(wg, bg, we, be):
    d = wg.shape[0]
    w = jnp.zeros((d, ROUTER_LANES), F32)
    w = w.at[:, 0:N_GROUPS].set(wg).at[:, 8:8 + N_EXPERTS].set(we)
    w_hi = w.astype(BF16)
    w_lo = (w - w_hi.astype(F32)).astype(BF16)
    wr = w_hi.at[:, ROUTER_LO:ROUTER_LO + 8 + N_EXPERTS].set(w_lo[:, 0:8 + N_EXPERTS])
    br = jnp.zeros((1, ROUTER_LANES), F32)
    br = br.at[0, 0:N_GROUPS].set(bg).at[0, 8:8 + N_EXPERTS].set(be)
    return wr, br


def kernel(x, p, mix_norm, w_in, conf_conv_w, conf_conv_b, conf_ln_g, conf_ln_b, pool_w, pool_b, pool_scale, diff_lam_q1, diff_lam_k1, diff_lam_q2, diff_lam_k2, diff_subln_g, sconv_w, w_out, ffn_norm, router_group_w, router_group_b, router_expert_w, router_expert_b, expert_w_gate, expert_w_up, expert_w_down, ple_norm, ple_gate_w, ple_gate_b, ple_proj, final_norm):
    bsz, seq, d = x.shape
    depth = w_in.shape[0]
    tkn = bsz * seq
    rope = _rope_table(seq)
    att_t = min(ATT_TILE, seq)
    row = lambda v: v.reshape(1, -1).astype(F32)

    h = x
    for i in range(depth):
        lam_init = 0.8 - 0.6 * math.exp(-0.3 * i)
        yabd, qt, k, vt = _inproj_mix(
            h, row(mix_norm[i]), w_in[i].astype(BF16), rope,
            conf_conv_w[i], row(conf_conv_b[i]), row(conf_ln_g[i]), row(conf_ln_b[i]),
            _block_diag(pool_w[i]).astype(BF16), row(pool_b[i]), row(pool_scale[i]), sconv_w[i])
        lam_params = jnp.stack([diff_lam_q1[i], diff_lam_k1[i], diff_lam_q2[i], diff_lam_k2[i]]).astype(F32)
        g_bcast = jnp.broadcast_to(diff_subln_g[i].astype(F32)[:, None], (DIFF_V_DIM, att_t))
        yc = _diff_attn(qt, k, vt, lam_params, g_bcast, lam_init)

        wo = w_out[i].astype(BF16)
        wabd = jnp.concatenate([wo[0:2 * W_GROUP], wo[3 * W_GROUP:]], axis=0)
        wc = wo[2 * W_GROUP:3 * W_GROUP]
        wr, br = _router_weights(router_group_w[i], router_group_b[i], router_expert_w[i], router_expert_b[i])
        h1, xs, meta, cnt = _outproj_router(
            h.reshape(tkn, d), yabd.reshape(tkn, 3 * W_GROUP), yc.reshape(tkn, W_GROUP),
            wabd, wc, row(ffn_norm[i]), wr, br)

        tok_of_slot, sched = _moe_schedule(
            meta[:, 0, :], meta[:, 1, :],
            cnt[:N_BUCKETS, 0].astype(jnp.int32), MOE_TILE)
        wgu = jnp.concatenate([expert_w_gate[i], expert_w_up[i]], axis=-1).astype(BF16)
        y = _moe(xs, tok_of_slot, sched, wgu, expert_w_down[i].astype(BF16))

        h3 = _ple(h1, y, p[i].reshape(tkn, -1), row(ple_norm[i]), ple_gate_w[i].astype(BF16),
                  row(ple_gate_b[i]), ple_proj[i].astype(BF16), row(final_norm), final=(i == depth - 1))
        h = h3.reshape(bsz, seq, d)
    return h
```

```python
import functools
import math

import jax
import jax.numpy as jnp
from jax import lax
from jax.experimental import pallas as pl
from jax.experimental.pallas import tpu as pltpu

F32 = jnp.float32
BF16 = jnp.bfloat16

EPS = 1e-6
ROPE_THETA = 10000.0
W_GROUP = 256
CONF_KERNEL = 31
CONF_HIST = 32
SUBLANES = 8
POOL_HIST = 16
SCONV_KERNEL = 3
SCONV_HIST = 8
DIFF_HEADS = 4
DIFF_HEAD_DIM = 32
DIFF_V_DIM = 64
V_AUG_ROWS = 80
N_GROUPS = 4
EXPERTS_PER_GROUP = 4
N_EXPERTS = 16
D_EXPERT = 256
ROUTER_LANES = 128
ROUTER_LO = 32
NEG_BIG = -1e30
LOG2E = 1.4426950408889634

VMEM_LIMIT = 48 * 1024 * 1024

SEQ_TILE = 512
ROW_CHUNK = 64
ATT_TILE = 256
ATT_Q_BLOCKS = 2
TOK_TILE = 512
MOE_TILE = 256
N_BUCKETS = 24
N_BUCKET_ROWS = 32


def _cparams(sem):
    return pltpu.CompilerParams(dimension_semantics=sem, vmem_limit_bytes=VMEM_LIMIT)


def _rms_rows(x, g):
    ms = jnp.mean(x * x, axis=-1, keepdims=True)
    return x * lax.rsqrt(ms + EPS) * g


def _sigmoid(x):
    return 1.0 / (1.0 + jnp.exp(-x))


def _inproj_mix_kernel(h_ref, g_ref, w_ref, rope_ref, cw_ref, cb_ref, lg_ref, lb_ref,
                       pw_ref, pb_ref, ps_ref, sw_ref,
                       yabd_ref, qt_ref, k_ref, vt_ref,
                       gbuf, pbuf, zbuf, gshift, *, tm):
    s = pl.program_id(1)

    @pl.when(s == 0)
    def _():
        gbuf[0:CONF_HIST, :] = jnp.zeros((CONF_HIST, W_GROUP), F32)
        pbuf[0:POOL_HIST, :] = jnp.zeros((POOL_HIST, W_GROUP), F32)
        zbuf[0:SCONV_HIST, :] = jnp.zeros((SCONV_HIST, W_GROUP), F32)

    n = _rms_rows(h_ref[...], g_ref[...]).astype(BF16)

    def proj(lo, hi):
        return jnp.dot(n, w_ref[:, lo:hi], preferred_element_type=F32)

    a = proj(0, 2 * W_GROUP)
    gbuf[CONF_HIST:CONF_HIST + tm, :] = a[:, :W_GROUP] * _sigmoid(a[:, W_GROUP:])
    span = tm + CONF_HIST - SUBLANES
    for sh in range(1, SUBLANES):
        gshift[sh - 1, 0:span, :] = gbuf[sh:sh + span, :]
    base = CONF_HIST - (CONF_KERNEL - 1)
    for c in range(tm // ROW_CHUNK):
        r0 = c * ROW_CHUNK
        acc = jnp.broadcast_to(cb_ref[...], (ROW_CHUNK, W_GROUP))
        for j in range(CONF_KERNEL):
            sh = (base + j) % SUBLANES
            q0 = r0 + base + j - sh
            win = gbuf[q0:q0 + ROW_CHUNK, :] if sh == 0 else gshift[sh - 1, q0:q0 + ROW_CHUNK, :]
            acc = acc + cw_ref[j:j + 1, :] * win
        mu = jnp.mean(acc, axis=-1, keepdims=True)
        d = acc - mu
        var = jnp.mean(d * d, axis=-1, keepdims=True)
        y = d * lax.rsqrt(var + EPS) * lg_ref[...] + lb_ref[...]
        yabd_ref[r0:r0 + ROW_CHUNK, 0:W_GROUP] = (y * _sigmoid(y)).astype(BF16)
    gbuf[0:CONF_HIST, :] = gbuf[tm:tm + CONF_HIST, :]

    pbuf[POOL_HIST:POOL_HIST + tm, :] = proj(2 * W_GROUP, 3 * W_GROUP)
    lane = lax.broadcasted_iota(jnp.int32, (ROW_CHUNK, 128), 1)
    first = lane < 64
    for c in range(tm // ROW_CHUNK):
        r0 = c * ROW_CHUNK
        tpos = (s * tm + r0 + 1 + lax.broadcasted_iota(jnp.int32, (ROW_CHUNK, 128), 0)).astype(F32)
        halves = []
        for half, (w_small, w_big) in enumerate(((2, 4), (8, 16))):
            l0 = half * 128

            def ld(j):
                return pbuf[r0 + POOL_HIST - j:r0 + POOL_HIST - j + ROW_CHUNK, l0:l0 + 128]

            cur = ld(0)
            run = cur
            for j in range(1, w_small):
                run = run + ld(j)
            small = run
            for j in range(w_small, w_big):
                run = run + ld(j)
            cnt = jnp.where(first, jnp.minimum(tpos, float(w_small)), jnp.minimum(tpos, float(w_big)))
            halves.append(jnp.where(first, small, run) / cnt - cur)
        pc = jnp.concatenate(halves, axis=1).astype(BF16)
        yb = (jnp.dot(pc, pw_ref[...], preferred_element_type=F32) + pb_ref[...]) * ps_ref[...]
        yabd_ref[r0:r0 + ROW_CHUNK, W_GROUP:2 * W_GROUP] = yb.astype(BF16)
    pbuf[0:POOL_HIST, :] = pbuf[tm:tm + POOL_HIST, :]

    sc = proj(6 * W_GROUP, 9 * W_GROUP)
    zbuf[SCONV_HIST:SCONV_HIST + tm, :] = sc[:, W_GROUP:2 * W_GROUP] * sc[:, 2 * W_GROUP:]
    conv = None
    for j in range(SCONV_KERNEL):
        off = SCONV_HIST - (SCONV_KERNEL - 1) + j
        term = sw_ref[j:j + 1, :] * zbuf[off:off + tm, :]
        conv = term if conv is None else conv + term
    yabd_ref[:, 2 * W_GROUP:3 * W_GROUP] = (sc[:, :W_GROUP] * conv).astype(BF16)
    zbuf[0:SCONV_HIST, :] = zbuf[tm:tm + SCONV_HIST, :]

    qk = proj(3 * W_GROUP, 5 * W_GROUP)
    tab = rope_ref[...]
    cos = jnp.concatenate([tab[:, 0:128]] * 4, axis=1)
    sin_lo = jnp.concatenate([tab[:, 128:256]] * 4, axis=1)
    sin_hi = jnp.concatenate([tab[:, 256:384]] * 4, axis=1)
    half = DIFF_HEAD_DIM // 2
    width = 2 * W_GROUP
    qk = qk * cos + pltpu.roll(qk, width - half, axis=1) * sin_lo + pltpu.roll(qk, half, axis=1) * sin_hi
    q = qk[:, :W_GROUP] * (DIFF_HEAD_DIM ** -0.5 * LOG2E)
    qt_ref[...] = q.T.astype(BF16)
    k_ref[...] = qk[:, W_GROUP:].astype(BF16)
    vt = proj(5 * W_GROUP, 6 * W_GROUP).T.astype(BF16)
    for hd in range(DIFF_HEADS):
        r0 = hd * V_AUG_ROWS
        vt_ref[r0:r0 + DIFF_V_DIM, :] = vt[hd * DIFF_V_DIM:(hd + 1) * DIFF_V_DIM, :]
        vt_ref[r0 + DIFF_V_DIM:r0 + V_AUG_ROWS, :] = jnp.ones((V_AUG_ROWS - DIFF_V_DIM, tm), BF16)


def _inproj_mix(h, gain, w_in, rope, cw, cb, lg, lb, pw_bd, pb, ps, sw):
    bsz, seq, d = h.shape
    tm = min(SEQ_TILE, seq)
    assert seq % tm == 0 and tm % ROW_CHUNK == 0 and tm >= CONF_HIST
    ncol = w_in.shape[1]
    full = lambda shape: pl.BlockSpec(shape, lambda b, s: (0,) * len(shape))
    return pl.pallas_call(
        functools.partial(_inproj_mix_kernel, tm=tm),
        grid=(bsz, seq // tm),
        in_specs=[
            pl.BlockSpec((None, tm, d), lambda b, s: (b, s, 0)),
            full((1, d)),
            full((d, ncol)),
            pl.BlockSpec((tm, 384), lambda b, s: (s, 0)),
            full((CONF_KERNEL, W_GROUP)), full((1, W_GROUP)), full((1, W_GROUP)), full((1, W_GROUP)),
            full((W_GROUP, W_GROUP)), full((1, W_GROUP)), full((1, W_GROUP)),
            full((SCONV_KERNEL, W_GROUP)),
        ],
        out_specs=[
            pl.BlockSpec((None, tm, 3 * W_GROUP), lambda b, s: (b, s, 0)),
            pl.BlockSpec((None, W_GROUP, tm), lambda b, s: (b, 0, s)),
            pl.BlockSpec((None, tm, W_GROUP), lambda b, s: (b, s, 0)),
            pl.BlockSpec((None, DIFF_HEADS * V_AUG_ROWS, tm), lambda b, s: (b, 0, s)),
        ],
        out_shape=[
            jax.ShapeDtypeStruct((bsz, seq, 3 * W_GROUP), BF16),
            jax.ShapeDtypeStruct((bsz, W_GROUP, seq), BF16),
            jax.ShapeDtypeStruct((bsz, seq, W_GROUP), BF16),
            jax.ShapeDtypeStruct((bsz, DIFF_HEADS * V_AUG_ROWS, seq), BF16),
        ],
        scratch_shapes=[
            pltpu.VMEM((tm + CONF_HIST, W_GROUP), F32),
            pltpu.VMEM((tm + POOL_HIST, W_GROUP), F32),
            pltpu.VMEM((tm + SCONV_HIST, W_GROUP), F32),
            pltpu.VMEM((SUBLANES - 1, tm + CONF_HIST - SUBLANES, W_GROUP), F32),
        ],
        compiler_params=_cparams(("arbitrary", "arbitrary")),
        name="inproj_mix",
    )(h, gain, w_in, rope, cw, cb, lg, lb, pw_bd, pb, ps, sw)


def _diff_attn_kernel(qt_ref, k_ref, vt_ref, lam_ref, g_ref, o_ref,
                      qm_ref, s0_ref, s1_ref, p0_ref, p1_ref, al0_ref, al1_ref, m_ref, l_ref, acc_ref,
                      *, t, nq, lam_init):
    qi = pl.program_id(1)
    nhc = 2 * DIFF_HEADS
    tq = nq * t

    qm_ref[...] = jnp.zeros(qm_ref.shape, BF16)
    for j in range(nhc):
        lo = j * DIFF_HEAD_DIM
        qm_ref[lo:lo + DIFF_HEAD_DIM, j * tq:(j + 1) * tq] = qt_ref[lo:lo + DIFF_HEAD_DIM, :]
    m_ref[...] = jnp.full(m_ref.shape, NEG_BIG, F32)
    l_ref[...] = jnp.zeros(l_ref.shape, F32)
    acc_ref[...] = jnp.zeros(acc_ref.shape, F32)
    p1_ref[...] = jnp.zeros(p1_ref.shape, BF16)
    al1_ref[...] = jnp.ones(al1_ref.shape, F32)

    def scores(kt, s_ref):
        k0 = pl.multiple_of(kt * t, t)
        s_ref[...] = jnp.dot(k_ref[pl.ds(k0, t), :], qm_ref[...], preferred_element_type=F32)

    def softmax(s_ref, p_ref, al_ref, diag):
        if diag is not None:
            above = (lax.broadcasted_iota(jnp.int32, (t, t), 0) > lax.broadcasted_iota(jnp.int32, (t, t), 1))
        for j in range(nhc):
            for r in range(nq):
                cols = slice(j * tq + r * t, j * tq + (r + 1) * t)
                qs = slice(r * t, (r + 1) * t)
                if diag is not None and r < diag:
                    p_ref[:, cols] = jnp.zeros((t, t), BF16)
                    al_ref[j, :, qs] = jnp.ones((1, t), F32)
                    continue
                sc = s_ref[:, cols]
                if diag is not None and r == diag:
                    sc = jnp.where(above, NEG_BIG, sc)
                m_old = m_ref[j, :, qs]
                m_new = jnp.maximum(m_old, jnp.max(sc, axis=0, keepdims=True))
                al_ref[j, :, qs] = jnp.exp2(m_old - m_new)
                m_ref[j, :, qs] = m_new
                p_ref[:, cols] = jnp.exp2(sc - m_new).astype(BF16)

    def weighted_values(p_ref, al_ref, kt):
        k0 = pl.multiple_of(jnp.maximum(kt, 0) * t, t)
        for hd in range(DIFF_HEADS):
            vth = vt_ref[hd * V_AUG_ROWS:(hd + 1) * V_AUG_ROWS, pl.ds(k0, t)]
            o = jnp.dot(vth, p_ref[:, 2 * hd * tq:(2 * hd + 2) * tq], preferred_element_type=F32)
            for c in range(2):
                j = 2 * hd + c
                alpha = al_ref[j]
                acc_ref[j] = alpha * acc_ref[j] + o[0:DIFF_V_DIM, c * tq:(c + 1) * tq]
                l_ref[j] = alpha * l_ref[j] + o[DIFF_V_DIM:DIFF_V_DIM + 1, c * tq:(c + 1) * tq]

    scores(0, s0_ref)

    def pair(i, carry):
        kt = 2 * i
        scores(kt + 1, s1_ref)
        softmax(s0_ref, p0_ref, al0_ref, None)
        weighted_values(p1_ref, al1_ref, kt - 1)
        scores(kt + 2, s0_ref)
        softmax(s1_ref, p1_ref, al1_ref, None)
        weighted_values(p0_ref, al0_ref, kt)
        return carry

    lax.fori_loop(0, qi, pair, 0)

    kd = 2 * qi
    scores(kd + 1, s1_ref)
    softmax(s0_ref, p0_ref, al0_ref, 0)
    weighted_values(p1_ref, al1_ref, kd - 1)
    softmax(s1_ref, p1_ref, al1_ref, 1)
    weighted_values(p0_ref, al0_ref, kd)
    weighted_values(p1_ref, al1_ref, kd + 1)

    lp = lam_ref[...]
    lam = (jnp.exp(jnp.sum(lp[0:1] * lp[1:2], axis=-1, keepdims=True))
           - jnp.exp(jnp.sum(lp[2:3] * lp[3:4], axis=-1, keepdims=True)) + lam_init)
    outs = []
    for hd in range(DIFF_HEADS):
        o1 = acc_ref[2 * hd] / l_ref[2 * hd]
        o2 = acc_ref[2 * hd + 1] / l_ref[2 * hd + 1]
        o = o1 - lam * o2
        ms = jnp.mean(o * o, axis=0, keepdims=True)
        outs.append(o * lax.rsqrt(ms + EPS) * g_ref[...] * (1.0 - lam_init))
    o_ref[...] = jnp.concatenate(outs, axis=0).T.astype(BF16)


def _diff_attn(qt, k, vt, lam_params, g_bcast, lam_init):
    bsz, seq, _ = k.shape
    t, nq = ATT_TILE, ATT_Q_BLOCKS
    tq = nq * t
    assert seq % tq == 0 and nq == 2
    nhc = 2 * DIFF_HEADS
    return pl.pallas_call(
        functools.partial(_diff_attn_kernel, t=t, nq=nq, lam_init=lam_init),
        grid=(bsz, seq // tq),
        in_specs=[
            pl.BlockSpec((None, W_GROUP, tq), lambda b, q: (b, 0, q)),
            pl.BlockSpec((None, seq, W_GROUP), lambda b, q: (b, 0, 0)),
            pl.BlockSpec((None, DIFF_HEADS * V_AUG_ROWS, seq), lambda b, q: (b, 0, 0)),
            pl.BlockSpec((4, DIFF_HEAD_DIM), lambda b, q: (0, 0)),
            pl.BlockSpec((DIFF_V_DIM, tq), lambda b, q: (0, 0)),
        ],
        out_specs=pl.BlockSpec((None, tq, W_GROUP), lambda b, q: (b, q, 0)),
        out_shape=jax.ShapeDtypeStruct((bsz, seq, W_GROUP), BF16),
        scratch_shapes=[
            pltpu.VMEM((W_GROUP, nhc * tq), BF16),
            pltpu.VMEM((t, nhc * tq), F32), pltpu.VMEM((t, nhc * tq), F32),
            pltpu.VMEM((t, nhc * tq), BF16), pltpu.VMEM((t, nhc * tq), BF16),
            pltpu.VMEM((nhc, 1, tq), F32), pltpu.VMEM((nhc, 1, tq), F32),
            pltpu.VMEM((nhc, 1, tq), F32),
            pltpu.VMEM((nhc, 1, tq), F32),
            pltpu.VMEM((nhc, DIFF_V_DIM, tq), F32),
        ],
        compiler_params=_cparams(("arbitrary", "arbitrary")),
        name="diff_attn",
    )(qt, k, vt, lam_params, g_bcast)


def _outproj_router_kernel(h_ref, yabd_ref, yc_ref, wabd_ref, wc_ref, g_ref, wr_ref, br_ref,
                           h1_ref, xs_ref, meta_ref, cnt_ref, rt_ref, ct_ref, oh_ref, carry_ref, *, tm):
    d = h_ref.shape[1]

    @pl.when(pl.program_id(0) == 0)
    def _():
        carry_ref[...] = jnp.zeros(carry_ref.shape, F32)

    h1 = (h_ref[...]
          + jnp.dot(yabd_ref[...], wabd_ref[...], preferred_element_type=F32)
          + jnp.dot(yc_ref[...], wc_ref[...], preferred_element_type=F32))
    h1_ref[...] = h1
    xn = _rms_rows(h1, g_ref[...])
    x_hi = xn.astype(BF16)
    x_lo = (xn - x_hi.astype(F32)).astype(BF16)
    xs_ref[:, 0:d] = xn
    r = (jnp.dot(x_hi, wr_ref[...], preferred_element_type=F32)
         + jnp.dot(x_lo, wr_ref[...], preferred_element_type=F32) + br_ref[...])
    rt_ref[...] = r.T

    def row(i):
        return rt_ref[i:i + 1, :] + rt_ref[ROUTER_LO + i:ROUTER_LO + i + 1, :]

    def first_hits(vals, target):
        hits, free = [], None
        for v in vals:
            f = jnp.where(v == target, 1.0, 0.0)
            hits.append(f if free is None else f * free)
            free = (1.0 - f) if free is None else free * (1.0 - f)
        return hits

    add = lambda a, b: a + b
    gl = [row(i) for i in range(N_GROUPS)]
    el = [row(8 + e) for e in range(N_EXPERTS)]
    gmax = functools.reduce(jnp.maximum, gl)
    gsel = first_hits(gl, gmax)
    pg_c = 1.0 / functools.reduce(add, [jnp.exp(v - gmax) for v in gl])
    elc = []
    for kk in range(EXPERTS_PER_GROUP):
        v = gsel[0] * el[kk]
        for g in range(1, N_GROUPS):
            v = v + gsel[g] * el[g * EXPERTS_PER_GROUP + kk]
        elc.append(v)
    emax = functools.reduce(jnp.maximum, elc)
    ex = [jnp.exp(v - emax) for v in elc]
    se = functools.reduce(add, ex)
    pe = [v / se for v in ex]
    p1 = functools.reduce(jnp.maximum, pe)
    t1 = first_hits(pe, p1)
    rest = [jnp.where(t > 0.0, -1.0, v) for t, v in zip(t1, pe)]
    p2 = functools.reduce(jnp.maximum, rest)
    t2 = first_hits(rest, p2)
    den = p1 + p2
    w1 = pg_c * (p1 / den)
    w2 = pg_c * (p2 / den)
    wk = [a * w1 + b * w2 for a, b in zip(t1, t2)]
    sel = [a + b for a, b in zip(t1, t2)]
    lower = first_hits(sel, 1.0)
    c_a = functools.reduce(add, [f * w for f, w in zip(lower, wk)])
    c_b = functools.reduce(add, [(s - f) * w for s, f, w in zip(sel, lower, wk)])
    pair = (sel[0] * sel[2] + 2.0 * sel[0] * sel[3] + 3.0 * sel[1] * sel[2]
            + 4.0 * sel[1] * sel[3] + 5.0 * sel[2] * sel[3])
    bucket = 6.0 * (gsel[1] + 2.0 * gsel[2] + 3.0 * gsel[3]) + pair

    ct_ref[...] = jnp.zeros(ct_ref.shape, F32)
    ct_ref[0:1, :] = c_a
    ct_ref[1:2, :] = c_b
    xs_ref[:, d:d + ROUTER_LANES] = ct_ref[...].T

    for b in range(N_BUCKET_ROWS):
        oh_ref[b:b + 1, :] = jnp.where(bucket == float(b), 1.0, 0.0)
    oh = oh_ref[...]
    upper = (lax.broadcasted_iota(jnp.int32, (tm, tm), 0) <= lax.broadcasted_iota(jnp.int32, (tm, tm), 1))
    cum = jnp.dot(oh.astype(BF16), jnp.where(upper, 1.0, 0.0).astype(BF16), preferred_element_type=F32)
    carry = carry_ref[...]
    rank = jnp.sum(oh * (cum - 1.0 + carry), axis=0, keepdims=True)
    carry_new = carry + jnp.sum(oh, axis=1, keepdims=True)
    carry_ref[...] = carry_new
    cnt_ref[...] = carry_new[:, 0:ROUTER_LANES]
    meta_ref[...] = jnp.zeros(meta_ref.shape, jnp.int32)
    meta_ref[0:1, :] = bucket.astype(jnp.int32)
    meta_ref[1:2, :] = rank.astype(jnp.int32)


def _outproj_router(h2d, yabd, yc, wabd, wc, gain, wr, br):
    tkn, d = h2d.shape
    tm = min(TOK_TILE, tkn)
    assert tkn % tm == 0
    row = lambda w: pl.BlockSpec((tm, w), lambda i: (i, 0))
    full = lambda shape: pl.BlockSpec(shape, lambda i: (0,) * len(shape))
    return pl.pallas_call(
        functools.partial(_outproj_router_kernel, tm=tm),
        grid=(tkn // tm,),
        in_specs=[row(d), row(3 * W_GROUP), row(W_GROUP),
                  full((3 * W_GROUP, d)), full((W_GROUP, d)), full((1, d)),
                  full((d, ROUTER_LANES)), full((1, ROUTER_LANES))],
        out_specs=[row(d), row(d + ROUTER_LANES),
                   pl.BlockSpec((None, 8, tm), lambda i: (i, 0, 0)),
                   full((N_BUCKET_ROWS, ROUTER_LANES))],
        out_shape=[jax.ShapeDtypeStruct((tkn, d), F32),
                   jax.ShapeDtypeStruct((tkn, d + ROUTER_LANES), F32),
                   jax.ShapeDtypeStruct((tkn // tm, 8, tm), jnp.int32),
                   jax.ShapeDtypeStruct((N_BUCKET_ROWS, ROUTER_LANES), F32)],
        scratch_shapes=[pltpu.VMEM((ROUTER_LANES, tm), F32), pltpu.VMEM((ROUTER_LANES, tm), F32),
                        pltpu.VMEM((N_BUCKET_ROWS, tm), F32), pltpu.VMEM((N_BUCKET_ROWS, tm), F32)],
        compiler_params=_cparams(("arbitrary",)),
        name="outproj_router",
    )(h2d, yabd, yc, wabd, wc, gain, wr, br)


def _moe_kernel(vt_ref, vea_ref, veb_ref, vlo_ref, vhi_ref, vfl_ref,
                tok_ref, tokn_ref, xs_hbm, wgua_ref, wgub_ref, wda_ref, wdb_ref,
                y_hbm, xbuf, ybuf, gsem, ssem, *, tm, n_tiles, n_visits):
    v = pl.program_id(0)
    tile = vt_ref[v]
    flags = vfl_ref[v]
    first = (flags & 1) != 0
    last = (flags & 2) != 0
    valid = (flags & 4) != 0
    slot = tile % 2
    d = y_hbm.shape[1]

    def row_gather(ids_ref, s, k, u):
        return pltpu.make_async_copy(xs_hbm.at[pl.ds(ids_ref[0, k * SUBLANES + u], 1), :],
                                     xbuf.at[s, k, pl.ds(u, 1), :], gsem.at[s])

    def row_scatter(ids_ref, s, k, u):
        return pltpu.make_async_copy(ybuf.at[s, k, pl.ds(u, 1), :],
                                     y_hbm.at[pl.ds(ids_ref[0, k * SUBLANES + u], 1), :], ssem.at[s])

    def start_all(mk, ids_ref, s):
        def body(k, c):
            for u in range(SUBLANES):
                mk(ids_ref, s, k, u).start(priority=u % 2)
            return c
        lax.fori_loop(0, tm // SUBLANES, body, 0)

    def wait_gather(s):
        pltpu.make_async_copy(xbuf.at[s], xbuf.at[s], gsem.at[s]).wait()

    def wait_scatter(s):
        pltpu.make_async_copy(ybuf.at[s], ybuf.at[s], ssem.at[s]).wait()

    @pl.when(v == 0)
    def _():
        start_all(row_gather, tok_ref, 0)

    @pl.when(jnp.logical_and(first, valid))
    def _():
        wait_gather(slot)

        @pl.when(tile + 1 < n_tiles)
        def _():
            start_all(row_gather, tokn_ref, 1 - slot)

        @pl.when(tile >= 2)
        def _():
            wait_scatter(slot)

    @pl.when(valid)
    def _():
        rows = tile * tm + lax.broadcasted_iota(jnp.int32, (tm, 1), 0)
        inb = jnp.where(jnp.logical_and(rows >= vlo_ref[v], rows < vhi_ref[v]), 1.0, 0.0)
        xt = xbuf[slot].reshape(tm, xs_hbm.shape[1])
        x = xt[:, 0:d].astype(BF16)
        c_a = xt[:, d:d + 1] * inb
        c_b = xt[:, d + 1:d + 2] * inb

        def expert(wgu_ref, wd_ref, c):
            gu = jnp.dot(x, wgu_ref[...], preferred_element_type=F32)
            gate = gu[:, :D_EXPERT]
            hdn = gate * _sigmoid(gate) * gu[:, D_EXPERT:] * c
            return jnp.dot(hdn.astype(BF16), wd_ref[...], preferred_element_type=F32)

        y = expert(wgua_ref, wda_ref, c_a) + expert(wgub_ref, wdb_ref, c_b)

        y = y.reshape(tm // SUBLANES, SUBLANES, d)

        @pl.when(first)
        def _():
            ybuf[slot] = y

        @pl.when(jnp.logical_not(first))
        def _():
            ybuf[slot] = ybuf[slot] + y

    @pl.when(jnp.logical_and(last, valid))
    def _():
        start_all(row_scatter, tok_ref, slot)

    @pl.when(v == n_visits - 1)
    def _():
        wait_scatter(0)
        wait_scatter(1)


def _moe(xs, tok_of_slot, sched, wgu, wd):
    tkn, dx = xs.shape
    d = dx - ROUTER_LANES
    tm = MOE_TILE
    assert tkn % tm == 0 and tkn // tm >= 2
    n_tiles = tkn // tm
    n_visits = sched[0].shape[0]
    tok3 = tok_of_slot.reshape(n_tiles, 1, tm)
    wspec = lambda shape, which: pl.BlockSpec(
        (None,) + shape, lambda v, vt, vea, veb, vlo, vhi, vfl: ((vea, veb)[which][v], 0, 0))
    grid_spec = pltpu.PrefetchScalarGridSpec(
        num_scalar_prefetch=6,
        grid=(n_visits,),
        in_specs=[
            pl.BlockSpec((None, 1, tm), lambda v, vt, *_: (vt[v], 0, 0), memory_space=pltpu.SMEM),
            pl.BlockSpec((None, 1, tm), lambda v, vt, *_: (jnp.minimum(vt[v] + 1, n_tiles - 1), 0, 0),
                         memory_space=pltpu.SMEM),
            pl.BlockSpec(memory_space=pl.ANY),
            wspec((d, 2 * D_EXPERT), 0), wspec((d, 2 * D_EXPERT), 1),
            wspec((D_EXPERT, d), 0), wspec((D_EXPERT, d), 1),
        ],
        out_specs=pl.BlockSpec(memory_space=pl.ANY),
        scratch_shapes=[pltpu.VMEM((2, tm // SUBLANES, SUBLANES, dx), F32),
                        pltpu.VMEM((2, tm // SUBLANES, SUBLANES, d), F32),
                        pltpu.SemaphoreType.DMA((2,)), pltpu.SemaphoreType.DMA((2,))],
    )
    return pl.pallas_call(
        functools.partial(_moe_kernel, tm=tm, n_tiles=n_tiles, n_visits=n_visits),
        grid_spec=grid_spec,
        out_shape=jax.ShapeDtypeStruct((tkn, d), F32),
        compiler_params=_cparams(("arbitrary",)),
        name="moe_experts",
    )(*sched, tok3, tok3, xs, wgu, wgu, wd, wd)


def _moe_schedule(bucket, rank, counts, tm):
    tkn = bucket.size
    n_tiles = tkn // tm
    n_visits = n_tiles + N_BUCKETS
    ends = jnp.cumsum(counts)
    offs = ends - counts
    slot = rank
    for bb in range(N_BUCKETS):
        slot = slot + jnp.where(bucket == bb, offs[bb], 0)
    _, tok_of_slot = lax.sort_key_val(slot.reshape(tkn), jnp.arange(tkn, dtype=jnp.int32))
    t_first = offs // tm
    t_last = jnp.maximum(ends - 1, 0) // tm
    nv = jnp.where(counts > 0, t_last - t_first + 1, 0)
    v_end = jnp.cumsum(nv)
    v_start = v_end - nv
    total = v_end[-1]
    v = jnp.arange(n_visits, dtype=jnp.int32)
    vc = jnp.minimum(v, total - 1)
    b = jnp.sum((v_end[None, :] <= vc[:, None]).astype(jnp.int32), axis=1)
    onehot = b[:, None] == jnp.arange(N_BUCKETS, dtype=jnp.int32)[None, :]
    pick = lambda table: jnp.sum(jnp.where(onehot, table[None, :], 0), axis=1)
    tile = (pick(t_first) + vc - pick(v_start)).astype(jnp.int32)
    valid = v < total
    prev_tile = jnp.concatenate([jnp.full((1,), -1, jnp.int32), tile[:-1]])
    next_tile = jnp.concatenate([tile[1:], jnp.full((1,), -1, jnp.int32)])
    first = tile != prev_tile
    last = jnp.logical_or(tile != next_tile, v == total - 1)
    flags = (first.astype(jnp.int32) + 2 * last.astype(jnp.int32) + 4) * valid.astype(jnp.int32)
    g, pr = b // 6, b % 6
    lo_e = (pr >= 3).astype(jnp.int32) + (pr >= 5).astype(jnp.int32)
    hi_e = pr + 1 - 2 * (pr >= 3).astype(jnp.int32) - (pr >= 5).astype(jnp.int32)
    ea = g * EXPERTS_PER_GROUP + lo_e
    eb = g * EXPERTS_PER_GROUP + hi_e
    sched = (tile, ea.astype(jnp.int32), eb.astype(jnp.int32),
             pick(offs).astype(jnp.int32), pick(ends).astype(jnp.int32), flags.astype(jnp.int32))
    return tok_of_slot, sched


def _ple_kernel(h_ref, y_ref, p_ref, g_ref, wg_ref, bg_ref, wp_ref, fg_ref, o_ref, *, final):
    h = h_ref[...] + y_ref[...]
    xn = _rms_rows(h, g_ref[...]).astype(BF16)
    gate = _sigmoid(jnp.dot(xn, wg_ref[...], preferred_element_type=F32) + bg_ref[...])
    out = h + gate * jnp.dot(p_ref[...].astype(BF16), wp_ref[...], preferred_element_type=F32)
    if final:
        out = _rms_rows(out, fg_ref[...])
    o_ref[...] = out


def _ple(h2d, y2d, p2d, gain, wg, bg, wp, fgain, final):
    tkn, d = h2d.shape
    pd = p2d.shape[1]
    tm = min(TOK_TILE, tkn)
    assert tkn % tm == 0
    full = lambda shape: pl.BlockSpec(shape, lambda i: (0,) * len(shape))
    return pl.pallas_call(
        functools.partial(_ple_kernel, final=final),
        grid=(tkn // tm,),
        in_specs=[pl.BlockSpec((tm, d), lambda i: (i, 0)), pl.BlockSpec((tm, d), lambda i: (i, 0)),
                  pl.BlockSpec((tm, pd), lambda i: (i, 0)),
                  full((1, d)), full((d, d)), full((1, d)), full((pd, d)), full((1, d))],
        out_specs=pl.BlockSpec((tm, d), lambda i: (i, 0)),
        out_shape=jax.ShapeDtypeStruct((tkn, d), F32),
        compiler_params=_cparams(("arbitrary",)),
        name="ple_gate",
    )(h2d, y2d, p2d, gain, wg, bg, wp, fgain)


def _rope_table(seq):
    pos = jnp.arange(seq, dtype=F32)
    inv = ROPE_THETA ** (-jnp.arange(0, DIFF_HEAD_DIM, 2, dtype=F32) / DIFF_HEAD_DIM)
    ang = pos[:, None] * inv[None, :]
    ang = jnp.concatenate([ang, ang], axis=-1)
    cos, sin = jnp.cos(ang), jnp.sin(ang)
    lo = jnp.arange(DIFF_HEAD_DIM) < DIFF_HEAD_DIM // 2
    sin_lo = jnp.where(lo, -sin, 0.0)
    sin_hi = jnp.where(lo, 0.0, sin)
    rep = 128 // DIFF_HEAD_DIM
    return jnp.concatenate([jnp.tile(t, (1, rep)) for t in (cos, sin_lo, sin_hi)], axis=1)


def _block_diag(w):
    g, c, dd = w.shape
    out = jnp.zeros((g * c, g * dd), w.dtype)
    for i in range(g):
        out = out.at[i * c:(i + 1) * c, i * dd:(i + 1) * dd].set(w[i])
    return out


def _router_weights(wg, bg, we, be):
    d = wg.shape[0]
    w = jnp.zeros((d, ROUTER_LANES), F32)
    w = w.at[:, 0:N_GROUPS].set(wg).at[:, 8:8 + N_EXPERTS].set(we)
    w_hi = w.astype(BF16)
    w_lo = (w - w_hi.astype(F32)).astype(BF16)
    wr = w_hi.at[:, ROUTER_LO:ROUTER_LO + 8 + N_EXPERTS].set(w_lo[:, 0:8 + N_EXPERTS])
    br = jnp.zeros((1, ROUTER_LANES), F32)
    br = br.at[0, 0:N_GROUPS].set(bg).at[0, 8:8 + N_EXPERTS].set(be)
    return wr, br


def kernel(x, p, mix_norm, w_in, conf_conv_w, conf_conv_b, conf_ln_g, conf_ln_b, pool_w, pool_b, pool_scale, diff_lam_q1, diff_lam_k1, diff_lam_q2, diff_lam_k2, diff_subln_g, sconv_w, w_out, ffn_norm, router_group_w, router_group_b, router_expert_w, router_expert_b, expert_w_gate, expert_w_up, expert_w_down, ple_norm, ple_gate_w, ple_gate_b, ple_proj, final_norm):
    bsz, seq, d = x.shape
    depth = w_in.shape[0]
    tkn = bsz * seq
    rope = _rope_table(seq)
    att_t = ATT_TILE * ATT_Q_BLOCKS
    row = lambda v: v.reshape(1, -1).astype(F32)

    h = x
    for i in range(depth):
        lam_init = 0.8 - 0.6 * math.exp(-0.3 * i)
        yabd, qt, k, vt = _inproj_mix(
            h, row(mix_norm[i]), w_in[i].astype(BF16), rope,
            conf_conv_w[i], row(conf_conv_b[i]), row(conf_ln_g[i]), row(conf_ln_b[i]),
            _block_diag(pool_w[i]).astype(BF16), row(pool_b[i]), row(pool_scale[i]), sconv_w[i])
        lam_params = jnp.stack([diff_lam_q1[i], diff_lam_k1[i], diff_lam_q2[i], diff_lam_k2[i]]).astype(F32)
        g_bcast = jnp.broadcast_to(diff_subln_g[i].astype(F32)[:, None], (DIFF_V_DIM, att_t))
        yc = _diff_attn(qt, k, vt, lam_params, g_bcast, lam_init)

        wo = w_out[i].astype(BF16)
        wabd = jnp.concatenate([wo[0:2 * W_GROUP], wo[3 * W_GROUP:]], axis=0)
        wc = wo[2 * W_GROUP:3 * W_GROUP]
        wr, br = _router_weights(router_group_w[i], router_group_b[i], router_expert_w[i], router_expert_b[i])
        h1, xs, meta, cnt = _outproj_router(
            h.reshape(tkn, d), yabd.reshape(tkn, 3 * W_GROUP), yc.reshape(tkn, W_GROUP),
            wabd, wc, row(ffn_norm[i]), wr, br)

        tok_of_slot, sched = _moe_schedule(
            meta[:, 0, :], meta[:, 1, :],
            cnt[:N_BUCKETS, 0].astype(jnp.int32), MOE_TILE)
        wgu = jnp.concatenate([expert_w_gate[i], expert_w_up[i]], axis=-1).astype(BF16)
        y = _moe(xs, tok_of_slot, sched, wgu, expert_w_down[i].astype(BF16))

        h3 = _ple(h1, y, p[i].reshape(tkn, -1), row(ple_norm[i]), ple_gate_w[i].astype(BF16),
                  row(ple_gate_b[i]), ple_proj[i].astype(BF16), row(final_norm), final=(i == depth - 1))
        h = h3.reshape(bsz, seq, d)
    return h
```

```python
import functools
import math

import jax
import jax.numpy as jnp
import numpy as np
from jax import lax
from jax.experimental import pallas as pl
from jax.experimental.pallas import tpu as pltpu

F32 = jnp.float32
BF16 = jnp.bfloat16

EPS = 1e-6
ROPE_THETA = 10000.0
W_GROUP = 256
CONF_KERNEL = 31
CONF_HIST = 32
SUBLANES = 8
POOL_HIST = 16
SCONV_KERNEL = 3
SCONV_HIST = 8
DIFF_HEADS = 4
DIFF_HEAD_DIM = 32
DIFF_V_DIM = 64
V_AUG_ROWS = 80
N_GROUPS = 4
EXPERTS_PER_GROUP = 4
N_EXPERTS = 16
D_EXPERT = 256
ROUTER_LANES = 128
ROUTER_LO = 32
NEG_BIG = -1e30
LOG2E = 1.4426950408889634

VMEM_LIMIT = 48 * 1024 * 1024

SEQ_TILE = 512
ROW_CHUNK = 64
ATT_TILE = 256
ATT_Q_BLOCKS = 2
TOK_TILE = 512
MOE_TILE = 256
N_BUCKETS = 24
N_BUCKET_ROWS = 32


def _cparams(sem):
    return pltpu.CompilerParams(dimension_semantics=sem, vmem_limit_bytes=VMEM_LIMIT)


def _rms_rows(x, g):
    ms = jnp.mean(x * x, axis=-1, keepdims=True)
    return x * lax.rsqrt(ms + EPS) * g


def _sigmoid(x):
    return 1.0 / (1.0 + jnp.exp(-x))


def _ple_update(h, y, p, g, wg, bg, wp):
    h = h + y
    xn = _rms_rows(h, g).astype(BF16)
    gate = _sigmoid(jnp.dot(xn, wg, preferred_element_type=F32) + bg)
    return h + gate * jnp.dot(p.astype(BF16), wp, preferred_element_type=F32)


def _inproj_mix_kernel(*refs, tm, fused_ple):
    if fused_ple:
        (h_ref, y_ref, p_ref, pg_ref, pwg_ref, pbg_ref, pwp_ref), refs = refs[:7], refs[7:]
    else:
        h_ref, refs = refs[0], refs[1:]
    (g_ref, w_ref, rope_ref, cw_ref, cb_ref, lg_ref, lb_ref, pw_ref, pb_ref, ps_ref, sw_ref), refs = refs[:11], refs[11:]
    if fused_ple:
        hout_ref, refs = refs[0], refs[1:]
    yabd_ref, qt_ref, k_ref, vt_ref, gbuf, pbuf, zbuf, gshift = refs
    s = pl.program_id(1)

    @pl.when(s == 0)
    def _():
        gbuf[0:CONF_HIST, :] = jnp.zeros((CONF_HIST, W_GROUP), F32)
        pbuf[0:POOL_HIST, :] = jnp.zeros((POOL_HIST, W_GROUP), F32)
        zbuf[0:SCONV_HIST, :] = jnp.zeros((SCONV_HIST, W_GROUP), F32)

    if fused_ple:
        h = _ple_update(h_ref[...], y_ref[...], p_ref[...], pg_ref[...], pwg_ref[...], pbg_ref[...], pwp_ref[...])
        hout_ref[...] = h
    else:
        h = h_ref[...]
    n = _rms_rows(h, g_ref[...]).astype(BF16)

    def proj(lo, hi):
        return jnp.dot(n, w_ref[:, lo:hi], preferred_element_type=F32)

    a = proj(0, 2 * W_GROUP)
    gbuf[CONF_HIST:CONF_HIST + tm, :] = a[:, :W_GROUP] * _sigmoid(a[:, W_GROUP:])
    span = tm + CONF_HIST - SUBLANES
    for sh in range(1, SUBLANES):
        gshift[sh - 1, 0:span, :] = gbuf[sh:sh + span, :]
    base = CONF_HIST - (CONF_KERNEL - 1)
    for c in range(tm // ROW_CHUNK):
        r0 = c * ROW_CHUNK
        acc = jnp.broadcast_to(cb_ref[...], (ROW_CHUNK, W_GROUP))
        for j in range(CONF_KERNEL):
            sh = (base + j) % SUBLANES
            q0 = r0 + base + j - sh
            win = gbuf[q0:q0 + ROW_CHUNK, :] if sh == 0 else gshift[sh - 1, q0:q0 + ROW_CHUNK, :]
            acc = acc + cw_ref[j:j + 1, :] * win
        mu = jnp.mean(acc, axis=-1, keepdims=True)
        d = acc - mu
        var = jnp.mean(d * d, axis=-1, keepdims=True)
        y = d * lax.rsqrt(var + EPS) * lg_ref[...] + lb_ref[...]
        yabd_ref[r0:r0 + ROW_CHUNK, 0:W_GROUP] = (y * _sigmoid(y)).astype(BF16)
    gbuf[0:CONF_HIST, :] = gbuf[tm:tm + CONF_HIST, :]

    pbuf[POOL_HIST:POOL_HIST + tm, :] = proj(2 * W_GROUP, 3 * W_GROUP)
    lane = lax.broadcasted_iota(jnp.int32, (ROW_CHUNK, 128), 1)
    first = lane < 64
    for c in range(tm // ROW_CHUNK):
        r0 = c * ROW_CHUNK
        tpos = (s * tm + r0 + 1 + lax.broadcasted_iota(jnp.int32, (ROW_CHUNK, 128), 0)).astype(F32)
        halves = []
        for half, (w_small, w_big) in enumerate(((2, 4), (8, 16))):
            l0 = half * 128

            def ld(j):
                return pbuf[r0 + POOL_HIST - j:r0 + POOL_HIST - j + ROW_CHUNK, l0:l0 + 128]

            cur = ld(0)
            run = cur
            for j in range(1, w_small):
                run = run + ld(j)
            small = run
            for j in range(w_small, w_big):
                run = run + ld(j)
            cnt = jnp.where(first, jnp.minimum(tpos, float(w_small)), jnp.minimum(tpos, float(w_big)))
            halves.append(jnp.where(first, small, run) / cnt - cur)
        pc = jnp.concatenate(halves, axis=1).astype(BF16)
        yb = (jnp.dot(pc, pw_ref[...], preferred_element_type=F32) + pb_ref[...]) * ps_ref[...]
        yabd_ref[r0:r0 + ROW_CHUNK, W_GROUP:2 * W_GROUP] = yb.astype(BF16)
    pbuf[0:POOL_HIST, :] = pbuf[tm:tm + POOL_HIST, :]

    sc = proj(6 * W_GROUP, 9 * W_GROUP)
    zbuf[SCONV_HIST:SCONV_HIST + tm, :] = sc[:, W_GROUP:2 * W_GROUP] * sc[:, 2 * W_GROUP:]
    conv = None
    for j in range(SCONV_KERNEL):
        off = SCONV_HIST - (SCONV_KERNEL - 1) + j
        term = sw_ref[j:j + 1, :] * zbuf[off:off + tm, :]
        conv = term if conv is None else conv + term
    yabd_ref[:, 2 * W_GROUP:3 * W_GROUP] = (sc[:, :W_GROUP] * conv).astype(BF16)
    zbuf[0:SCONV_HIST, :] = zbuf[tm:tm + SCONV_HIST, :]

    qk = proj(3 * W_GROUP, 5 * W_GROUP)
    tab = rope_ref[...]
    cos = jnp.concatenate([tab[:, 0:128]] * 4, axis=1)
    sin_lo = jnp.concatenate([tab[:, 128:256]] * 4, axis=1)
    sin_hi = jnp.concatenate([tab[:, 256:384]] * 4, axis=1)
    half = DIFF_HEAD_DIM // 2
    width = 2 * W_GROUP
    qk = qk * cos + pltpu.roll(qk, width - half, axis=1) * sin_lo + pltpu.roll(qk, half, axis=1) * sin_hi
    q = qk[:, :W_GROUP] * (DIFF_HEAD_DIM ** -0.5 * LOG2E)
    qt_ref[...] = q.T.astype(BF16)
    k_ref[...] = qk[:, W_GROUP:].astype(BF16)
    vt = proj(5 * W_GROUP, 6 * W_GROUP).T.astype(BF16)
    for hd in range(DIFF_HEADS):
        r0 = hd * V_AUG_ROWS
        vt_ref[r0:r0 + DIFF_V_DIM, :] = vt[hd * DIFF_V_DIM:(hd + 1) * DIFF_V_DIM, :]
        vt_ref[r0 + DIFF_V_DIM:r0 + V_AUG_ROWS, :] = jnp.ones((V_AUG_ROWS - DIFF_V_DIM, tm), BF16)


def _inproj_mix(h, gain, w_in, rope, cw, cb, lg, lb, pw_bd, pb, ps, sw, ple=None):
    bsz, seq, d = h.shape
    tm = min(SEQ_TILE, seq)
    assert seq % tm == 0 and tm % ROW_CHUNK == 0 and tm >= CONF_HIST
    ncol = w_in.shape[1]
    full = lambda shape: pl.BlockSpec(shape, lambda b, s: (0,) * len(shape))
    tok = lambda w: pl.BlockSpec((None, tm, w), lambda b, s: (b, s, 0))
    tok_t = lambda rows: pl.BlockSpec((None, rows, tm), lambda b, s: (b, 0, s))
    args, in_specs = [h], [tok(d)]
    out_specs, out_shape = [], []
    if ple is not None:
        y, p, pgain, pwg, pbg, pwp = ple
        pd = p.shape[-1]
        args += [y, p, pgain, pwg, pbg, pwp]
        in_specs += [tok(d), tok(pd), full((1, d)), full((d, d)), full((1, d)), full((pd, d))]
        out_specs.append(tok(d))
        out_shape.append(jax.ShapeDtypeStruct((bsz, seq, d), F32))
    args += [gain, w_in, rope, cw, cb, lg, lb, pw_bd, pb, ps, sw]
    in_specs += [
        full((1, d)),
        full((d, ncol)),
        pl.BlockSpec((tm, 384), lambda b, s: (s, 0)),
        full((CONF_KERNEL, W_GROUP)), full((1, W_GROUP)), full((1, W_GROUP)), full((1, W_GROUP)),
        full((W_GROUP, W_GROUP)), full((1, W_GROUP)), full((1, W_GROUP)),
        full((SCONV_KERNEL, W_GROUP)),
    ]
    out_specs += [tok(3 * W_GROUP), tok_t(W_GROUP), tok(W_GROUP), tok_t(DIFF_HEADS * V_AUG_ROWS)]
    out_shape += [
        jax.ShapeDtypeStruct((bsz, seq, 3 * W_GROUP), BF16),
        jax.ShapeDtypeStruct((bsz, W_GROUP, seq), BF16),
        jax.ShapeDtypeStruct((bsz, seq, W_GROUP), BF16),
        jax.ShapeDtypeStruct((bsz, DIFF_HEADS * V_AUG_ROWS, seq), BF16),
    ]
    return pl.pallas_call(
        functools.partial(_inproj_mix_kernel, tm=tm, fused_ple=ple is not None),
        grid=(bsz, seq // tm),
        in_specs=in_specs,
        out_specs=out_specs,
        out_shape=out_shape,
        scratch_shapes=[
            pltpu.VMEM((tm + CONF_HIST, W_GROUP), F32),
            pltpu.VMEM((tm + POOL_HIST, W_GROUP), F32),
            pltpu.VMEM((tm + SCONV_HIST, W_GROUP), F32),
            pltpu.VMEM((SUBLANES - 1, tm + CONF_HIST - SUBLANES, W_GROUP), F32),
        ],
        compiler_params=_cparams(("arbitrary", "arbitrary")),
        name="inproj_mix",
    )(*args)


def _diff_attn_kernel(qt_ref, k_ref, vt_ref, lam_ref, g_ref, o_ref,
                      qm_ref, s0_ref, s1_ref, p0_ref, p1_ref, al0_ref, al1_ref, m_ref, l_ref, acc_ref,
                      *, t, nq, lam_init):
    qi = pl.program_id(1)
    nhc = 2 * DIFF_HEADS
    tq = nq * t

    qm_ref[...] = jnp.zeros(qm_ref.shape, BF16)
    for j in range(nhc):
        lo = j * DIFF_HEAD_DIM
        qm_ref[lo:lo + DIFF_HEAD_DIM, j * tq:(j + 1) * tq] = qt_ref[lo:lo + DIFF_HEAD_DIM, :]
    m_ref[...] = jnp.full(m_ref.shape, NEG_BIG, F32)
    l_ref[...] = jnp.zeros(l_ref.shape, F32)
    acc_ref[...] = jnp.zeros(acc_ref.shape, F32)
    p1_ref[...] = jnp.zeros(p1_ref.shape, BF16)
    al1_ref[...] = jnp.ones(al1_ref.shape, F32)

    def scores(kt, s_ref):
        k0 = pl.multiple_of(kt * t, t)
        s_ref[...] = jnp.dot(k_ref[pl.ds(k0, t), :], qm_ref[...], preferred_element_type=F32)

    def softmax(s_ref, p_ref, al_ref, diag):
        if diag is not None:
            above = (lax.broadcasted_iota(jnp.int32, (t, t), 0) > lax.broadcasted_iota(jnp.int32, (t, t), 1))
        for j in range(nhc):
            for r in range(nq):
                cols = slice(j * tq + r * t, j * tq + (r + 1) * t)
                qs = slice(r * t, (r + 1) * t)
                if diag is not None and r < diag:
                    p_ref[:, cols] = jnp.zeros((t, t), BF16)
                    al_ref[j, :, qs] = jnp.ones((1, t), F32)
                    continue
                sc = s_ref[:, cols]
                if diag is not None and r == diag:
                    sc = jnp.where(above, NEG_BIG, sc)
                m_old = m_ref[j, :, qs]
                m_new = jnp.maximum(m_old, jnp.max(sc, axis=0, keepdims=True))
                al_ref[j, :, qs] = jnp.exp2(m_old - m_new)
                m_ref[j, :, qs] = m_new
                p_ref[:, cols] = jnp.exp2(sc - m_new).astype(BF16)

    def weighted_values(p_ref, al_ref, kt):
        k0 = pl.multiple_of(jnp.maximum(kt, 0) * t, t)
        for hd in range(DIFF_HEADS):
            vth = vt_ref[hd * V_AUG_ROWS:(hd + 1) * V_AUG_ROWS, pl.ds(k0, t)]
            o = jnp.dot(vth, p_ref[:, 2 * hd * tq:(2 * hd + 2) * tq], preferred_element_type=F32)
            for c in range(2):
                j = 2 * hd + c
                alpha = al_ref[j]
                acc_ref[j] = alpha * acc_ref[j] + o[0:DIFF_V_DIM, c * tq:(c + 1) * tq]
                l_ref[j] = alpha * l_ref[j] + o[DIFF_V_DIM:DIFF_V_DIM + 1, c * tq:(c + 1) * tq]

    scores(0, s0_ref)

    def pair(i, carry):
        kt = 2 * i
        scores(kt + 1, s1_ref)
        softmax(s0_ref, p0_ref, al0_ref, None)
        weighted_values(p1_ref, al1_ref, kt - 1)
        scores(kt + 2, s0_ref)
        softmax(s1_ref, p1_ref, al1_ref, None)
        weighted_values(p0_ref, al0_ref, kt)
        return carry

    lax.fori_loop(0, qi, pair, 0)

    kd = 2 * qi
    scores(kd + 1, s1_ref)
    softmax(s0_ref, p0_ref, al0_ref, 0)
    weighted_values(p1_ref, al1_ref, kd - 1)
    softmax(s1_ref, p1_ref, al1_ref, 1)
    weighted_values(p0_ref, al0_ref, kd)
    weighted_values(p1_ref, al1_ref, kd + 1)

    lp = lam_ref[...]
    lam = (jnp.exp(jnp.sum(lp[0:1] * lp[1:2], axis=-1, keepdims=True))
           - jnp.exp(jnp.sum(lp[2:3] * lp[3:4], axis=-1, keepdims=True)) + lam_init)
    outs = []
    for hd in range(DIFF_HEADS):
        o1 = acc_ref[2 * hd] / l_ref[2 * hd]
        o2 = acc_ref[2 * hd + 1] / l_ref[2 * hd + 1]
        o = o1 - lam * o2
        ms = jnp.mean(o * o, axis=0, keepdims=True)
        outs.append(o * lax.rsqrt(ms + EPS) * g_ref[...] * (1.0 - lam_init))
    o_ref[...] = jnp.concatenate(outs, axis=0).T.astype(BF16)


def _diff_attn(qt, k, vt, lam_params, g_bcast, lam_init):
    bsz, seq, _ = k.shape
    t, nq = ATT_TILE, ATT_Q_BLOCKS
    tq = nq * t
    assert seq % tq == 0 and nq == 2
    nhc = 2 * DIFF_HEADS
    return pl.pallas_call(
        functools.partial(_diff_attn_kernel, t=t, nq=nq, lam_init=lam_init),
        grid=(bsz, seq // tq),
        in_specs=[
            pl.BlockSpec((None, W_GROUP, tq), lambda b, q: (b, 0, q)),
            pl.BlockSpec((None, seq, W_GROUP), lambda b, q: (b, 0, 0)),
            pl.BlockSpec((None, DIFF_HEADS * V_AUG_ROWS, seq), lambda b, q: (b, 0, 0)),
            pl.BlockSpec((4, DIFF_HEAD_DIM), lambda b, q: (0, 0)),
            pl.BlockSpec((DIFF_V_DIM, tq), lambda b, q: (0, 0)),
        ],
        out_specs=pl.BlockSpec((None, tq, W_GROUP), lambda b, q: (b, q, 0)),
        out_shape=jax.ShapeDtypeStruct((bsz, seq, W_GROUP), BF16),
        scratch_shapes=[
            pltpu.VMEM((W_GROUP, nhc * tq), BF16),
            pltpu.VMEM((t, nhc * tq), F32), pltpu.VMEM((t, nhc * tq), F32),
            pltpu.VMEM((t, nhc * tq), BF16), pltpu.VMEM((t, nhc * tq), BF16),
            pltpu.VMEM((nhc, 1, tq), F32), pltpu.VMEM((nhc, 1, tq), F32),
            pltpu.VMEM((nhc, 1, tq), F32),
            pltpu.VMEM((nhc, 1, tq), F32),
            pltpu.VMEM((nhc, DIFF_V_DIM, tq), F32),
        ],
        compiler_params=_cparams(("arbitrary", "arbitrary")),
        name="diff_attn",
    )(qt, k, vt, lam_params, g_bcast)


def _outproj_router_kernel(h_ref, yabd_ref, yc_ref, wabd_ref, wc_ref, g_ref, wr_ref, br_ref,
                           h1_ref, xs_ref, meta_ref, cnt_ref, rt_ref, ct_ref, oh_ref, carry_ref, *, tm):
    d = h_ref.shape[1]

    @pl.when(pl.program_id(0) == 0)
    def _():
        carry_ref[...] = jnp.zeros(carry_ref.shape, F32)

    h1 = (h_ref[...]
          + jnp.dot(yabd_ref[...], wabd_ref[...], preferred_element_type=F32)
          + jnp.dot(yc_ref[...], wc_ref[...], preferred_element_type=F32))
    h1_ref[...] = h1
    xn = _rms_rows(h1, g_ref[...])
    x_hi = xn.astype(BF16)
    x_lo = (xn - x_hi.astype(F32)).astype(BF16)
    xs_ref[:, 0:d] = xn
    r = (jnp.dot(x_hi, wr_ref[...], preferred_element_type=F32)
         + jnp.dot(x_lo, wr_ref[...], preferred_element_type=F32) + br_ref[...])
    rt_ref[...] = r.T

    def row(i):
        return rt_ref[i:i + 1, :] + rt_ref[ROUTER_LO + i:ROUTER_LO + i + 1, :]

    def first_hits(vals, target):
        hits, free = [], None
        for v in vals:
            f = jnp.where(v == target, 1.0, 0.0)
            hits.append(f if free is None else f * free)
            free = (1.0 - f) if free is None else free * (1.0 - f)
        return hits

    add = lambda a, b: a + b
    gl = [row(i) for i in range(N_GROUPS)]
    el = [row(8 + e) for e in range(N_EXPERTS)]
    gmax = functools.reduce(jnp.maximum, gl)
    gsel = first_hits(gl, gmax)
    pg_c = 1.0 / functools.reduce(add, [jnp.exp(v - gmax) for v in gl])
    elc = []
    for kk in range(EXPERTS_PER_GROUP):
        v = gsel[0] * el[kk]
        for g in range(1, N_GROUPS):
            v = v + gsel[g] * el[g * EXPERTS_PER_GROUP + kk]
        elc.append(v)
    emax = functools.reduce(jnp.maximum, elc)
    ex = [jnp.exp(v - emax) for v in elc]
    se = functools.reduce(add, ex)
    pe = [v / se for v in ex]
    p1 = functools.reduce(jnp.maximum, pe)
    t1 = first_hits(pe, p1)
    rest = [jnp.where(t > 0.0, -1.0, v) for t, v in zip(t1, pe)]
    p2 = functools.reduce(jnp.maximum, rest)
    t2 = first_hits(rest, p2)
    den = p1 + p2
    w1 = pg_c * (p1 / den)
    w2 = pg_c * (p2 / den)
    wk = [a * w1 + b * w2 for a, b in zip(t1, t2)]
    sel = [a + b for a, b in zip(t1, t2)]
    lower = first_hits(sel, 1.0)
    c_a = functools.reduce(add, [f * w for f, w in zip(lower, wk)])
    c_b = functools.reduce(add, [(s - f) * w for s, f, w in zip(sel, lower, wk)])
    pair = (sel[0] * sel[2] + 2.0 * sel[0] * sel[3] + 3.0 * sel[1] * sel[2]
            + 4.0 * sel[1] * sel[3] + 5.0 * sel[2] * sel[3])
    bucket = 6.0 * (gsel[1] + 2.0 * gsel[2] + 3.0 * gsel[3]) + pair

    ct_ref[...] = jnp.zeros(ct_ref.shape, F32)
    ct_ref[0:1, :] = c_a
    ct_ref[1:2, :] = c_b
    xs_ref[:, d:d + ROUTER_LANES] = ct_ref[...].T

    for b in range(N_BUCKET_ROWS):
        oh_ref[b:b + 1, :] = jnp.where(bucket == float(b), 1.0, 0.0)
    oh = oh_ref[...]
    upper = (lax.broadcasted_iota(jnp.int32, (tm, tm), 0) <= lax.broadcasted_iota(jnp.int32, (tm, tm), 1))
    cum = jnp.dot(oh.astype(BF16), jnp.where(upper, 1.0, 0.0).astype(BF16), preferred_element_type=F32)
    carry = carry_ref[...]
    rank = jnp.sum(oh * (cum - 1.0 + carry), axis=0, keepdims=True)
    carry_new = carry + jnp.sum(oh, axis=1, keepdims=True)
    carry_ref[...] = carry_new
    cnt_ref[...] = carry_new[:, 0:ROUTER_LANES]
    meta_ref[...] = jnp.zeros(meta_ref.shape, jnp.int32)
    meta_ref[0:1, :] = bucket.astype(jnp.int32)
    meta_ref[1:2, :] = rank.astype(jnp.int32)


def _outproj_router(h2d, yabd, yc, wabd, wc, gain, wr, br):
    tkn, d = h2d.shape
    tm = min(TOK_TILE, tkn)
    assert tkn % tm == 0
    row = lambda w: pl.BlockSpec((tm, w), lambda i: (i, 0))
    full = lambda shape: pl.BlockSpec(shape, lambda i: (0,) * len(shape))
    return pl.pallas_call(
        functools.partial(_outproj_router_kernel, tm=tm),
        grid=(tkn // tm,),
        in_specs=[row(d), row(3 * W_GROUP), row(W_GROUP),
                  full((3 * W_GROUP, d)), full((W_GROUP, d)), full((1, d)),
                  full((d, ROUTER_LANES)), full((1, ROUTER_LANES))],
        out_specs=[row(d), row(d + ROUTER_LANES),
                   pl.BlockSpec((None, 8, tm), lambda i: (i, 0, 0)),
                   full((N_BUCKET_ROWS, ROUTER_LANES))],
        out_shape=[jax.ShapeDtypeStruct((tkn, d), F32),
                   jax.ShapeDtypeStruct((tkn, d + ROUTER_LANES), F32),
                   jax.ShapeDtypeStruct((tkn // tm, 8, tm), jnp.int32),
                   jax.ShapeDtypeStruct((N_BUCKET_ROWS, ROUTER_LANES), F32)],
        scratch_shapes=[pltpu.VMEM((ROUTER_LANES, tm), F32), pltpu.VMEM((ROUTER_LANES, tm), F32),
                        pltpu.VMEM((N_BUCKET_ROWS, tm), F32), pltpu.VMEM((N_BUCKET_ROWS, tm), F32)],
        compiler_params=_cparams(("arbitrary",)),
        name="outproj_router",
    )(h2d, yabd, yc, wabd, wc, gain, wr, br)


def _moe_kernel(vt_ref, vea_ref, veb_ref, vlo_ref, vhi_ref, vfl_ref,
                tok_ref, tokn_ref, xs_hbm, wgua_ref, wgub_ref, wda_ref, wdb_ref,
                y_hbm, xbuf, ybuf, gsem, ssem, *, tm, n_tiles, n_visits):
    v = pl.program_id(0)
    tile = vt_ref[v]
    flags = vfl_ref[v]
    first = (flags & 1) != 0
    last = (flags & 2) != 0
    valid = (flags & 4) != 0
    slot = tile % 2
    d = y_hbm.shape[1]

    def row_gather(ids_ref, s, k, u):
        return pltpu.make_async_copy(xs_hbm.at[pl.ds(ids_ref[0, k * SUBLANES + u], 1), :],
                                     xbuf.at[s, k, pl.ds(u, 1), :], gsem.at[s])

    def row_scatter(ids_ref, s, k, u):
        return pltpu.make_async_copy(ybuf.at[s, k, pl.ds(u, 1), :],
                                     y_hbm.at[pl.ds(ids_ref[0, k * SUBLANES + u], 1), :], ssem.at[s])

    def start_all(mk, ids_ref, s):
        def body(k, c):
            for u in range(SUBLANES):
                mk(ids_ref, s, k, u).start(priority=u % 2)
            return c
        lax.fori_loop(0, tm // SUBLANES, body, 0)

    def wait_gather(s):
        pltpu.make_async_copy(xbuf.at[s], xbuf.at[s], gsem.at[s]).wait()

    def wait_scatter(s):
        pltpu.make_async_copy(ybuf.at[s], ybuf.at[s], ssem.at[s]).wait()

    @pl.when(v == 0)
    def _():
        start_all(row_gather, tok_ref, 0)

    @pl.when(jnp.logical_and(first, valid))
    def _():
        wait_gather(slot)

        @pl.when(tile + 1 < n_tiles)
        def _():
            start_all(row_gather, tokn_ref, 1 - slot)

        @pl.when(tile >= 2)
        def _():
            wait_scatter(slot)

    @pl.when(valid)
    def _():
        rows = tile * tm + lax.broadcasted_iota(jnp.int32, (tm, 1), 0)
        inb = jnp.where(jnp.logical_and(rows >= vlo_ref[v], rows < vhi_ref[v]), 1.0, 0.0)
        xt = xbuf[slot].reshape(tm, xs_hbm.shape[1])
        x = xt[:, 0:d].astype(BF16)
        c_a = xt[:, d:d + 1] * inb
        c_b = xt[:, d + 1:d + 2] * inb

        def expert(wgu_ref, wd_ref, c):
            gu = jnp.dot(x, wgu_ref[...], preferred_element_type=F32)
            gate = gu[:, :D_EXPERT]
            hdn = gate * _sigmoid(gate) * gu[:, D_EXPERT:] * c
            return jnp.dot(hdn.astype(BF16), wd_ref[...], preferred_element_type=F32)

        y = expert(wgua_ref, wda_ref, c_a) + expert(wgub_ref, wdb_ref, c_b)

        y = y.reshape(tm // SUBLANES, SUBLANES, d)

        @pl.when(first)
        def _():
            ybuf[slot] = y

        @pl.when(jnp.logical_not(first))
        def _():
            ybuf[slot] = ybuf[slot] + y

    @pl.when(jnp.logical_and(last, valid))
    def _():
        start_all(row_scatter, tok_ref, slot)

    @pl.when(v == n_visits - 1)
    def _():
        wait_scatter(0)
        wait_scatter(1)


def _moe(xs, tok_of_slot, sched, wgu, wd):
    tkn, dx = xs.shape
    d = dx - ROUTER_LANES
    tm = MOE_TILE
    assert tkn % tm == 0 and tkn // tm >= 2
    n_tiles = tkn // tm
    n_visits = sched[0].shape[0]
    tok3 = tok_of_slot.reshape(n_tiles, 1, tm)
    wspec = lambda shape, which: pl.BlockSpec(
        (None,) + shape, lambda v, vt, vea, veb, vlo, vhi, vfl: ((vea, veb)[which][v], 0, 0))
    grid_spec = pltpu.PrefetchScalarGridSpec(
        num_scalar_prefetch=6,
        grid=(n_visits,),
        in_specs=[
            pl.BlockSpec((None, 1, tm), lambda v, vt, *_: (vt[v], 0, 0), memory_space=pltpu.SMEM),
            pl.BlockSpec((None, 1, tm), lambda v, vt, *_: (jnp.minimum(vt[v] + 1, n_tiles - 1), 0, 0),
                         memory_space=pltpu.SMEM),
            pl.BlockSpec(memory_space=pl.ANY),
            wspec((d, 2 * D_EXPERT), 0), wspec((d, 2 * D_EXPERT), 1),
            wspec((D_EXPERT, d), 0), wspec((D_EXPERT, d), 1),
        ],
        out_specs=pl.BlockSpec(memory_space=pl.ANY),
        scratch_shapes=[pltpu.VMEM((2, tm // SUBLANES, SUBLANES, dx), F32),
                        pltpu.VMEM((2, tm // SUBLANES, SUBLANES, d), F32),
                        pltpu.SemaphoreType.DMA((2,)), pltpu.SemaphoreType.DMA((2,))],
    )
    return pl.pallas_call(
        functools.partial(_moe_kernel, tm=tm, n_tiles=n_tiles, n_visits=n_visits),
        grid_spec=grid_spec,
        out_shape=jax.ShapeDtypeStruct((tkn, d), F32),
        compiler_params=_cparams(("arbitrary",)),
        name="moe_experts",
    )(*sched, tok3, tok3, xs, wgu, wgu, wd, wd)


def _moe_schedule(bucket, rank, counts, tm):
    tkn = bucket.size
    n_tiles = tkn // tm
    n_visits = n_tiles + N_BUCKETS
    ends = jnp.cumsum(counts)
    offs = ends - counts
    slot = rank
    for bb in range(N_BUCKETS):
        slot = slot + jnp.where(bucket == bb, offs[bb], 0)
    _, tok_of_slot = lax.sort_key_val(slot.reshape(tkn), jnp.arange(tkn, dtype=jnp.int32))
    t_first = offs // tm
    t_last = jnp.maximum(ends - 1, 0) // tm
    nv = jnp.where(counts > 0, t_last - t_first + 1, 0)
    v_end = jnp.cumsum(nv)
    v_start = v_end - nv
    total = v_end[-1]
    v = jnp.arange(n_visits, dtype=jnp.int32)
    vc = jnp.minimum(v, total - 1)
    b = jnp.sum((v_end[None, :] <= vc[:, None]).astype(jnp.int32), axis=1)
    onehot = b[:, None] == jnp.arange(N_BUCKETS, dtype=jnp.int32)[None, :]
    pick = lambda table: jnp.sum(jnp.where(onehot, table[None, :], 0), axis=1)
    tile = (pick(t_first) + vc - pick(v_start)).astype(jnp.int32)
    valid = v < total
    prev_tile = jnp.concatenate([jnp.full((1,), -1, jnp.int32), tile[:-1]])
    next_tile = jnp.concatenate([tile[1:], jnp.full((1,), -1, jnp.int32)])
    first = tile != prev_tile
    last = jnp.logical_or(tile != next_tile, v == total - 1)
    flags = (first.astype(jnp.int32) + 2 * last.astype(jnp.int32) + 4) * valid.astype(jnp.int32)
    g, pr = b // 6, b % 6
    lo_e = (pr >= 3).astype(jnp.int32) + (pr >= 5).astype(jnp.int32)
    hi_e = pr + 1 - 2 * (pr >= 3).astype(jnp.int32) - (pr >= 5).astype(jnp.int32)
    ea = g * EXPERTS_PER_GROUP + lo_e
    eb = g * EXPERTS_PER_GROUP + hi_e
    sched = (tile, ea.astype(jnp.int32), eb.astype(jnp.int32),
             pick(offs).astype(jnp.int32), pick(ends).astype(jnp.int32), flags.astype(jnp.int32))
    return tok_of_slot, sched


def _ple_final_kernel(h_ref, y_ref, p_ref, g_ref, wg_ref, bg_ref, wp_ref, fg_ref, o_ref):
    out = _ple_update(h_ref[...], y_ref[...], p_ref[...], g_ref[...], wg_ref[...], bg_ref[...], wp_ref[...])
    o_ref[...] = _rms_rows(out, fg_ref[...])


def _ple_final(h2d, y2d, p2d, gain, wg, bg, wp, fgain):
    tkn, d = h2d.shape
    pd = p2d.shape[1]
    tm = min(TOK_TILE, tkn)
    assert tkn % tm == 0
    full = lambda shape: pl.BlockSpec(shape, lambda i: (0,) * len(shape))
    return pl.pallas_call(
        _ple_final_kernel,
        grid=(tkn // tm,),
        in_specs=[pl.BlockSpec((tm, d), lambda i: (i, 0)), pl.BlockSpec((tm, d), lambda i: (i, 0)),
                  pl.BlockSpec((tm, pd), lambda i: (i, 0)),
                  full((1, d)), full((d, d)), full((1, d)), full((pd, d)), full((1, d))],
        out_specs=pl.BlockSpec((tm, d), lambda i: (i, 0)),
        out_shape=jax.ShapeDtypeStruct((tkn, d), F32),
        compiler_params=_cparams(("arbitrary",)),
        name="ple_final",
    )(h2d, y2d, p2d, gain, wg, bg, wp, fgain)


def _rope_table(seq):
    pos = np.arange(seq, dtype=np.float64)
    inv = ROPE_THETA ** (-np.arange(0, DIFF_HEAD_DIM, 2, dtype=np.float64) / DIFF_HEAD_DIM)
    ang = pos[:, None] * inv[None, :]
    ang = np.concatenate([ang, ang], axis=-1)
    cos, sin = np.cos(ang), np.sin(ang)
    lo = np.arange(DIFF_HEAD_DIM) < DIFF_HEAD_DIM // 2
    sin_lo = np.where(lo, -sin, 0.0)
    sin_hi = np.where(lo, 0.0, sin)
    rep = 128 // DIFF_HEAD_DIM
    table = np.concatenate([np.tile(t, (1, rep)) for t in (cos, sin_lo, sin_hi)], axis=1)
    return jnp.asarray(table.astype(np.float32))


def _block_diag(w):
    g, c, dd = w.shape
    out = jnp.zeros((g * c, g * dd), w.dtype)
    for i in range(g):
        out = out.at[i * c:(i + 1) * c, i * dd:(i + 1) * dd].set(w[i])
    return out


def _router_weights(wg, bg, we, be):
    d = wg.shape[0]
    w = jnp.zeros((d, ROUTER_LANES), F32)
    w = w.at[:, 0:N_GROUPS].set(wg).at[:, 8:8 + N_EXPERTS].set(we)
    w_hi = w.astype(BF16)
    w_lo = (w - w_hi.astype(F32)).astype(BF16)
    wr = w_hi.at[:, ROUTER_LO:ROUTER_LO + 8 + N_EXPERTS].set(w_lo[:, 0:8 + N_EXPERTS])
    br = jnp.zeros((1, ROUTER_LANES), F32)
    br = br.at[0, 0:N_GROUPS].set(bg).at[0, 8:8 + N_EXPERTS].set(be)
    return wr, br


def kernel(x, p, mix_norm, w_in, conf_conv_w, conf_conv_b, conf_ln_g, conf_ln_b, pool_w, pool_b, pool_scale, diff_lam_q1, diff_lam_k1, diff_lam_q2, diff_lam_k2, diff_subln_g, sconv_w, w_out, ffn_norm, router_group_w, router_group_b, router_expert_w, router_expert_b, expert_w_gate, expert_w_up, expert_w_down, ple_norm, ple_gate_w, ple_gate_b, ple_proj, final_norm):
    bsz, seq, d = x.shape
    depth = w_in.shape[0]
    tkn = bsz * seq
    rope = _rope_table(seq)
    att_t = ATT_TILE * ATT_Q_BLOCKS
    row = lambda v: v.reshape(1, -1).astype(F32)

    h, pending = x, None
    for i in range(depth):
        lam_init = 0.8 - 0.6 * math.exp(-0.3 * i)
        mix_args = (row(mix_norm[i]), w_in[i].astype(BF16), rope,
                    conf_conv_w[i], row(conf_conv_b[i]), row(conf_ln_g[i]), row(conf_ln_b[i]),
                    _block_diag(pool_w[i]).astype(BF16), row(pool_b[i]), row(pool_scale[i]), sconv_w[i])
        if pending is None:
            yabd, qt, k, vt = _inproj_mix(h, *mix_args)
        else:
            h, yabd, qt, k, vt = _inproj_mix(h, *mix_args, ple=pending)
        lam_params = jnp.stack([diff_lam_q1[i], diff_lam_k1[i], diff_lam_q2[i], diff_lam_k2[i]]).astype(F32)
        g_bcast = jnp.broadcast_to(diff_subln_g[i].astype(F32)[:, None], (DIFF_V_DIM, att_t))
        yc = _diff_attn(qt, k, vt, lam_params, g_bcast, lam_init)

        wo = w_out[i].astype(BF16)
        wabd = jnp.concatenate([wo[0:2 * W_GROUP], wo[3 * W_GROUP:]], axis=0)
        wc = wo[2 * W_GROUP:3 * W_GROUP]
        wr, br = _router_weights(router_group_w[i], router_group_b[i], router_expert_w[i], router_expert_b[i])
        h1, xs, meta, cnt = _outproj_router(
            h.reshape(tkn, d), yabd.reshape(tkn, 3 * W_GROUP), yc.reshape(tkn, W_GROUP),
            wabd, wc, row(ffn_norm[i]), wr, br)

        tok_of_slot, sched = _moe_schedule(
            meta[:, 0, :], meta[:, 1, :],
            cnt[:N_BUCKETS, 0].astype(jnp.int32), MOE_TILE)
        wgu = jnp.concatenate([expert_w_gate[i], expert_w_up[i]], axis=-1).astype(BF16)
        y = _moe(xs, tok_of_slot, sched, wgu, expert_w_down[i].astype(BF16))

        ple_w = (row(ple_norm[i]), ple_gate_w[i].astype(BF16), row(ple_gate_b[i]), ple_proj[i].astype(BF16))
        if i == depth - 1:
            out = _ple_final(h1, y, p[i].reshape(tkn, -1), *ple_w, row(final_norm))
            return out.reshape(bsz, seq, d)
        h = h1.reshape(bsz, seq, d)
        pending = (y.reshape(bsz, seq, d), p[i], *ple_w)
```

```python
import functools
import math

import jax
import jax.numpy as jnp
import numpy as np
from jax import lax
from jax.experimental import pallas as pl
from jax.experimental.pallas import tpu as pltpu

F32 = jnp.float32
BF16 = jnp.bfloat16

EPS = 1e-6
ROPE_THETA = 10000.0
W_GROUP = 256
CONF_KERNEL = 31
CONF_HIST = 32
SUBLANES = 8
POOL_HIST = 16
SCONV_KERNEL = 3
SCONV_HIST = 8
DIFF_HEADS = 4
DIFF_HEAD_DIM = 32
DIFF_V_DIM = 64
V_AUG_ROWS = 80
N_GROUPS = 4
EXPERTS_PER_GROUP = 4
N_EXPERTS = 16
D_EXPERT = 256
ROUTER_LANES = 128
ROUTER_LO = 32
NEG_BIG = -1e30
LOG2E = 1.4426950408889634

VMEM_LIMIT = 48 * 1024 * 1024

SEQ_TILE = 512
ROW_CHUNK = 64
ATT_TILE = 256
ATT_Q_BLOCKS = 2
TOK_TILE = 512
FINAL_TILE = 1024
MOE_TILE = 256
N_BUCKETS = 24
N_BUCKET_ROWS = 32


def _cparams(sem):
    return pltpu.CompilerParams(dimension_semantics=sem, vmem_limit_bytes=VMEM_LIMIT)


def _layer_spec(arr, layer):
    return pl.BlockSpec((None,) + arr.shape[1:], lambda *_: (layer,) + (0,) * (arr.ndim - 1))


def _rms_rows(x, g):
    ms = jnp.mean(x * x, axis=-1, keepdims=True)
    return x * lax.rsqrt(ms + EPS) * g


def _sigmoid(x):
    return 1.0 / (1.0 + jnp.exp(-x))


def _ple_update(h, y, p, g, wg, bg, wp):
    h = h + y
    xn = _rms_rows(h, g).astype(BF16)
    gate = _sigmoid(jnp.dot(xn, wg, preferred_element_type=F32) + bg)
    return h + gate * jnp.dot(p.astype(BF16), wp, preferred_element_type=F32)


def _inproj_mix_kernel(*refs, tm, fused_ple):
    if fused_ple:
        (h_ref, y_ref, p_ref, pg_ref, pwg_ref, pbg_ref, pwp_ref), refs = refs[:7], refs[7:]
    else:
        h_ref, refs = refs[0], refs[1:]
    (g_ref, w_ref, rope_ref, cw_ref, cb_ref, lg_ref, lb_ref, pw_ref, pb_ref, ps_ref, sw_ref), refs = refs[:11], refs[11:]
    if fused_ple:
        hout_ref, refs = refs[0], refs[1:]
    yabd_ref, qt_ref, k_ref, vt_ref, gbuf, pbuf, zbuf, gshift = refs
    s = pl.program_id(1)

    @pl.when(s == 0)
    def _():
        gbuf[0:CONF_HIST, :] = jnp.zeros((CONF_HIST, W_GROUP), F32)
        pbuf[0:POOL_HIST, :] = jnp.zeros((POOL_HIST, W_GROUP), F32)
        zbuf[0:SCONV_HIST, :] = jnp.zeros((SCONV_HIST, W_GROUP), F32)

    if fused_ple:
        h = _ple_update(h_ref[...], y_ref[...], p_ref[...], pg_ref[...], pwg_ref[...], pbg_ref[...], pwp_ref[...])
        hout_ref[...] = h
    else:
        h = h_ref[...]
    n = _rms_rows(h, g_ref[...]).astype(BF16)

    def proj(lo, hi):
        return jnp.dot(n, w_ref[:, lo:hi], preferred_element_type=F32)

    a = proj(0, 2 * W_GROUP)
    gbuf[CONF_HIST:CONF_HIST + tm, :] = a[:, :W_GROUP] * _sigmoid(a[:, W_GROUP:])
    span = tm + CONF_HIST - SUBLANES
    for sh in range(1, SUBLANES):
        gshift[sh - 1, 0:span, :] = gbuf[sh:sh + span, :]
    base = CONF_HIST - (CONF_KERNEL - 1)
    for c in range(tm // ROW_CHUNK):
        r0 = c * ROW_CHUNK
        acc = jnp.broadcast_to(cb_ref[...], (ROW_CHUNK, W_GROUP))
        for j in range(CONF_KERNEL):
            sh = (base + j) % SUBLANES
            q0 = r0 + base + j - sh
            win = gbuf[q0:q0 + ROW_CHUNK, :] if sh == 0 else gshift[sh - 1, q0:q0 + ROW_CHUNK, :]
            acc = acc + cw_ref[j:j + 1, :] * win
        mu = jnp.mean(acc, axis=-1, keepdims=True)
        d = acc - mu
        var = jnp.mean(d * d, axis=-1, keepdims=True)
        y = d * lax.rsqrt(var + EPS) * lg_ref[...] + lb_ref[...]
        yabd_ref[r0:r0 + ROW_CHUNK, 0:W_GROUP] = (y * _sigmoid(y)).astype(BF16)
    gbuf[0:CONF_HIST, :] = gbuf[tm:tm + CONF_HIST, :]

    pbuf[POOL_HIST:POOL_HIST + tm, :] = proj(2 * W_GROUP, 3 * W_GROUP)
    lane = lax.broadcasted_iota(jnp.int32, (ROW_CHUNK, 128), 1)
    first = lane < 64
    for c in range(tm // ROW_CHUNK):
        r0 = c * ROW_CHUNK
        tpos = (s * tm + r0 + 1 + lax.broadcasted_iota(jnp.int32, (ROW_CHUNK, 128), 0)).astype(F32)
        halves = []
        for half, (w_small, w_big) in enumerate(((2, 4), (8, 16))):
            l0 = half * 128

            def ld(j):
                return pbuf[r0 + POOL_HIST - j:r0 + POOL_HIST - j + ROW_CHUNK, l0:l0 + 128]

            cur = ld(0)
            run = cur
            for j in range(1, w_small):
                run = run + ld(j)
            small = run
            for j in range(w_small, w_big):
                run = run + ld(j)
            cnt = jnp.where(first, jnp.minimum(tpos, float(w_small)), jnp.minimum(tpos, float(w_big)))
            halves.append(jnp.where(first, small, run) / cnt - cur)
        pc = jnp.concatenate(halves, axis=1).astype(BF16)
        yb = (jnp.dot(pc, pw_ref[...], preferred_element_type=F32) + pb_ref[...]) * ps_ref[...]
        yabd_ref[r0:r0 + ROW_CHUNK, W_GROUP:2 * W_GROUP] = yb.astype(BF16)
    pbuf[0:POOL_HIST, :] = pbuf[tm:tm + POOL_HIST, :]

    sc = proj(6 * W_GROUP, 9 * W_GROUP)
    zbuf[SCONV_HIST:SCONV_HIST + tm, :] = sc[:, W_GROUP:2 * W_GROUP] * sc[:, 2 * W_GROUP:]
    conv = None
    for j in range(SCONV_KERNEL):
        off = SCONV_HIST - (SCONV_KERNEL - 1) + j
        term = sw_ref[j:j + 1, :] * zbuf[off:off + tm, :]
        conv = term if conv is None else conv + term
    yabd_ref[:, 2 * W_GROUP:3 * W_GROUP] = (sc[:, :W_GROUP] * conv).astype(BF16)
    zbuf[0:SCONV_HIST, :] = zbuf[tm:tm + SCONV_HIST, :]

    qk = proj(3 * W_GROUP, 5 * W_GROUP)
    tab = rope_ref[...]
    cos = jnp.concatenate([tab[:, 0:128]] * 4, axis=1)
    sin_lo = jnp.concatenate([tab[:, 128:256]] * 4, axis=1)
    sin_hi = jnp.concatenate([tab[:, 256:384]] * 4, axis=1)
    half = DIFF_HEAD_DIM // 2
    width = 2 * W_GROUP
    qk = qk * cos + pltpu.roll(qk, width - half, axis=1) * sin_lo + pltpu.roll(qk, half, axis=1) * sin_hi
    q = qk[:, :W_GROUP] * (DIFF_HEAD_DIM ** -0.5 * LOG2E)
    qt_ref[...] = q.T.astype(BF16)
    k_ref[...] = qk[:, W_GROUP:].astype(BF16)
    vt = proj(5 * W_GROUP, 6 * W_GROUP).T.astype(BF16)
    for hd in range(DIFF_HEADS):
        r0 = hd * V_AUG_ROWS
        vt_ref[r0:r0 + DIFF_V_DIM, :] = vt[hd * DIFF_V_DIM:(hd + 1) * DIFF_V_DIM, :]
        vt_ref[r0 + DIFF_V_DIM:r0 + V_AUG_ROWS, :] = jnp.ones((V_AUG_ROWS - DIFF_V_DIM, tm), BF16)


def _inproj_mix(h, layer, prm, ple=None):
    bsz, seq, d = h.shape
    tm = min(SEQ_TILE, seq)
    assert seq % tm == 0 and tm % ROW_CHUNK == 0 and tm >= CONF_HIST
    tok = lambda w: pl.BlockSpec((None, tm, w), lambda b, s: (b, s, 0))
    tok_t = lambda rows: pl.BlockSpec((None, rows, tm), lambda b, s: (b, 0, s))
    args, in_specs = [h], [tok(d)]
    out_specs, out_shape = [], []
    if ple is not None:
        y, prev = ple
        pd = prm["p"].shape[-1]
        stacked = [prm[n] for n in ("ple_norm", "ple_wg", "ple_bg", "ple_wp")]
        args += [y, prm["p"]] + stacked
        in_specs += [tok(d), pl.BlockSpec((None, None, tm, pd), lambda b, s: (prev, b, s, 0))]
        in_specs += [_layer_spec(a, prev) for a in stacked]
        out_specs.append(tok(d))
        out_shape.append(jax.ShapeDtypeStruct((bsz, seq, d), F32))
    stacked = [prm[n] for n in ("mix_norm", "w_in")]
    args += stacked + [prm["rope"]]
    in_specs += [_layer_spec(a, layer) for a in stacked] + [pl.BlockSpec((tm, 384), lambda b, s: (s, 0))]
    stacked = [prm[n] for n in ("conf_w", "conf_b", "ln_g", "ln_b", "pool_bd", "pool_b", "pool_s", "sconv_w")]
    args += stacked
    in_specs += [_layer_spec(a, layer) for a in stacked]
    out_specs += [tok(3 * W_GROUP), tok_t(W_GROUP), tok(W_GROUP), tok_t(DIFF_HEADS * V_AUG_ROWS)]
    out_shape += [
        jax.ShapeDtypeStruct((bsz, seq, 3 * W_GROUP), BF16),
        jax.ShapeDtypeStruct((bsz, W_GROUP, seq), BF16),
        jax.ShapeDtypeStruct((bsz, seq, W_GROUP), BF16),
        jax.ShapeDtypeStruct((bsz, DIFF_HEADS * V_AUG_ROWS, seq), BF16),
    ]
    return pl.pallas_call(
        functools.partial(_inproj_mix_kernel, tm=tm, fused_ple=ple is not None),
        grid=(bsz, seq // tm),
        in_specs=in_specs,
        out_specs=out_specs,
        out_shape=out_shape,
        scratch_shapes=[
            pltpu.VMEM((tm + CONF_HIST, W_GROUP), F32),
            pltpu.VMEM((tm + POOL_HIST, W_GROUP), F32),
            pltpu.VMEM((tm + SCONV_HIST, W_GROUP), F32),
            pltpu.VMEM((SUBLANES - 1, tm + CONF_HIST - SUBLANES, W_GROUP), F32),
        ],
        compiler_params=_cparams(("arbitrary", "arbitrary")),
        name="inproj_mix",
    )(*args)


def _diff_attn_kernel(qt_ref, k_ref, vt_ref, lam_ref, g_ref, o_ref,
                      qm_ref, s0_ref, s1_ref, p0_ref, p1_ref, al0_ref, al1_ref, m_ref, l_ref, acc_ref,
                      *, t, nq, lam_init):
    qi = pl.program_id(1)
    nhc = 2 * DIFF_HEADS
    tq = nq * t

    qm_ref[...] = jnp.zeros(qm_ref.shape, BF16)
    for j in range(nhc):
        lo = j * DIFF_HEAD_DIM
        qm_ref[lo:lo + DIFF_HEAD_DIM, j * tq:(j + 1) * tq] = qt_ref[lo:lo + DIFF_HEAD_DIM, :]
    m_ref[...] = jnp.full(m_ref.shape, NEG_BIG, F32)
    l_ref[...] = jnp.zeros(l_ref.shape, F32)
    acc_ref[...] = jnp.zeros(acc_ref.shape, F32)
    p1_ref[...] = jnp.zeros(p1_ref.shape, BF16)
    al1_ref[...] = jnp.ones(al1_ref.shape, F32)

    def scores(kt, s_ref):
        k0 = pl.multiple_of(kt * t, t)
        s_ref[...] = jnp.dot(k_ref[pl.ds(k0, t), :], qm_ref[...], preferred_element_type=F32)

    def softmax(s_ref, p_ref, al_ref, diag):
        if diag is not None:
            above = (lax.broadcasted_iota(jnp.int32, (t, t), 0) > lax.broadcasted_iota(jnp.int32, (t, t), 1))
        for j in range(nhc):
            for r in range(nq):
                cols = slice(j * tq + r * t, j * tq + (r + 1) * t)
                qs = slice(r * t, (r + 1) * t)
                if diag is not None and r < diag:
                    p_ref[:, cols] = jnp.zeros((t, t), BF16)
                    al_ref[j, :, qs] = jnp.ones((1, t), F32)
                    continue
                sc = s_ref[:, cols]
                if diag is not None and r == diag:
                    sc = jnp.where(above, NEG_BIG, sc)
                m_old = m_ref[j, :, qs]
                m_new = jnp.maximum(m_old, jnp.max(sc, axis=0, keepdims=True))
                al_ref[j, :, qs] = jnp.exp2(m_old - m_new)
                m_ref[j, :, qs] = m_new
                p_ref[:, cols] = jnp.exp2(sc - m_new).astype(BF16)

    def weighted_values(p_ref, al_ref, kt):
        k0 = pl.multiple_of(jnp.maximum(kt, 0) * t, t)
        for hd in range(DIFF_HEADS):
            vth = vt_ref[hd * V_AUG_ROWS:(hd + 1) * V_AUG_ROWS, pl.ds(k0, t)]
            o = jnp.dot(vth, p_ref[:, 2 * hd * tq:(2 * hd + 2) * tq], preferred_element_type=F32)
            for c in range(2):
                j = 2 * hd + c
                alpha = al_ref[j]
                acc_ref[j] = alpha * acc_ref[j] + o[0:DIFF_V_DIM, c * tq:(c + 1) * tq]
                l_ref[j] = alpha * l_ref[j] + o[DIFF_V_DIM:DIFF_V_DIM + 1, c * tq:(c + 1) * tq]

    scores(0, s0_ref)

    def pair(i, carry):
        kt = 2 * i
        scores(kt + 1, s1_ref)
        softmax(s0_ref, p0_ref, al0_ref, None)
        weighted_values(p1_ref, al1_ref, kt - 1)
        scores(kt + 2, s0_ref)
        softmax(s1_ref, p1_ref, al1_ref, None)
        weighted_values(p0_ref, al0_ref, kt)
        return carry

    lax.fori_loop(0, qi, pair, 0)

    kd = 2 * qi
    scores(kd + 1, s1_ref)
    softmax(s0_ref, p0_ref, al0_ref, 0)
    weighted_values(p1_ref, al1_ref, kd - 1)
    softmax(s1_ref, p1_ref, al1_ref, 1)
    weighted_values(p0_ref, al0_ref, kd)
    weighted_values(p1_ref, al1_ref, kd + 1)

    lp = lam_ref[...]
    lam = (jnp.exp(jnp.sum(lp[0:1] * lp[1:2], axis=-1, keepdims=True))
           - jnp.exp(jnp.sum(lp[2:3] * lp[3:4], axis=-1, keepdims=True)) + lam_init)
    outs = []
    for hd in range(DIFF_HEADS):
        o1 = acc_ref[2 * hd] / l_ref[2 * hd]
        o2 = acc_ref[2 * hd + 1] / l_ref[2 * hd + 1]
        o = o1 - lam * o2
        ms = jnp.mean(o * o, axis=0, keepdims=True)
        outs.append(o * lax.rsqrt(ms + EPS) * g_ref[...] * (1.0 - lam_init))
    o_ref[...] = jnp.concatenate(outs, axis=0).T.astype(BF16)


def _diff_attn(qt, k, vt, layer, prm, lam_init):
    bsz, seq, _ = k.shape
    t, nq = ATT_TILE, ATT_Q_BLOCKS
    tq = nq * t
    assert seq % tq == 0 and nq == 2
    nhc = 2 * DIFF_HEADS
    return pl.pallas_call(
        functools.partial(_diff_attn_kernel, t=t, nq=nq, lam_init=lam_init),
        grid=(bsz, seq // tq),
        in_specs=[
            pl.BlockSpec((None, W_GROUP, tq), lambda b, q: (b, 0, q)),
            pl.BlockSpec((None, seq, W_GROUP), lambda b, q: (b, 0, 0)),
            pl.BlockSpec((None, DIFF_HEADS * V_AUG_ROWS, seq), lambda b, q: (b, 0, 0)),
            _layer_spec(prm["lam"], layer),
            _layer_spec(prm["subln_g"], layer),
        ],
        out_specs=pl.BlockSpec((None, tq, W_GROUP), lambda b, q: (b, q, 0)),
        out_shape=jax.ShapeDtypeStruct((bsz, seq, W_GROUP), BF16),
        scratch_shapes=[
            pltpu.VMEM((W_GROUP, nhc * tq), BF16),
            pltpu.VMEM((t, nhc * tq), F32), pltpu.VMEM((t, nhc * tq), F32),
            pltpu.VMEM((t, nhc * tq), BF16), pltpu.VMEM((t, nhc * tq), BF16),
            pltpu.VMEM((nhc, 1, tq), F32), pltpu.VMEM((nhc, 1, tq), F32),
            pltpu.VMEM((nhc, 1, tq), F32),
            pltpu.VMEM((nhc, 1, tq), F32),
            pltpu.VMEM((nhc, DIFF_V_DIM, tq), F32),
        ],
        compiler_params=_cparams(("arbitrary", "arbitrary")),
        name="diff_attn",
    )(qt, k, vt, prm["lam"], prm["subln_g"])


def _outproj_router_kernel(h_ref, yabd_ref, yc_ref, wo_ref, g_ref, wr_ref, br_ref,
                           h1_ref, xs_ref, meta_ref, cnt_ref, rt_ref, ct_ref, oh_ref, carry_ref, *, tm):
    d = h_ref.shape[1]

    @pl.when(pl.program_id(0) == 0)
    def _():
        carry_ref[...] = jnp.zeros(carry_ref.shape, F32)

    wg2 = 2 * W_GROUP
    h1 = (h_ref[...]
          + jnp.dot(yabd_ref[:, 0:wg2], wo_ref[0:wg2, :], preferred_element_type=F32)
          + jnp.dot(yc_ref[...], wo_ref[wg2:wg2 + W_GROUP, :], preferred_element_type=F32)
          + jnp.dot(yabd_ref[:, wg2:], wo_ref[wg2 + W_GROUP:, :], preferred_element_type=F32))
    h1_ref[...] = h1
    xn = _rms_rows(h1, g_ref[...])
    x_hi = xn.astype(BF16)
    x_lo = (xn - x_hi.astype(F32)).astype(BF16)
    xs_ref[:, 0:d] = xn
    r = (jnp.dot(x_hi, wr_ref[...], preferred_element_type=F32)
         + jnp.dot(x_lo, wr_ref[...], preferred_element_type=F32) + br_ref[...])
    rt_ref[...] = r.T

    def row(i):
        return rt_ref[i:i + 1, :] + rt_ref[ROUTER_LO + i:ROUTER_LO + i + 1, :]

    def first_hits(vals, target):
        hits, free = [], None
        for v in vals:
            f = jnp.where(v == target, 1.0, 0.0)
            hits.append(f if free is None else f * free)
            free = (1.0 - f) if free is None else free * (1.0 - f)
        return hits

    add = lambda a, b: a + b
    gl = [row(i) for i in range(N_GROUPS)]
    el = [row(8 + e) for e in range(N_EXPERTS)]
    gmax = functools.reduce(jnp.maximum, gl)
    gsel = first_hits(gl, gmax)
    pg_c = 1.0 / functools.reduce(add, [jnp.exp(v - gmax) for v in gl])
    elc = []
    for kk in range(EXPERTS_PER_GROUP):
        v = gsel[0] * el[kk]
        for g in range(1, N_GROUPS):
            v = v + gsel[g] * el[g * EXPERTS_PER_GROUP + kk]
        elc.append(v)
    emax = functools.reduce(jnp.maximum, elc)
    ex = [jnp.exp(v - emax) for v in elc]
    se = functools.reduce(add, ex)
    pe = [v / se for v in ex]
    p1 = functools.reduce(jnp.maximum, pe)
    t1 = first_hits(pe, p1)
    rest = [jnp.where(t > 0.0, -1.0, v) for t, v in zip(t1, pe)]
    p2 = functools.reduce(jnp.maximum, rest)
    t2 = first_hits(rest, p2)
    den = p1 + p2
    w1 = pg_c * (p1 / den)
    w2 = pg_c * (p2 / den)
    wk = [a * w1 + b * w2 for a, b in zip(t1, t2)]
    sel = [a + b for a, b in zip(t1, t2)]
    lower = first_hits(sel, 1.0)
    c_a = functools.reduce(add, [f * w for f, w in zip(lower, wk)])
    c_b = functools.reduce(add, [(s - f) * w for s, f, w in zip(sel, lower, wk)])
    pair = (sel[0] * sel[2] + 2.0 * sel[0] * sel[3] + 3.0 * sel[1] * sel[2]
            + 4.0 * sel[1] * sel[3] + 5.0 * sel[2] * sel[3])
    bucket = 6.0 * (gsel[1] + 2.0 * gsel[2] + 3.0 * gsel[3]) + pair

    ct_ref[...] = jnp.zeros(ct_ref.shape, F32)
    ct_ref[0:1, :] = c_a
    ct_ref[1:2, :] = c_b
    xs_ref[:, d:d + ROUTER_LANES] = ct_ref[...].T

    for b in range(N_BUCKET_ROWS):
        oh_ref[b:b + 1, :] = jnp.where(bucket == float(b), 1.0, 0.0)
    oh = oh_ref[...]
    upper = (lax.broadcasted_iota(jnp.int32, (tm, tm), 0) <= lax.broadcasted_iota(jnp.int32, (tm, tm), 1))
    cum = jnp.dot(oh.astype(BF16), jnp.where(upper, 1.0, 0.0).astype(BF16), preferred_element_type=F32)
    carry = carry_ref[...]
    rank = jnp.sum(oh * (cum - 1.0 + carry), axis=0, keepdims=True)
    carry_new = carry + jnp.sum(oh, axis=1, keepdims=True)
    carry_ref[...] = carry_new
    cnt_ref[...] = carry_new[:, 0:ROUTER_LANES]
    meta_ref[...] = jnp.zeros(meta_ref.shape, jnp.int32)
    meta_ref[0:1, :] = bucket.astype(jnp.int32)
    meta_ref[1:2, :] = rank.astype(jnp.int32)


def _outproj_router(h2d, yabd, yc, layer, prm):
    tkn, d = h2d.shape
    tm = min(TOK_TILE, tkn)
    assert tkn % tm == 0
    row = lambda w: pl.BlockSpec((tm, w), lambda i: (i, 0))
    full = lambda shape: pl.BlockSpec(shape, lambda i: (0,) * len(shape))
    stacked = [prm[n] for n in ("w_out", "ffn_norm", "router_w", "router_b")]
    return pl.pallas_call(
        functools.partial(_outproj_router_kernel, tm=tm),
        grid=(tkn // tm,),
        in_specs=[row(d), row(3 * W_GROUP), row(W_GROUP)] + [_layer_spec(a, layer) for a in stacked],
        out_specs=[row(d), row(d + ROUTER_LANES),
                   pl.BlockSpec((None, 8, tm), lambda i: (i, 0, 0)),
                   full((N_BUCKET_ROWS, ROUTER_LANES))],
        out_shape=[jax.ShapeDtypeStruct((tkn, d), F32),
                   jax.ShapeDtypeStruct((tkn, d + ROUTER_LANES), F32),
                   jax.ShapeDtypeStruct((tkn // tm, 8, tm), jnp.int32),
                   jax.ShapeDtypeStruct((N_BUCKET_ROWS, ROUTER_LANES), F32)],
        scratch_shapes=[pltpu.VMEM((ROUTER_LANES, tm), F32), pltpu.VMEM((ROUTER_LANES, tm), F32),
                        pltpu.VMEM((N_BUCKET_ROWS, tm), F32), pltpu.VMEM((N_BUCKET_ROWS, tm), F32)],
        compiler_params=_cparams(("arbitrary",)),
        name="outproj_router",
    )(h2d, yabd, yc, *stacked)


def _moe_kernel(vt_ref, vea_ref, veb_ref, vlo_ref, vhi_ref, vfl_ref,
                tok_ref, tokn_ref, xs_hbm, wga_ref, wgb_ref, wua_ref, wub_ref, wda_ref, wdb_ref,
                y_hbm, xbuf, ybuf, gsem, ssem, *, tm, n_tiles, n_visits):
    v = pl.program_id(0)
    tile = vt_ref[v]
    flags = vfl_ref[v]
    first = (flags & 1) != 0
    last = (flags & 2) != 0
    valid = (flags & 4) != 0
    slot = tile % 2
    d = y_hbm.shape[1]

    def row_gather(ids_ref, s, k, u):
        return pltpu.make_async_copy(xs_hbm.at[pl.ds(ids_ref[0, k * SUBLANES + u], 1), :],
                                     xbuf.at[s, k, pl.ds(u, 1), :], gsem.at[s])

    def row_scatter(ids_ref, s, k, u):
        return pltpu.make_async_copy(ybuf.at[s, k, pl.ds(u, 1), :],
                                     y_hbm.at[pl.ds(ids_ref[0, k * SUBLANES + u], 1), :], ssem.at[s])

    def start_all(mk, ids_ref, s):
        def body(k, c):
            for u in range(SUBLANES):
                mk(ids_ref, s, k, u).start(priority=u % 2)
            return c
        lax.fori_loop(0, tm // SUBLANES, body, 0)

    def wait_gather(s):
        pltpu.make_async_copy(xbuf.at[s], xbuf.at[s], gsem.at[s]).wait()

    def wait_scatter(s):
        pltpu.make_async_copy(ybuf.at[s], ybuf.at[s], ssem.at[s]).wait()

    @pl.when(v == 0)
    def _():
        start_all(row_gather, tok_ref, 0)

    @pl.when(jnp.logical_and(first, valid))
    def _():
        wait_gather(slot)

        @pl.when(tile + 1 < n_tiles)
        def _():
            start_all(row_gather, tokn_ref, 1 - slot)

        @pl.when(tile >= 2)
        def _():
            wait_scatter(slot)

    @pl.when(valid)
    def _():
        rows = tile * tm + lax.broadcasted_iota(jnp.int32, (tm, 1), 0)
        inb = jnp.where(jnp.logical_and(rows >= vlo_ref[v], rows < vhi_ref[v]), 1.0, 0.0)
        xt = xbuf[slot].reshape(tm, xs_hbm.shape[1])
        x = xt[:, 0:d].astype(BF16)
        c_a = xt[:, d:d + 1] * inb
        c_b = xt[:, d + 1:d + 2] * inb

        def expert(wg_ref, wu_ref, wd_ref, c):
            gate = jnp.dot(x, wg_ref[...], preferred_element_type=F32)
            up = jnp.dot(x, wu_ref[...], preferred_element_type=F32)
            hdn = gate * _sigmoid(gate) * up * c
            return jnp.dot(hdn.astype(BF16), wd_ref[...], preferred_element_type=F32)

        y = expert(wga_ref, wua_ref, wda_ref, c_a) + expert(wgb_ref, wub_ref, wdb_ref, c_b)

        y = y.reshape(tm // SUBLANES, SUBLANES, d)

        @pl.when(first)
        def _():
            ybuf[slot] = y

        @pl.when(jnp.logical_not(first))
        def _():
            ybuf[slot] = ybuf[slot] + y

    @pl.when(jnp.logical_and(last, valid))
    def _():
        start_all(row_scatter, tok_ref, slot)

    @pl.when(v == n_visits - 1)
    def _():
        wait_scatter(0)
        wait_scatter(1)


def _moe(xs, tok_of_slot, sched, layer, prm):
    tkn, dx = xs.shape
    d = dx - ROUTER_LANES
    tm = MOE_TILE
    assert tkn % tm == 0 and tkn // tm >= 2
    n_tiles = tkn // tm
    n_visits = sched[0].shape[0]
    tok3 = tok_of_slot.reshape(n_tiles, 1, tm)
    wspec = lambda shape, which: pl.BlockSpec(
        (None, None) + shape, lambda v, vt, vea, veb, vlo, vhi, vfl: (layer, (vea, veb)[which][v], 0, 0))
    grid_spec = pltpu.PrefetchScalarGridSpec(
        num_scalar_prefetch=6,
        grid=(n_visits,),
        in_specs=[
            pl.BlockSpec((None, 1, tm), lambda v, vt, *_: (vt[v], 0, 0), memory_space=pltpu.SMEM),
            pl.BlockSpec((None, 1, tm), lambda v, vt, *_: (jnp.minimum(vt[v] + 1, n_tiles - 1), 0, 0),
                         memory_space=pltpu.SMEM),
            pl.BlockSpec(memory_space=pl.ANY),
            wspec((d, D_EXPERT), 0), wspec((d, D_EXPERT), 1),
            wspec((d, D_EXPERT), 0), wspec((d, D_EXPERT), 1),
            wspec((D_EXPERT, d), 0), wspec((D_EXPERT, d), 1),
        ],
        out_specs=pl.BlockSpec(memory_space=pl.ANY),
        scratch_shapes=[pltpu.VMEM((2, tm // SUBLANES, SUBLANES, dx), F32),
                        pltpu.VMEM((2, tm // SUBLANES, SUBLANES, d), F32),
                        pltpu.SemaphoreType.DMA((2,)), pltpu.SemaphoreType.DMA((2,))],
    )
    return pl.pallas_call(
        functools.partial(_moe_kernel, tm=tm, n_tiles=n_tiles, n_visits=n_visits),
        grid_spec=grid_spec,
        out_shape=jax.ShapeDtypeStruct((tkn, d), F32),
        compiler_params=_cparams(("arbitrary",)),
        name="moe_experts",
    )(*sched, tok3, tok3, xs, prm["exp_wg"], prm["exp_wg"], prm["exp_wu"], prm["exp_wu"], prm["exp_wd"], prm["exp_wd"])


def _moe_schedule(bucket, rank, counts, tm):
    tkn = bucket.size
    n_tiles = tkn // tm
    n_visits = n_tiles + N_BUCKETS
    ends = jnp.cumsum(counts)
    offs = ends - counts
    slot = rank
    for bb in range(N_BUCKETS):
        slot = slot + jnp.where(bucket == bb, offs[bb], 0)
    _, tok_of_slot = lax.sort_key_val(slot.reshape(tkn), jnp.arange(tkn, dtype=jnp.int32))
    t_first = offs // tm
    t_last = jnp.maximum(ends - 1, 0) // tm
    nv = jnp.where(counts > 0, t_last - t_first + 1, 0)
    v_end = jnp.cumsum(nv)
    v_start = v_end - nv
    total = v_end[-1]
    v = jnp.arange(n_visits, dtype=jnp.int32)
    vc = jnp.minimum(v, total - 1)
    b = jnp.sum((v_end[None, :] <= vc[:, None]).astype(jnp.int32), axis=1)
    onehot = b[:, None] == jnp.arange(N_BUCKETS, dtype=jnp.int32)[None, :]
    pick = lambda table: jnp.sum(jnp.where(onehot, table[None, :], 0), axis=1)
    tile = (pick(t_first) + vc - pick(v_start)).astype(jnp.int32)
    valid = v < total
    prev_tile = jnp.concatenate([jnp.full((1,), -1, jnp.int32), tile[:-1]])
    next_tile = jnp.concatenate([tile[1:], jnp.full((1,), -1, jnp.int32)])
    first = tile != prev_tile
    last = jnp.logical_or(tile != next_tile, v == total - 1)
    flags = (first.astype(jnp.int32) + 2 * last.astype(jnp.int32) + 4) * valid.astype(jnp.int32)
    g, pr = b // 6, b % 6
    lo_e = (pr >= 3).astype(jnp.int32) + (pr >= 5).astype(jnp.int32)
    hi_e = pr + 1 - 2 * (pr >= 3).astype(jnp.int32) - (pr >= 5).astype(jnp.int32)
    ea = g * EXPERTS_PER_GROUP + lo_e
    eb = g * EXPERTS_PER_GROUP + hi_e
    sched = (tile, ea.astype(jnp.int32), eb.astype(jnp.int32),
             pick(offs).astype(jnp.int32), pick(ends).astype(jnp.int32), flags.astype(jnp.int32))
    return tok_of_slot, sched


def _ple_final_kernel(h_ref, y_ref, p_ref, g_ref, wg_ref, bg_ref, wp_ref, fg_ref, o_ref):
    out = _ple_update(h_ref[...], y_ref[...], p_ref[...], g_ref[...], wg_ref[...], bg_ref[...], wp_ref[...])
    o_ref[...] = _rms_rows(out, fg_ref[...])


def _ple_final(h2d, y2d, layer, prm):
    tkn, d = h2d.shape
    p4 = prm["p"]
    _, bsz, seq, pd = p4.shape
    tm = min(FINAL_TILE, seq)
    assert seq % tm == 0 and tkn == bsz * seq
    per_row = seq // tm
    stacked = [prm[n] for n in ("ple_norm", "ple_wg", "ple_bg", "ple_wp")]
    return pl.pallas_call(
        _ple_final_kernel,
        grid=(tkn // tm,),
        in_specs=[pl.BlockSpec((tm, d), lambda i: (i, 0)), pl.BlockSpec((tm, d), lambda i: (i, 0)),
                  pl.BlockSpec((None, None, tm, pd), lambda i: (layer, i // per_row, i % per_row, 0))]
                 + [_layer_spec(a, layer) for a in stacked]
                 + [pl.BlockSpec((1, d), lambda i: (0, 0))],
        out_specs=pl.BlockSpec((tm, d), lambda i: (i, 0)),
        out_shape=jax.ShapeDtypeStruct((tkn, d), F32),
        compiler_params=_cparams(("arbitrary",)),
        name="ple_final",
    )(h2d, y2d, p4, *stacked, prm["final_norm"])


def _rope_table(seq):
    pos = np.arange(seq, dtype=np.float64)
    inv = ROPE_THETA ** (-np.arange(0, DIFF_HEAD_DIM, 2, dtype=np.float64) / DIFF_HEAD_DIM)
    ang = pos[:, None] * inv[None, :]
    ang = np.concatenate([ang, ang], axis=-1)
    cos, sin = np.cos(ang), np.sin(ang)
    lo = np.arange(DIFF_HEAD_DIM) < DIFF_HEAD_DIM // 2
    sin_lo = np.where(lo, -sin, 0.0)
    sin_hi = np.where(lo, 0.0, sin)
    rep = 128 // DIFF_HEAD_DIM
    table = np.concatenate([np.tile(t, (1, rep)) for t in (cos, sin_lo, sin_hi)], axis=1)
    return jnp.asarray(table.astype(np.float32))


def _block_diag(w):
    nl, g, c, dd = w.shape
    out = jnp.zeros((nl, g * c, g * dd), w.dtype)
    for i in range(g):
        out = out.at[:, i * c:(i + 1) * c, i * dd:(i + 1) * dd].set(w[:, i])
    return out


def _router_weights(wg, bg, we, be):
    nl, d, _ = wg.shape
    w = jnp.zeros((nl, d, ROUTER_LANES), F32)
    w = w.at[:, :, 0:N_GROUPS].set(wg).at[:, :, 8:8 + N_EXPERTS].set(we)
    w_hi = w.astype(BF16)
    w_lo = (w - w_hi.astype(F32)).astype(BF16)
    wr = w_hi.at[:, :, ROUTER_LO:ROUTER_LO + 8 + N_EXPERTS].set(w_lo[:, :, 0:8 + N_EXPERTS])
    br = jnp.zeros((nl, 1, ROUTER_LANES), F32)
    br = br.at[:, 0, 0:N_GROUPS].set(bg).at[:, 0, 8:8 + N_EXPERTS].set(be)
    return wr, br


def kernel(x, p, mix_norm, w_in, conf_conv_w, conf_conv_b, conf_ln_g, conf_ln_b, pool_w, pool_b, pool_scale, diff_lam_q1, diff_lam_k1, diff_lam_q2, diff_lam_k2, diff_subln_g, sconv_w, w_out, ffn_norm, router_group_w, router_group_b, router_expert_w, router_expert_b, expert_w_gate, expert_w_up, expert_w_down, ple_norm, ple_gate_w, ple_gate_b, ple_proj, final_norm):
    bsz, seq, d = x.shape
    depth = w_in.shape[0]
    tkn = bsz * seq
    att_t = ATT_TILE * ATT_Q_BLOCKS
    rows = lambda v: v.reshape(depth, 1, -1).astype(F32)
    router_w, router_b = _router_weights(router_group_w, router_group_b, router_expert_w, router_expert_b)
    prm = dict(
        p=p, rope=_rope_table(seq),
        mix_norm=rows(mix_norm), w_in=w_in.astype(BF16),
        conf_w=conf_conv_w, conf_b=rows(conf_conv_b), ln_g=rows(conf_ln_g), ln_b=rows(conf_ln_b),
        pool_bd=_block_diag(pool_w).astype(BF16), pool_b=rows(pool_b), pool_s=rows(pool_scale), sconv_w=sconv_w,
        lam=jnp.stack([diff_lam_q1, diff_lam_k1, diff_lam_q2, diff_lam_k2], axis=1).astype(F32),
        subln_g=jnp.broadcast_to(diff_subln_g.astype(F32)[:, :, None], (depth, DIFF_V_DIM, att_t)),
        w_out=w_out.astype(BF16), ffn_norm=rows(ffn_norm), router_w=router_w, router_b=router_b,
        exp_wg=expert_w_gate.astype(BF16), exp_wu=expert_w_up.astype(BF16), exp_wd=expert_w_down.astype(BF16),
        ple_norm=rows(ple_norm), ple_wg=ple_gate_w.astype(BF16), ple_bg=rows(ple_gate_b),
        ple_wp=ple_proj.astype(BF16), final_norm=final_norm.reshape(1, -1).astype(F32),
    )

    h, pending = x, None
    for i in range(depth):
        lam_init = 0.8 - 0.6 * math.exp(-0.3 * i)
        if pending is None:
            yabd, qt, k, vt = _inproj_mix(h, i, prm)
        else:
            h, yabd, qt, k, vt = _inproj_mix(h, i, prm, ple=pending)
        yc = _diff_attn(qt, k, vt, i, prm, lam_init)
        h1, xs, meta, cnt = _outproj_router(
            h.reshape(tkn, d), yabd.reshape(tkn, 3 * W_GROUP), yc.reshape(tkn, W_GROUP), i, prm)
        tok_of_slot, sched = _moe_schedule(
            meta[:, 0, :], meta[:, 1, :], cnt[:N_BUCKETS, 0].astype(jnp.int32), MOE_TILE)
        y = _moe(xs, tok_of_slot, sched, i, prm)
        if i == depth - 1:
            return _ple_final(h1, y, i, prm).reshape(bsz, seq, d)
        h = h1.reshape(bsz, seq, d)
        pending = (y.reshape(bsz, seq, d), i)
```

```python
import functools
import math

import jax
import jax.numpy as jnp
import numpy as np
from jax import lax
from jax.experimental import pallas as pl
from jax.experimental.pallas import tpu as pltpu

F32 = jnp.float32
BF16 = jnp.bfloat16

EPS = 1e-6
ROPE_THETA = 10000.0
W_GROUP = 256
CONF_KERNEL = 31
CONF_HIST = 32
SUBLANES = 8
POOL_HIST = 16
SCONV_KERNEL = 3
SCONV_HIST = 8
DIFF_HEADS = 4
DIFF_HEAD_DIM = 32
DIFF_V_DIM = 64
V_AUG_ROWS = 80
N_GROUPS = 4
EXPERTS_PER_GROUP = 4
N_EXPERTS = 16
D_EXPERT = 256
ROUTER_LANES = 128
ROUTER_LO = 32
NEG_BIG = -1e30
LOG2E = 1.4426950408889634

VMEM_LIMIT = 48 * 1024 * 1024

SEQ_TILE = 512
ROW_CHUNK = 64
ATT_TILE = 256
ATT_Q_BLOCKS = 2
ATT_ROWS = 128
TOK_TILE = 512
FINAL_TILE = 1024
MOE_TILE = 256
N_BUCKETS = 24
N_BUCKET_ROWS = 32


def _cparams(sem):
    return pltpu.CompilerParams(dimension_semantics=sem, vmem_limit_bytes=VMEM_LIMIT)


def _layer_spec(arr, layer):
    return pl.BlockSpec((None,) + arr.shape[1:], lambda *_: (layer,) + (0,) * (arr.ndim - 1))


def _rms_rows(x, g):
    ms = jnp.mean(x * x, axis=-1, keepdims=True)
    return x * lax.rsqrt(ms + EPS) * g


def _sigmoid(x):
    return 1.0 / (1.0 + jnp.exp(-x))


def _ple_update(h, y, p, g, wg, bg, wp):
    h = h + y
    xn = _rms_rows(h, g).astype(BF16)
    gate = _sigmoid(jnp.dot(xn, wg, preferred_element_type=F32) + bg)
    return h + gate * jnp.dot(p.astype(BF16), wp, preferred_element_type=F32)


def _inproj_mix_kernel(*refs, tm, fused_ple):
    if fused_ple:
        (h_ref, y_ref, p_ref, pg_ref, pwg_ref, pbg_ref, pwp_ref), refs = refs[:7], refs[7:]
    else:
        h_ref, refs = refs[0], refs[1:]
    (g_ref, w_ref, rope_ref, cw_ref, cb_ref, lg_ref, lb_ref, pw_ref, pb_ref, ps_ref, sw_ref), refs = refs[:11], refs[11:]
    if fused_ple:
        hout_ref, refs = refs[0], refs[1:]
    yabd_ref, qt_ref, k_ref, vt_ref, gbuf, pbuf, zbuf, gshift = refs
    s = pl.program_id(1)

    @pl.when(s == 0)
    def _():
        gbuf[0:CONF_HIST, :] = jnp.zeros((CONF_HIST, W_GROUP), F32)
        pbuf[0:POOL_HIST, :] = jnp.zeros((POOL_HIST, W_GROUP), F32)
        zbuf[0:SCONV_HIST, :] = jnp.zeros((SCONV_HIST, W_GROUP), F32)

    if fused_ple:
        h = _ple_update(h_ref[...], y_ref[...], p_ref[...], pg_ref[...], pwg_ref[...], pbg_ref[...], pwp_ref[...])
        hout_ref[...] = h
    else:
        h = h_ref[...]
    n = _rms_rows(h, g_ref[...]).astype(BF16)

    def proj(lo, hi):
        return jnp.dot(n, w_ref[:, lo:hi], preferred_element_type=F32)

    a = proj(0, 2 * W_GROUP)
    gbuf[CONF_HIST:CONF_HIST + tm, :] = a[:, :W_GROUP] * _sigmoid(a[:, W_GROUP:])
    span = tm + CONF_HIST - SUBLANES
    for sh in range(1, SUBLANES):
        gshift[sh - 1, 0:span, :] = gbuf[sh:sh + span, :]
    base = CONF_HIST - (CONF_KERNEL - 1)
    for c in range(tm // ROW_CHUNK):
        r0 = c * ROW_CHUNK
        acc = jnp.broadcast_to(cb_ref[...], (ROW_CHUNK, W_GROUP))
        for j in range(CONF_KERNEL):
            sh = (base + j) % SUBLANES
            q0 = r0 + base + j - sh
            win = gbuf[q0:q0 + ROW_CHUNK, :] if sh == 0 else gshift[sh - 1, q0:q0 + ROW_CHUNK, :]
            acc = acc + cw_ref[j:j + 1, :] * win
        mu = jnp.mean(acc, axis=-1, keepdims=True)
        d = acc - mu
        var = jnp.mean(d * d, axis=-1, keepdims=True)
        y = d * lax.rsqrt(var + EPS) * lg_ref[...] + lb_ref[...]
        yabd_ref[r0:r0 + ROW_CHUNK, 0:W_GROUP] = (y * _sigmoid(y)).astype(BF16)
    gbuf[0:CONF_HIST, :] = gbuf[tm:tm + CONF_HIST, :]

    pbuf[POOL_HIST:POOL_HIST + tm, :] = proj(2 * W_GROUP, 3 * W_GROUP)
    lane = lax.broadcasted_iota(jnp.int32, (ROW_CHUNK, 128), 1)
    first = lane < 64
    for c in range(tm // ROW_CHUNK):
        r0 = c * ROW_CHUNK
        tpos = (s * tm + r0 + 1 + lax.broadcasted_iota(jnp.int32, (ROW_CHUNK, 128), 0)).astype(F32)
        halves = []
        for half, (w_small, w_big) in enumerate(((2, 4), (8, 16))):
            l0 = half * 128

            def ld(j):
                return pbuf[r0 + POOL_HIST - j:r0 + POOL_HIST - j + ROW_CHUNK, l0:l0 + 128]

            cur = ld(0)
            run = cur
            for j in range(1, w_small):
                run = run + ld(j)
            small = run
            for j in range(w_small, w_big):
                run = run + ld(j)
            cnt = jnp.where(first, jnp.minimum(tpos, float(w_small)), jnp.minimum(tpos, float(w_big)))
            halves.append(jnp.where(first, small, run) / cnt - cur)
        pc = jnp.concatenate(halves, axis=1).astype(BF16)
        yb = (jnp.dot(pc, pw_ref[...], preferred_element_type=F32) + pb_ref[...]) * ps_ref[...]
        yabd_ref[r0:r0 + ROW_CHUNK, W_GROUP:2 * W_GROUP] = yb.astype(BF16)
    pbuf[0:POOL_HIST, :] = pbuf[tm:tm + POOL_HIST, :]

    sc = proj(6 * W_GROUP, 9 * W_GROUP)
    zbuf[SCONV_HIST:SCONV_HIST + tm, :] = sc[:, W_GROUP:2 * W_GROUP] * sc[:, 2 * W_GROUP:]
    conv = None
    for j in range(SCONV_KERNEL):
        off = SCONV_HIST - (SCONV_KERNEL - 1) + j
        term = sw_ref[j:j + 1, :] * zbuf[off:off + tm, :]
        conv = term if conv is None else conv + term
    yabd_ref[:, 2 * W_GROUP:3 * W_GROUP] = (sc[:, :W_GROUP] * conv).astype(BF16)
    zbuf[0:SCONV_HIST, :] = zbuf[tm:tm + SCONV_HIST, :]

    qk = proj(3 * W_GROUP, 5 * W_GROUP)
    tab = rope_ref[...]
    cos = jnp.concatenate([tab[:, 0:128]] * 4, axis=1)
    sin_lo = jnp.concatenate([tab[:, 128:256]] * 4, axis=1)
    sin_hi = jnp.concatenate([tab[:, 256:384]] * 4, axis=1)
    half = DIFF_HEAD_DIM // 2
    width = 2 * W_GROUP
    qk = qk * cos + pltpu.roll(qk, width - half, axis=1) * sin_lo + pltpu.roll(qk, half, axis=1) * sin_hi
    q = qk[:, :W_GROUP] * (DIFF_HEAD_DIM ** -0.5 * LOG2E)
    qt_ref[...] = q.T.astype(BF16)
    k_ref[...] = qk[:, W_GROUP:].astype(BF16)
    vt = proj(5 * W_GROUP, 6 * W_GROUP).T.astype(BF16)
    for hd in range(DIFF_HEADS):
        r0 = hd * V_AUG_ROWS
        vt_ref[r0:r0 + DIFF_V_DIM, :] = vt[hd * DIFF_V_DIM:(hd + 1) * DIFF_V_DIM, :]
        vt_ref[r0 + DIFF_V_DIM:r0 + V_AUG_ROWS, :] = jnp.ones((V_AUG_ROWS - DIFF_V_DIM, tm), BF16)


def _inproj_mix(h, layer, prm, ple=None):
    bsz, seq, d = h.shape
    tm = min(SEQ_TILE, seq)
    assert seq % tm == 0 and tm % ROW_CHUNK == 0 and tm >= CONF_HIST
    tok = lambda w: pl.BlockSpec((None, tm, w), lambda b, s: (b, s, 0))
    tok_t = lambda rows: pl.BlockSpec((None, rows, tm), lambda b, s: (b, 0, s))
    args, in_specs = [h], [tok(d)]
    out_specs, out_shape = [], []
    if ple is not None:
        y, prev = ple
        pd = prm["p"].shape[-1]
        stacked = [prm[n] for n in ("ple_norm", "ple_wg", "ple_bg", "ple_wp")]
        args += [y, prm["p"]] + stacked
        in_specs += [tok(d), pl.BlockSpec((None, None, tm, pd), lambda b, s: (prev, b, s, 0))]
        in_specs += [_layer_spec(a, prev) for a in stacked]
        out_specs.append(tok(d))
        out_shape.append(jax.ShapeDtypeStruct((bsz, seq, d), F32))
    stacked = [prm[n] for n in ("mix_norm", "w_in")]
    args += stacked + [prm["rope"]]
    in_specs += [_layer_spec(a, layer) for a in stacked] + [pl.BlockSpec((tm, 384), lambda b, s: (s, 0))]
    stacked = [prm[n] for n in ("conf_w", "conf_b", "ln_g", "ln_b", "pool_bd", "pool_b", "pool_s", "sconv_w")]
    args += stacked
    in_specs += [_layer_spec(a, layer) for a in stacked]
    out_specs += [tok(3 * W_GROUP), tok_t(W_GROUP), tok(W_GROUP), tok_t(DIFF_HEADS * V_AUG_ROWS)]
    out_shape += [
        jax.ShapeDtypeStruct((bsz, seq, 3 * W_GROUP), BF16),
        jax.ShapeDtypeStruct((bsz, W_GROUP, seq), BF16),
        jax.ShapeDtypeStruct((bsz, seq, W_GROUP), BF16),
        jax.ShapeDtypeStruct((bsz, DIFF_HEADS * V_AUG_ROWS, seq), BF16),
    ]
    return pl.pallas_call(
        functools.partial(_inproj_mix_kernel, tm=tm, fused_ple=ple is not None),
        grid=(bsz, seq // tm),
        in_specs=in_specs,
        out_specs=out_specs,
        out_shape=out_shape,
        scratch_shapes=[
            pltpu.VMEM((tm + CONF_HIST, W_GROUP), F32),
            pltpu.VMEM((tm + POOL_HIST, W_GROUP), F32),
            pltpu.VMEM((tm + SCONV_HIST, W_GROUP), F32),
            pltpu.VMEM((SUBLANES - 1, tm + CONF_HIST - SUBLANES, W_GROUP), F32),
        ],
        compiler_params=_cparams(("arbitrary", "arbitrary")),
        name="inproj_mix",
    )(*args)


def _diff_attn_kernel(qt_ref, k_ref, vt_ref, lam_ref, g_ref, o_ref,
                      qm_ref, s0_ref, s1_ref, p0_ref, p1_ref, al0_ref, al1_ref, m_ref, l_ref, acc_ref,
                      *, t, nq, lam_init):
    qi = pl.program_id(1)
    nhc = 2 * DIFF_HEADS
    tq = nq * t

    qm_ref[...] = jnp.zeros(qm_ref.shape, BF16)
    for j in range(nhc):
        lo = j * DIFF_HEAD_DIM
        qm_ref[lo:lo + DIFF_HEAD_DIM, j * tq:(j + 1) * tq] = qt_ref[lo:lo + DIFF_HEAD_DIM, :]
    m_ref[...] = jnp.full(m_ref.shape, NEG_BIG, F32)
    l_ref[...] = jnp.zeros(l_ref.shape, F32)
    acc_ref[...] = jnp.zeros(acc_ref.shape, F32)
    p1_ref[...] = jnp.zeros(p1_ref.shape, BF16)
    al1_ref[...] = jnp.ones(al1_ref.shape, F32)

    def scores(kt, s_ref):
        k0 = pl.multiple_of(kt * t, t)
        s_ref[...] = jnp.dot(k_ref[pl.ds(k0, t), :], qm_ref[...], preferred_element_type=F32)

    def softmax(s_ref, p_ref, al_ref, diag):
        if diag is not None:
            above = (lax.broadcasted_iota(jnp.int32, (t, t), 0) > lax.broadcasted_iota(jnp.int32, (t, t), 1))
        for j in range(nhc):
            for r in range(nq):
                cols = slice(j * tq + r * t, j * tq + (r + 1) * t)
                qs = slice(r * t, (r + 1) * t)
                if diag is not None and r < diag:
                    p_ref[:, cols] = jnp.zeros((t, t), BF16)
                    al_ref[j, :, qs] = jnp.ones((1, t), F32)
                    continue
                tri = diag is not None and r == diag

                def chunk(c):
                    blk = s_ref[c * ATT_ROWS:(c + 1) * ATT_ROWS, cols]
                    return jnp.where(above[c * ATT_ROWS:(c + 1) * ATT_ROWS], NEG_BIG, blk) if tri else blk

                m_old = m_ref[j, :, qs]
                m_new = m_old
                for c in range(t // ATT_ROWS):
                    m_new = jnp.maximum(m_new, jnp.max(chunk(c), axis=0, keepdims=True))
                al_ref[j, :, qs] = jnp.exp2(m_old - m_new)
                m_ref[j, :, qs] = m_new
                for c in range(t // ATT_ROWS):
                    p_ref[c * ATT_ROWS:(c + 1) * ATT_ROWS, cols] = jnp.exp2(chunk(c) - m_new).astype(BF16)

    def weighted_values(p_ref, al_ref, kt):
        k0 = pl.multiple_of(jnp.maximum(kt, 0) * t, t)
        for hd in range(DIFF_HEADS):
            vth = vt_ref[hd * V_AUG_ROWS:(hd + 1) * V_AUG_ROWS, pl.ds(k0, t)]
            o = jnp.dot(vth, p_ref[:, 2 * hd * tq:(2 * hd + 2) * tq], preferred_element_type=F32)
            for c in range(2):
                j = 2 * hd + c
                alpha = al_ref[j]
                acc_ref[j] = alpha * acc_ref[j] + o[0:DIFF_V_DIM, c * tq:(c + 1) * tq]
                l_ref[j] = alpha * l_ref[j] + o[DIFF_V_DIM:DIFF_V_DIM + 1, c * tq:(c + 1) * tq]

    scores(0, s0_ref)

    def pair(i, carry):
        kt = 2 * i
        scores(kt + 1, s1_ref)
        softmax(s0_ref, p0_ref, al0_ref, None)
        weighted_values(p1_ref, al1_ref, kt - 1)
        scores(kt + 2, s0_ref)
        softmax(s1_ref, p1_ref, al1_ref, None)
        weighted_values(p0_ref, al0_ref, kt)
        return carry

    lax.fori_loop(0, qi, pair, 0)

    kd = 2 * qi
    scores(kd + 1, s1_ref)
    softmax(s0_ref, p0_ref, al0_ref, 0)
    weighted_values(p1_ref, al1_ref, kd - 1)
    softmax(s1_ref, p1_ref, al1_ref, 1)
    weighted_values(p0_ref, al0_ref, kd)
    weighted_values(p1_ref, al1_ref, kd + 1)

    lp = lam_ref[...]
    lam = (jnp.exp(jnp.sum(lp[0:1] * lp[1:2], axis=-1, keepdims=True))
           - jnp.exp(jnp.sum(lp[2:3] * lp[3:4], axis=-1, keepdims=True)) + lam_init)
    outs = []
    for hd in range(DIFF_HEADS):
        o1 = acc_ref[2 * hd] / l_ref[2 * hd]
        o2 = acc_ref[2 * hd + 1] / l_ref[2 * hd + 1]
        o = o1 - lam * o2
        ms = jnp.mean(o * o, axis=0, keepdims=True)
        outs.append(o * lax.rsqrt(ms + EPS) * g_ref[...] * (1.0 - lam_init))
    o_ref[...] = jnp.concatenate(outs, axis=0).T.astype(BF16)


def _diff_attn(qt, k, vt, layer, prm, lam_init):
    bsz, seq, _ = k.shape
    t, nq = ATT_TILE, ATT_Q_BLOCKS
    tq = nq * t
    assert seq % tq == 0 and nq == 2
    nhc = 2 * DIFF_HEADS
    return pl.pallas_call(
        functools.partial(_diff_attn_kernel, t=t, nq=nq, lam_init=lam_init),
        grid=(bsz, seq // tq),
        in_specs=[
            pl.BlockSpec((None, W_GROUP, tq), lambda b, q: (b, 0, q)),
            pl.BlockSpec((None, seq, W_GROUP), lambda b, q: (b, 0, 0)),
            pl.BlockSpec((None, DIFF_HEADS * V_AUG_ROWS, seq), lambda b, q: (b, 0, 0)),
            _layer_spec(prm["lam"], layer),
            _layer_spec(prm["subln_g"], layer),
        ],
        out_specs=pl.BlockSpec((None, tq, W_GROUP), lambda b, q: (b, q, 0)),
        out_shape=jax.ShapeDtypeStruct((bsz, seq, W_GROUP), BF16),
        scratch_shapes=[
            pltpu.VMEM((W_GROUP, nhc * tq), BF16),
            pltpu.VMEM((t, nhc * tq), F32), pltpu.VMEM((t, nhc * tq), F32),
            pltpu.VMEM((t, nhc * tq), BF16), pltpu.VMEM((t, nhc * tq), BF16),
            pltpu.VMEM((nhc, 1, tq), F32), pltpu.VMEM((nhc, 1, tq), F32),
            pltpu.VMEM((nhc, 1, tq), F32),
            pltpu.VMEM((nhc, 1, tq), F32),
            pltpu.VMEM((nhc, DIFF_V_DIM, tq), F32),
        ],
        compiler_params=_cparams(("arbitrary", "arbitrary")),
        name="diff_attn",
    )(qt, k, vt, prm["lam"], prm["subln_g"])


def _outproj_router_kernel(h_ref, yabd_ref, yc_ref, wo_ref, g_ref, wr_ref, br_ref,
                           h1_ref, xs_ref, meta_ref, cnt_ref, rt_ref, ct_ref, oh_ref, carry_ref, *, tm):
    d = h_ref.shape[1]

    @pl.when(pl.program_id(0) == 0)
    def _():
        carry_ref[...] = jnp.zeros(carry_ref.shape, F32)

    wg2 = 2 * W_GROUP
    h1 = (h_ref[...]
          + jnp.dot(yabd_ref[:, 0:wg2], wo_ref[0:wg2, :], preferred_element_type=F32)
          + jnp.dot(yc_ref[...], wo_ref[wg2:wg2 + W_GROUP, :], preferred_element_type=F32)
          + jnp.dot(yabd_ref[:, wg2:], wo_ref[wg2 + W_GROUP:, :], preferred_element_type=F32))
    h1_ref[...] = h1
    xn = _rms_rows(h1, g_ref[...])
    x_hi = xn.astype(BF16)
    x_lo = (xn - x_hi.astype(F32)).astype(BF16)
    xs_ref[:, 0:d] = xn
    r = (jnp.dot(x_hi, wr_ref[...], preferred_element_type=F32)
         + jnp.dot(x_lo, wr_ref[...], preferred_element_type=F32) + br_ref[...])
    rt_ref[...] = r.T

    def row(i):
        return rt_ref[i:i + 1, :] + rt_ref[ROUTER_LO + i:ROUTER_LO + i + 1, :]

    def first_hits(vals, target):
        hits, free = [], None
        for v in vals:
            f = jnp.where(v == target, 1.0, 0.0)
            hits.append(f if free is None else f * free)
            free = (1.0 - f) if free is None else free * (1.0 - f)
        return hits

    add = lambda a, b: a + b
    gl = [row(i) for i in range(N_GROUPS)]
    el = [row(8 + e) for e in range(N_EXPERTS)]
    gmax = functools.reduce(jnp.maximum, gl)
    gsel = first_hits(gl, gmax)
    pg_c = 1.0 / functools.reduce(add, [jnp.exp(v - gmax) for v in gl])
    elc = []
    for kk in range(EXPERTS_PER_GROUP):
        v = gsel[0] * el[kk]
        for g in range(1, N_GROUPS):
            v = v + gsel[g] * el[g * EXPERTS_PER_GROUP + kk]
        elc.append(v)
    emax = functools.reduce(jnp.maximum, elc)
    ex = [jnp.exp(v - emax) for v in elc]
    se = functools.reduce(add, ex)
    pe = [v / se for v in ex]
    p1 = functools.reduce(jnp.maximum, pe)
    t1 = first_hits(pe, p1)
    rest = [jnp.where(t > 0.0, -1.0, v) for t, v in zip(t1, pe)]
    p2 = functools.reduce(jnp.maximum, rest)
    t2 = first_hits(rest, p2)
    den = p1 + p2
    w1 = pg_c * (p1 / den)
    w2 = pg_c * (p2 / den)
    wk = [a * w1 + b * w2 for a, b in zip(t1, t2)]
    sel = [a + b for a, b in zip(t1, t2)]
    lower = first_hits(sel, 1.0)
    c_a = functools.reduce(add, [f * w for f, w in zip(lower, wk)])
    c_b = functools.reduce(add, [(s - f) * w for s, f, w in zip(sel, lower, wk)])
    pair = (sel[0] * sel[2] + 2.0 * sel[0] * sel[3] + 3.0 * sel[1] * sel[2]
            + 4.0 * sel[1] * sel[3] + 5.0 * sel[2] * sel[3])
    bucket = 6.0 * (gsel[1] + 2.0 * gsel[2] + 3.0 * gsel[3]) + pair

    ct_ref[...] = jnp.zeros(ct_ref.shape, F32)
    ct_ref[0:1, :] = c_a
    ct_ref[1:2, :] = c_b
    xs_ref[:, d:d + ROUTER_LANES] = ct_ref[...].T

    for b in range(N_BUCKET_ROWS):
        oh_ref[b:b + 1, :] = jnp.where(bucket == float(b), 1.0, 0.0)
    oh = oh_ref[...]
    upper = (lax.broadcasted_iota(jnp.int32, (tm, tm), 0) <= lax.broadcasted_iota(jnp.int32, (tm, tm), 1))
    cum = jnp.dot(oh.astype(BF16), jnp.where(upper, 1.0, 0.0).astype(BF16), preferred_element_type=F32)
    carry = carry_ref[...]
    rank = jnp.sum(oh * (cum - 1.0 + carry), axis=0, keepdims=True)
    carry_new = carry + jnp.sum(oh, axis=1, keepdims=True)
    carry_ref[...] = carry_new
    cnt_ref[...] = carry_new[:, 0:ROUTER_LANES]
    meta_ref[...] = jnp.zeros(meta_ref.shape, jnp.int32)
    meta_ref[0:1, :] = bucket.astype(jnp.int32)
    meta_ref[1:2, :] = rank.astype(jnp.int32)


def _outproj_router(h2d, yabd, yc, layer, prm):
    tkn, d = h2d.shape
    tm = min(TOK_TILE, tkn)
    assert tkn % tm == 0
    row = lambda w: pl.BlockSpec((tm, w), lambda i: (i, 0))
    full = lambda shape: pl.BlockSpec(shape, lambda i: (0,) * len(shape))
    stacked = [prm[n] for n in ("w_out", "ffn_norm", "router_w", "router_b")]
    return pl.pallas_call(
        functools.partial(_outproj_router_kernel, tm=tm),
        grid=(tkn // tm,),
        in_specs=[row(d), row(3 * W_GROUP), row(W_GROUP)] + [_layer_spec(a, layer) for a in stacked],
        out_specs=[row(d), row(d + ROUTER_LANES),
                   pl.BlockSpec((None, 8, tm), lambda i: (i, 0, 0)),
                   full((N_BUCKET_ROWS, ROUTER_LANES))],
        out_shape=[jax.ShapeDtypeStruct((tkn, d), F32),
                   jax.ShapeDtypeStruct((tkn, d + ROUTER_LANES), F32),
                   jax.ShapeDtypeStruct((tkn // tm, 8, tm), jnp.int32),
                   jax.ShapeDtypeStruct((N_BUCKET_ROWS, ROUTER_LANES), F32)],
        scratch_shapes=[pltpu.VMEM((ROUTER_LANES, tm), F32), pltpu.VMEM((ROUTER_LANES, tm), F32),
                        pltpu.VMEM((N_BUCKET_ROWS, tm), F32), pltpu.VMEM((N_BUCKET_ROWS, tm), F32)],
        compiler_params=_cparams(("arbitrary",)),
        name="outproj_router",
    )(h2d, yabd, yc, *stacked)


def _moe_kernel(vt_ref, vea_ref, veb_ref, vlo_ref, vhi_ref, vfl_ref,
                tok_ref, tokn_ref, xs_hbm, wga_ref, wgb_ref, wua_ref, wub_ref, wda_ref, wdb_ref,
                y_hbm, xbuf, ybuf, gsem, ssem, *, tm, n_tiles, n_visits):
    v = pl.program_id(0)
    tile = vt_ref[v]
    flags = vfl_ref[v]
    first = (flags & 1) != 0
    last = (flags & 2) != 0
    valid = (flags & 4) != 0
    slot = tile % 2
    d = y_hbm.shape[1]

    def row_gather(ids_ref, s, k, u):
        return pltpu.make_async_copy(xs_hbm.at[pl.ds(ids_ref[0, k * SUBLANES + u], 1), :],
                                     xbuf.at[s, k, pl.ds(u, 1), :], gsem.at[s])

    def row_scatter(ids_ref, s, k, u):
        return pltpu.make_async_copy(ybuf.at[s, k, pl.ds(u, 1), :],
                                     y_hbm.at[pl.ds(ids_ref[0, k * SUBLANES + u], 1), :], ssem.at[s])

    def start_all(mk, ids_ref, s):
        def body(k, c):
            for u in range(SUBLANES):
                mk(ids_ref, s, k, u).start(priority=u % 2)
            return c
        lax.fori_loop(0, tm // SUBLANES, body, 0)

    def wait_gather(s):
        pltpu.make_async_copy(xbuf.at[s], xbuf.at[s], gsem.at[s]).wait()

    def wait_scatter(s):
        pltpu.make_async_copy(ybuf.at[s], ybuf.at[s], ssem.at[s]).wait()

    @pl.when(v == 0)
    def _():
        start_all(row_gather, tok_ref, 0)

    @pl.when(jnp.logical_and(first, valid))
    def _():
        wait_gather(slot)

        @pl.when(tile + 1 < n_tiles)
        def _():
            start_all(row_gather, tokn_ref, 1 - slot)

        @pl.when(tile >= 2)
        def _():
            wait_scatter(slot)

    @pl.when(valid)
    def _():
        rows = tile * tm + lax.broadcasted_iota(jnp.int32, (tm, 1), 0)
        inb = jnp.where(jnp.logical_and(rows >= vlo_ref[v], rows < vhi_ref[v]), 1.0, 0.0)
        xt = xbuf[slot].reshape(tm, xs_hbm.shape[1])
        x = xt[:, 0:d].astype(BF16)
        c_a = xt[:, d:d + 1] * inb
        c_b = xt[:, d + 1:d + 2] * inb

        def expert(wg_ref, wu_ref, wd_ref, c):
            gate = jnp.dot(x, wg_ref[...], preferred_element_type=F32)
            up = jnp.dot(x, wu_ref[...], preferred_element_type=F32)
            hdn = gate * _sigmoid(gate) * up * c
            return jnp.dot(hdn.astype(BF16), wd_ref[...], preferred_element_type=F32)

        y = expert(wga_ref, wua_ref, wda_ref, c_a) + expert(wgb_ref, wub_ref, wdb_ref, c_b)

        y = y.reshape(tm // SUBLANES, SUBLANES, d)

        @pl.when(first)
        def _():
            ybuf[slot] = y

        @pl.when(jnp.logical_not(first))
        def _():
            ybuf[slot] = ybuf[slot] + y

    @pl.when(jnp.logical_and(last, valid))
    def _():
        start_all(row_scatter, tok_ref, slot)

    @pl.when(v == n_visits - 1)
    def _():
        wait_scatter(0)
        wait_scatter(1)


def _moe(xs, tok_of_slot, sched, layer, prm):
    tkn, dx = xs.shape
    d = dx - ROUTER_LANES
    tm = MOE_TILE
    assert tkn % tm == 0 and tkn // tm >= 2
    n_tiles = tkn // tm
    n_visits = sched[0].shape[0]
    tok3 = tok_of_slot.reshape(n_tiles, 1, tm)
    wspec = lambda shape, which: pl.BlockSpec(
        (None, None) + shape, lambda v, vt, vea, veb, vlo, vhi, vfl: (layer, (vea, veb)[which][v], 0, 0))
    grid_spec = pltpu.PrefetchScalarGridSpec(
        num_scalar_prefetch=6,
        grid=(n_visits,),
        in_specs=[
            pl.BlockSpec((None, 1, tm), lambda v, vt, *_: (vt[v], 0, 0), memory_space=pltpu.SMEM),
            pl.BlockSpec((None, 1, tm), lambda v, vt, *_: (jnp.minimum(vt[v] + 1, n_tiles - 1), 0, 0),
                         memory_space=pltpu.SMEM),
            pl.BlockSpec(memory_space=pl.ANY),
            wspec((d, D_EXPERT), 0), wspec((d, D_EXPERT), 1),
            wspec((d, D_EXPERT), 0), wspec((d, D_EXPERT), 1),
            wspec((D_EXPERT, d), 0), wspec((D_EXPERT, d), 1),
        ],
        out_specs=pl.BlockSpec(memory_space=pl.ANY),
        scratch_shapes=[pltpu.VMEM((2, tm // SUBLANES, SUBLANES, dx), F32),
                        pltpu.VMEM((2, tm // SUBLANES, SUBLANES, d), F32),
                        pltpu.SemaphoreType.DMA((2,)), pltpu.SemaphoreType.DMA((2,))],
    )
    return pl.pallas_call(
        functools.partial(_moe_kernel, tm=tm, n_tiles=n_tiles, n_visits=n_visits),
        grid_spec=grid_spec,
        out_shape=jax.ShapeDtypeStruct((tkn, d), F32),
        compiler_params=_cparams(("arbitrary",)),
        name="moe_experts",
    )(*sched, tok3, tok3, xs, prm["exp_wg"], prm["exp_wg"], prm["exp_wu"], prm["exp_wu"], prm["exp_wd"], prm["exp_wd"])


def _moe_schedule(bucket, rank, counts, tm):
    tkn = bucket.size
    n_tiles = tkn // tm
    n_visits = n_tiles + N_BUCKETS
    ends = jnp.cumsum(counts)
    offs = ends - counts
    slot = rank
    for bb in range(N_BUCKETS):
        slot = slot + jnp.where(bucket == bb, offs[bb], 0)
    _, tok_of_slot = lax.sort_key_val(slot.reshape(tkn), jnp.arange(tkn, dtype=jnp.int32))
    t_first = offs // tm
    t_last = jnp.maximum(ends - 1, 0) // tm
    nv = jnp.where(counts > 0, t_last - t_first + 1, 0)
    v_end = jnp.cumsum(nv)
    v_start = v_end - nv
    total = v_end[-1]
    v = jnp.arange(n_visits, dtype=jnp.int32)
    vc = jnp.minimum(v, total - 1)
    b = jnp.sum((v_end[None, :] <= vc[:, None]).astype(jnp.int32), axis=1)
    onehot = b[:, None] == jnp.arange(N_BUCKETS, dtype=jnp.int32)[None, :]
    pick = lambda table: jnp.sum(jnp.where(onehot, table[None, :], 0), axis=1)
    tile = (pick(t_first) + vc - pick(v_start)).astype(jnp.int32)
    valid = v < total
    prev_tile = jnp.concatenate([jnp.full((1,), -1, jnp.int32), tile[:-1]])
    next_tile = jnp.concatenate([tile[1:], jnp.full((1,), -1, jnp.int32)])
    first = tile != prev_tile
    last = jnp.logical_or(tile != next_tile, v == total - 1)
    flags = (first.astype(jnp.int32) + 2 * last.astype(jnp.int32) + 4) * valid.astype(jnp.int32)
    g, pr = b // 6, b % 6
    lo_e = (pr >= 3).astype(jnp.int32) + (pr >= 5).astype(jnp.int32)
    hi_e = pr + 1 - 2 * (pr >= 3).astype(jnp.int32) - (pr >= 5).astype(jnp.int32)
    ea = g * EXPERTS_PER_GROUP + lo_e
    eb = g * EXPERTS_PER_GROUP + hi_e
    sched = (tile, ea.astype(jnp.int32), eb.astype(jnp.int32),
             pick(offs).astype(jnp.int32), pick(ends).astype(jnp.int32), flags.astype(jnp.int32))
    return tok_of_slot, sched


def _ple_final_kernel(h_ref, y_ref, p_ref, g_ref, wg_ref, bg_ref, wp_ref, fg_ref, o_ref):
    out = _ple_update(h_ref[...], y_ref[...], p_ref[...], g_ref[...], wg_ref[...], bg_ref[...], wp_ref[...])
    o_ref[...] = _rms_rows(out, fg_ref[...])


def _ple_final(h2d, y2d, layer, prm):
    tkn, d = h2d.shape
    p4 = prm["p"]
    _, bsz, seq, pd = p4.shape
    tm = min(FINAL_TILE, seq)
    assert seq % tm == 0 and tkn == bsz * seq
    per_row = seq // tm
    stacked = [prm[n] for n in ("ple_norm", "ple_wg", "ple_bg", "ple_wp")]
    return pl.pallas_call(
        _ple_final_kernel,
        grid=(tkn // tm,),
        in_specs=[pl.BlockSpec((tm, d), lambda i: (i, 0)), pl.BlockSpec((tm, d), lambda i: (i, 0)),
                  pl.BlockSpec((None, None, tm, pd), lambda i: (layer, i // per_row, i % per_row, 0))]
                 + [_layer_spec(a, layer) for a in stacked]
                 + [pl.BlockSpec((1, d), lambda i: (0, 0))],
        out_specs=pl.BlockSpec((tm, d), lambda i: (i, 0)),
        out_shape=jax.ShapeDtypeStruct((tkn, d), F32),
        compiler_params=_cparams(("arbitrary",)),
        name="ple_final",
    )(h2d, y2d, p4, *stacked, prm["final_norm"])


def _rope_table(seq):
    pos = np.arange(seq, dtype=np.float64)
    inv = ROPE_THETA ** (-np.arange(0, DIFF_HEAD_DIM, 2, dtype=np.float64) / DIFF_HEAD_DIM)
    ang = pos[:, None] * inv[None, :]
    ang = np.concatenate([ang, ang], axis=-1)
    cos, sin = np.cos(ang), np.sin(ang)
    lo = np.arange(DIFF_HEAD_DIM) < DIFF_HEAD_DIM // 2
    sin_lo = np.where(lo, -sin, 0.0)
    sin_hi = np.where(lo, 0.0, sin)
    rep = 128 // DIFF_HEAD_DIM
    table = np.concatenate([np.tile(t, (1, rep)) for t in (cos, sin_lo, sin_hi)], axis=1)
    return jnp.asarray(table.astype(np.float32))


def _block_diag(w):
    nl, g, c, dd = w.shape
    out = jnp.zeros((nl, g * c, g * dd), w.dtype)
    for i in range(g):
        out = out.at[:, i * c:(i + 1) * c, i * dd:(i + 1) * dd].set(w[:, i])
    return out


def _router_weights(wg, bg, we, be):
    nl, d, _ = wg.shape
    w = jnp.zeros((nl, d, ROUTER_LANES), F32)
    w = w.at[:, :, 0:N_GROUPS].set(wg).at[:, :, 8:8 + N_EXPERTS].set(we)
    w_hi = w.astype(BF16)
    w_lo = (w - w_hi.astype(F32)).astype(BF16)
    wr = w_hi.at[:, :, ROUTER_LO:ROUTER_LO + 8 + N_EXPERTS].set(w_lo[:, :, 0:8 + N_EXPERTS])
    br = jnp.zeros((nl, 1, ROUTER_LANES), F32)
    br = br.at[:, 0, 0:N_GROUPS].set(bg).at[:, 0, 8:8 + N_EXPERTS].set(be)
    return wr, br


def kernel(x, p, mix_norm, w_in, conf_conv_w, conf_conv_b, conf_ln_g, conf_ln_b, pool_w, pool_b, pool_scale, diff_lam_q1, diff_lam_k1, diff_lam_q2, diff_lam_k2, diff_subln_g, sconv_w, w_out, ffn_norm, router_group_w, router_group_b, router_expert_w, router_expert_b, expert_w_gate, expert_w_up, expert_w_down, ple_norm, ple_gate_w, ple_gate_b, ple_proj, final_norm):
    bsz, seq, d = x.shape
    depth = w_in.shape[0]
    tkn = bsz * seq
    att_t = ATT_TILE * ATT_Q_BLOCKS
    rows = lambda v: v.reshape(depth, 1, -1).astype(F32)
    router_w, router_b = _router_weights(router_group_w, router_group_b, router_expert_w, router_expert_b)
    prm = dict(
        p=p, rope=_rope_table(seq),
        mix_norm=rows(mix_norm), w_in=w_in.astype(BF16),
        conf_w=conf_conv_w, conf_b=rows(conf_conv_b), ln_g=rows(conf_ln_g), ln_b=rows(conf_ln_b),
        pool_bd=_block_diag(pool_w).astype(BF16), pool_b=rows(pool_b), pool_s=rows(pool_scale), sconv_w=sconv_w,
        lam=jnp.stack([diff_lam_q1, diff_lam_k1, diff_lam_q2, diff_lam_k2], axis=1).astype(F32),
        subln_g=jnp.broadcast_to(diff_subln_g.astype(F32)[:, :, None], (depth, DIFF_V_DIM, att_t)),
        w_out=w_out.astype(BF16), ffn_norm=rows(ffn_norm), router_w=router_w, router_b=router_b,
        exp_wg=expert_w_gate.astype(BF16), exp_wu=expert_w_up.astype(BF16), exp_wd=expert_w_down.astype(BF16),
        ple_norm=rows(ple_norm), ple_wg=ple_gate_w.astype(BF16), ple_bg=rows(ple_gate_b),
        ple_wp=ple_proj.astype(BF16), final_norm=final_norm.reshape(1, -1).astype(F32),
    )

    h, pending = x, None
    for i in range(depth):
        lam_init = 0.8 - 0.6 * math.exp(-0.3 * i)
        if pending is None:
            yabd, qt, k, vt = _inproj_mix(h, i, prm)
        else:
            h, yabd, qt, k, vt = _inproj_mix(h, i, prm, ple=pending)
        yc = _diff_attn(qt, k, vt, i, prm, lam_init)
        h1, xs, meta, cnt = _outproj_router(
            h.reshape(tkn, d), yabd.reshape(tkn, 3 * W_GROUP), yc.reshape(tkn, W_GROUP), i, prm)
        tok_of_slot, sched = _moe_schedule(
            meta[:, 0, :], meta[:, 1, :], cnt[:N_BUCKETS, 0].astype(jnp.int32), MOE_TILE)
        y = _moe(xs, tok_of_slot, sched, i, prm)
        if i == depth - 1:
            return _ple_final(h1, y, i, prm).reshape(bsz, seq, d)
        h = h1.reshape(bsz, seq, d)
        pending = (y.reshape(bsz, seq, d), i)
```

```python
import functools
import math

import jax
import jax.numpy as jnp
import numpy as np
from jax import lax
from jax.experimental import pallas as pl
from jax.experimental.pallas import tpu as pltpu

F32 = jnp.float32
BF16 = jnp.bfloat16

EPS = 1e-6
ROPE_THETA = 10000.0
W_GROUP = 256
CONF_KERNEL = 31
CONF_HIST = 32
SUBLANES = 8
POOL_HIST = 16
SCONV_KERNEL = 3
SCONV_HIST = 8
DIFF_HEADS = 4
DIFF_HEAD_DIM = 32
DIFF_V_DIM = 64
V_AUG_ROWS = 80
N_GROUPS = 4
EXPERTS_PER_GROUP = 4
N_EXPERTS = 16
D_EXPERT = 256
ROUTER_LANES = 128
ROUTER_LO = 32
NEG_BIG = -1e30
LOG2E = 1.4426950408889634

VMEM_LIMIT = 48 * 1024 * 1024

SEQ_TILE = 512
ROW_CHUNK = 32
ATT_TILE = 256
ATT_Q_BLOCKS = 2
TOK_TILE = 512
FINAL_TILE = 1024
MOE_TILE = 256
N_BUCKETS = 24
N_BUCKET_ROWS = 32


def _cparams(sem):
    return pltpu.CompilerParams(dimension_semantics=sem, vmem_limit_bytes=VMEM_LIMIT)


def _layer_spec(arr, layer):
    return pl.BlockSpec((None,) + arr.shape[1:], lambda *_: (layer,) + (0,) * (arr.ndim - 1))


def _rms_rows(x, g):
    ms = jnp.mean(x * x, axis=-1, keepdims=True)
    return x * lax.rsqrt(ms + EPS) * g


def _sigmoid(x):
    return 1.0 / (1.0 + jnp.exp(-x))


def _ple_update(h, y, p, g, wg, bg, wp):
    h = h + y
    xn = _rms_rows(h, g).astype(BF16)
    gate = _sigmoid(jnp.dot(xn, wg, preferred_element_type=F32) + bg)
    return h + gate * jnp.dot(p.astype(BF16), wp, preferred_element_type=F32)


def _inproj_mix_kernel(*refs, tm, fused_ple):
    if fused_ple:
        (h_ref, y_ref, p_ref, pg_ref, pwg_ref, pbg_ref, pwp_ref), refs = refs[:7], refs[7:]
    else:
        h_ref, refs = refs[0], refs[1:]
    (g_ref, w_ref, rope_ref, cw_ref, cb_ref, lg_ref, lb_ref, pw_ref, pb_ref, ps_ref, sw_ref), refs = refs[:11], refs[11:]
    if fused_ple:
        hout_ref, refs = refs[0], refs[1:]
    yabd_ref, qt_ref, k_ref, vt_ref, gbuf, pbuf, zbuf, gshift = refs
    s = pl.program_id(1)

    @pl.when(s == 0)
    def _():
        gbuf[0:CONF_HIST, :] = jnp.zeros((CONF_HIST, W_GROUP), F32)
        pbuf[0:POOL_HIST, :] = jnp.zeros((POOL_HIST, W_GROUP), F32)
        zbuf[0:SCONV_HIST, :] = jnp.zeros((SCONV_HIST, W_GROUP), F32)

    if fused_ple:
        h = _ple_update(h_ref[...], y_ref[...], p_ref[...], pg_ref[...], pwg_ref[...], pbg_ref[...], pwp_ref[...])
        hout_ref[...] = h
    else:
        h = h_ref[...]
    n = _rms_rows(h, g_ref[...]).astype(BF16)

    def proj(lo, hi):
        return jnp.dot(n, w_ref[:, lo:hi], preferred_element_type=F32)

    a = proj(0, 2 * W_GROUP)
    gbuf[CONF_HIST:CONF_HIST + tm, :] = a[:, :W_GROUP] * _sigmoid(a[:, W_GROUP:])
    span = tm + CONF_HIST - SUBLANES
    for sh in range(1, SUBLANES):
        gshift[sh - 1, 0:span, :] = gbuf[sh:sh + span, :]
    base = CONF_HIST - (CONF_KERNEL - 1)
    for c in range(tm // ROW_CHUNK):
        r0 = c * ROW_CHUNK
        acc = jnp.broadcast_to(cb_ref[...], (ROW_CHUNK, W_GROUP))
        for j in range(CONF_KERNEL):
            sh = (base + j) % SUBLANES
            q0 = r0 + base + j - sh
            win = gbuf[q0:q0 + ROW_CHUNK, :] if sh == 0 else gshift[sh - 1, q0:q0 + ROW_CHUNK, :]
            acc = acc + cw_ref[j:j + 1, :] * win
        mu = jnp.mean(acc, axis=-1, keepdims=True)
        d = acc - mu
        var = jnp.mean(d * d, axis=-1, keepdims=True)
        y = d * lax.rsqrt(var + EPS) * lg_ref[...] + lb_ref[...]
        yabd_ref[r0:r0 + ROW_CHUNK, 0:W_GROUP] = (y * _sigmoid(y)).astype(BF16)
    gbuf[0:CONF_HIST, :] = gbuf[tm:tm + CONF_HIST, :]

    pbuf[POOL_HIST:POOL_HIST + tm, :] = proj(2 * W_GROUP, 3 * W_GROUP)
    lane = lax.broadcasted_iota(jnp.int32, (ROW_CHUNK, 128), 1)
    first = lane < 64
    for c in range(tm // ROW_CHUNK):
        r0 = c * ROW_CHUNK
        tpos = (s * tm + r0 + 1 + lax.broadcasted_iota(jnp.int32, (ROW_CHUNK, 128), 0)).astype(F32)
        halves = []
        for half, (w_small, w_big) in enumerate(((2, 4), (8, 16))):
            l0 = half * 128

            def ld(j):
                return pbuf[r0 + POOL_HIST - j:r0 + POOL_HIST - j + ROW_CHUNK, l0:l0 + 128]

            cur = ld(0)
            run = cur
            for j in range(1, w_small):
                run = run + ld(j)
            small = run
            for j in range(w_small, w_big):
                run = run + ld(j)
            cnt = jnp.where(first, jnp.minimum(tpos, float(w_small)), jnp.minimum(tpos, float(w_big)))
            halves.append(jnp.where(first, small, run) / cnt - cur)
        pc = jnp.concatenate(halves, axis=1).astype(BF16)
        yb = (jnp.dot(pc, pw_ref[...], preferred_element_type=F32) + pb_ref[...]) * ps_ref[...]
        yabd_ref[r0:r0 + ROW_CHUNK, W_GROUP:2 * W_GROUP] = yb.astype(BF16)
    pbuf[0:POOL_HIST, :] = pbuf[tm:tm + POOL_HIST, :]

    sc = proj(6 * W_GROUP, 9 * W_GROUP)
    zbuf[SCONV_HIST:SCONV_HIST + tm, :] = sc[:, W_GROUP:2 * W_GROUP] * sc[:, 2 * W_GROUP:]
    conv = None
    for j in range(SCONV_KERNEL):
        off = SCONV_HIST - (SCONV_KERNEL - 1) + j
        term = sw_ref[j:j + 1, :] * zbuf[off:off + tm, :]
        conv = term if conv is None else conv + term
    yabd_ref[:, 2 * W_GROUP:3 * W_GROUP] = (sc[:, :W_GROUP] * conv).astype(BF16)
    zbuf[0:SCONV_HIST, :] = zbuf[tm:tm + SCONV_HIST, :]

    qk = proj(3 * W_GROUP, 5 * W_GROUP)
    tab = rope_ref[...]
    cos = jnp.concatenate([tab[:, 0:128]] * 4, axis=1)
    sin_lo = jnp.concatenate([tab[:, 128:256]] * 4, axis=1)
    sin_hi = jnp.concatenate([tab[:, 256:384]] * 4, axis=1)
    half = DIFF_HEAD_DIM // 2
    width = 2 * W_GROUP
    qk = qk * cos + pltpu.roll(qk, width - half, axis=1) * sin_lo + pltpu.roll(qk, half, axis=1) * sin_hi
    q = qk[:, :W_GROUP] * (DIFF_HEAD_DIM ** -0.5 * LOG2E)
    qt_ref[...] = q.T.astype(BF16)
    k_ref[...] = qk[:, W_GROUP:].astype(BF16)
    vt = proj(5 * W_GROUP, 6 * W_GROUP).T.astype(BF16)
    for hd in range(DIFF_HEADS):
        r0 = hd * V_AUG_ROWS
        vt_ref[r0:r0 + DIFF_V_DIM, :] = vt[hd * DIFF_V_DIM:(hd + 1) * DIFF_V_DIM, :]
        vt_ref[r0 + DIFF_V_DIM:r0 + V_AUG_ROWS, :] = jnp.ones((V_AUG_ROWS - DIFF_V_DIM, tm), BF16)


def _inproj_mix(h, layer, prm, ple=None):
    bsz, seq, d = h.shape
    tm = min(SEQ_TILE, seq)
    assert seq % tm == 0 and tm % ROW_CHUNK == 0 and tm >= CONF_HIST
    tok = lambda w: pl.BlockSpec((None, tm, w), lambda b, s: (b, s, 0))
    tok_t = lambda rows: pl.BlockSpec((None, rows, tm), lambda b, s: (b, 0, s))
    args, in_specs = [h], [tok(d)]
    out_specs, out_shape = [], []
    if ple is not None:
        y, prev = ple
        pd = prm["p"].shape[-1]
        stacked = [prm[n] for n in ("ple_norm", "ple_wg", "ple_bg", "ple_wp")]
        args += [y, prm["p"]] + stacked
        in_specs += [tok(d), pl.BlockSpec((None, None, tm, pd), lambda b, s: (prev, b, s, 0))]
        in_specs += [_layer_spec(a, prev) for a in stacked]
        out_specs.append(tok(d))
        out_shape.append(jax.ShapeDtypeStruct((bsz, seq, d), F32))
    stacked = [prm[n] for n in ("mix_norm", "w_in")]
    args += stacked + [prm["rope"]]
    in_specs += [_layer_spec(a, layer) for a in stacked] + [pl.BlockSpec((tm, 384), lambda b, s: (s, 0))]
    stacked = [prm[n] for n in ("conf_w", "conf_b", "ln_g", "ln_b", "pool_bd", "pool_b", "pool_s", "sconv_w")]
    args += stacked
    in_specs += [_layer_spec(a, layer) for a in stacked]
    out_specs += [tok(3 * W_GROUP), tok_t(W_GROUP), tok(W_GROUP), tok_t(DIFF_HEADS * V_AUG_ROWS)]
    out_shape += [
        jax.ShapeDtypeStruct((bsz, seq, 3 * W_GROUP), BF16),
        jax.ShapeDtypeStruct((bsz, W_GROUP, seq), BF16),
        jax.ShapeDtypeStruct((bsz, seq, W_GROUP), BF16),
        jax.ShapeDtypeStruct((bsz, DIFF_HEADS * V_AUG_ROWS, seq), BF16),
    ]
    return pl.pallas_call(
        functools.partial(_inproj_mix_kernel, tm=tm, fused_ple=ple is not None),
        grid=(bsz, seq // tm),
        in_specs=in_specs,
        out_specs=out_specs,
        out_shape=out_shape,
        scratch_shapes=[
            pltpu.VMEM((tm + CONF_HIST, W_GROUP), F32),
            pltpu.VMEM((tm + POOL_HIST, W_GROUP), F32),
            pltpu.VMEM((tm + SCONV_HIST, W_GROUP), F32),
            pltpu.VMEM((SUBLANES - 1, tm + CONF_HIST - SUBLANES, W_GROUP), F32),
        ],
        compiler_params=_cparams(("arbitrary", "arbitrary")),
        name="inproj_mix",
    )(*args)


def _diff_attn_kernel(qt_ref, k_ref, vt_ref, lam_ref, g_ref, o_ref,
                      qm_ref, s0_ref, s1_ref, p0_ref, p1_ref, al0_ref, al1_ref, m_ref, l_ref, acc_ref,
                      *, t, nq, lam_init):
    qi = pl.program_id(1)
    nhc = 2 * DIFF_HEADS
    tq = nq * t

    qm_ref[...] = jnp.zeros(qm_ref.shape, BF16)
    for j in range(nhc):
        lo = j * DIFF_HEAD_DIM
        qm_ref[lo:lo + DIFF_HEAD_DIM, j * tq:(j + 1) * tq] = qt_ref[lo:lo + DIFF_HEAD_DIM, :]
    m_ref[...] = jnp.full(m_ref.shape, NEG_BIG, F32)
    l_ref[...] = jnp.zeros(l_ref.shape, F32)
    acc_ref[...] = jnp.zeros(acc_ref.shape, F32)
    p1_ref[...] = jnp.zeros(p1_ref.shape, BF16)
    al1_ref[...] = jnp.ones(al1_ref.shape, F32)

    def scores(kt, s_ref):
        k0 = pl.multiple_of(kt * t, t)
        s_ref[...] = jnp.dot(k_ref[pl.ds(k0, t), :], qm_ref[...], preferred_element_type=F32)

    def softmax(s_ref, p_ref, al_ref, diag):
        if diag is not None:
            above = (lax.broadcasted_iota(jnp.int32, (t, t), 0) > lax.broadcasted_iota(jnp.int32, (t, t), 1))
        for j in range(nhc):
            for r in range(nq):
                cols = slice(j * tq + r * t, j * tq + (r + 1) * t)
                qs = slice(r * t, (r + 1) * t)
                if diag is not None and r < diag:
                    p_ref[:, cols] = jnp.zeros((t, t), BF16)
                    al_ref[j, :, qs] = jnp.ones((1, t), F32)
                    continue
                sc = s_ref[:, cols]
                if diag is not None and r == diag:
                    sc = jnp.where(above, NEG_BIG, sc)
                m_old = m_ref[j, :, qs]
                m_new = jnp.maximum(m_old, jnp.max(sc, axis=0, keepdims=True))
                al_ref[j, :, qs] = jnp.exp2(m_old - m_new)
                m_ref[j, :, qs] = m_new
                p_ref[:, cols] = jnp.exp2(sc - m_new).astype(BF16)

    def weighted_values(p_ref, al_ref, kt):
        k0 = pl.multiple_of(jnp.maximum(kt, 0) * t, t)
        for hd in range(DIFF_HEADS):
            vth = vt_ref[hd * V_AUG_ROWS:(hd + 1) * V_AUG_ROWS, pl.ds(k0, t)]
            o = jnp.dot(vth, p_ref[:, 2 * hd * tq:(2 * hd + 2) * tq], preferred_element_type=F32)
            for c in range(2):
                j = 2 * hd + c
                alpha = al_ref[j]
                acc_ref[j] = alpha * acc_ref[j] + o[0:DIFF_V_DIM, c * tq:(c + 1) * tq]
                l_ref[j] = alpha * l_ref[j] + o[DIFF_V_DIM:DIFF_V_DIM + 1, c * tq:(c + 1) * tq]

    scores(0, s0_ref)

    def pair(i, carry):
        kt = 2 * i
        scores(kt + 1, s1_ref)
        softmax(s0_ref, p0_ref, al0_ref, None)
        weighted_values(p1_ref, al1_ref, kt - 1)
        scores(kt + 2, s0_ref)
        softmax(s1_ref, p1_ref, al1_ref, None)
        weighted_values(p0_ref, al0_ref, kt)
        return carry

    lax.fori_loop(0, (nq // 2) * qi, pair, 0)

    kd = nq * qi
    bufs = ((s0_ref, p0_ref, al0_ref), (s1_ref, p1_ref, al1_ref))
    for dg in range(nq):
        s_cur, p_cur, al_cur = bufs[dg % 2]
        s_nxt, p_prev, al_prev = bufs[(dg + 1) % 2]
        if dg + 1 < nq:
            scores(kd + dg + 1, s_nxt)
        softmax(s_cur, p_cur, al_cur, dg)
        weighted_values(p_prev, al_prev, kd + dg - 1)
    weighted_values(*bufs[(nq - 1) % 2][1:], kd + nq - 1)

    lp = lam_ref[...]
    lam = (jnp.exp(jnp.sum(lp[0:1] * lp[1:2], axis=-1, keepdims=True))
           - jnp.exp(jnp.sum(lp[2:3] * lp[3:4], axis=-1, keepdims=True)) + lam_init)
    outs = []
    for hd in range(DIFF_HEADS):
        o1 = acc_ref[2 * hd] / l_ref[2 * hd]
        o2 = acc_ref[2 * hd + 1] / l_ref[2 * hd + 1]
        o = o1 - lam * o2
        ms = jnp.mean(o * o, axis=0, keepdims=True)
        outs.append(o * lax.rsqrt(ms + EPS) * g_ref[...] * (1.0 - lam_init))
    o_ref[...] = jnp.concatenate(outs, axis=0).T.astype(BF16)


def _diff_attn(qt, k, vt, layer, prm, lam_init):
    bsz, seq, _ = k.shape
    t, nq = ATT_TILE, ATT_Q_BLOCKS
    tq = nq * t
    assert seq % tq == 0 and nq % 2 == 0
    nhc = 2 * DIFF_HEADS
    return pl.pallas_call(
        functools.partial(_diff_attn_kernel, t=t, nq=nq, lam_init=lam_init),
        grid=(bsz, seq // tq),
        in_specs=[
            pl.BlockSpec((None, W_GROUP, tq), lambda b, q: (b, 0, q)),
            pl.BlockSpec((None, seq, W_GROUP), lambda b, q: (b, 0, 0)),
            pl.BlockSpec((None, DIFF_HEADS * V_AUG_ROWS, seq), lambda b, q: (b, 0, 0)),
            _layer_spec(prm["lam"], layer),
            _layer_spec(prm["subln_g"], layer),
        ],
        out_specs=pl.BlockSpec((None, tq, W_GROUP), lambda b, q: (b, q, 0)),
        out_shape=jax.ShapeDtypeStruct((bsz, seq, W_GROUP), BF16),
        scratch_shapes=[
            pltpu.VMEM((W_GROUP, nhc * tq), BF16),
            pltpu.VMEM((t, nhc * tq), F32), pltpu.VMEM((t, nhc * tq), F32),
            pltpu.VMEM((t, nhc * tq), BF16), pltpu.VMEM((t, nhc * tq), BF16),
            pltpu.VMEM((nhc, 1, tq), F32), pltpu.VMEM((nhc, 1, tq), F32),
            pltpu.VMEM((nhc, 1, tq), F32),
            pltpu.VMEM((nhc, 1, tq), F32),
            pltpu.VMEM((nhc, DIFF_V_DIM, tq), F32),
        ],
        compiler_params=_cparams(("arbitrary", "arbitrary")),
        name="diff_attn",
    )(qt, k, vt, prm["lam"], prm["subln_g"])


def _outproj_router_kernel(h_ref, yabd_ref, yc_ref, wo_ref, g_ref, wr_ref, br_ref,
                           h1_ref, xs_ref, meta_ref, cnt_ref, rt_ref, ct_ref, oh_ref, carry_ref, *, tm):
    d = h_ref.shape[1]

    @pl.when(pl.program_id(0) == 0)
    def _():
        carry_ref[...] = jnp.zeros(carry_ref.shape, F32)

    wg2 = 2 * W_GROUP
    h1 = (h_ref[...]
          + jnp.dot(yabd_ref[:, 0:wg2], wo_ref[0:wg2, :], preferred_element_type=F32)
          + jnp.dot(yc_ref[...], wo_ref[wg2:wg2 + W_GROUP, :], preferred_element_type=F32)
          + jnp.dot(yabd_ref[:, wg2:], wo_ref[wg2 + W_GROUP:, :], preferred_element_type=F32))
    h1_ref[...] = h1
    xn = _rms_rows(h1, g_ref[...])
    x_hi = xn.astype(BF16)
    x_lo = (xn - x_hi.astype(F32)).astype(BF16)
    xs_ref[:, 0:d] = xn
    r = (jnp.dot(x_hi, wr_ref[...], preferred_element_type=F32)
         + jnp.dot(x_lo, wr_ref[...], preferred_element_type=F32) + br_ref[...])
    rt_ref[...] = r.T

    def row(i):
        return rt_ref[i:i + 1, :] + rt_ref[ROUTER_LO + i:ROUTER_LO + i + 1, :]

    def first_hits(vals, target):
        hits, free = [], None
        for v in vals:
            f = jnp.where(v == target, 1.0, 0.0)
            hits.append(f if free is None else f * free)
            free = (1.0 - f) if free is None else free * (1.0 - f)
        return hits

    add = lambda a, b: a + b
    gl = [row(i) for i in range(N_GROUPS)]
    el = [row(8 + e) for e in range(N_EXPERTS)]
    gmax = functools.reduce(jnp.maximum, gl)
    gsel = first_hits(gl, gmax)
    pg_c = 1.0 / functools.reduce(add, [jnp.exp(v - gmax) for v in gl])
    elc = []
    for kk in range(EXPERTS_PER_GROUP):
        v = gsel[0] * el[kk]
        for g in range(1, N_GROUPS):
            v = v + gsel[g] * el[g * EXPERTS_PER_GROUP + kk]
        elc.append(v)
    emax = functools.reduce(jnp.maximum, elc)
    ex = [jnp.exp(v - emax) for v in elc]
    se = functools.reduce(add, ex)
    pe = [v / se for v in ex]
    p1 = functools.reduce(jnp.maximum, pe)
    t1 = first_hits(pe, p1)
    rest = [jnp.where(t > 0.0, -1.0, v) for t, v in zip(t1, pe)]
    p2 = functools.reduce(jnp.maximum, rest)
    t2 = first_hits(rest, p2)
    den = p1 + p2
    w1 = pg_c * (p1 / den)
    w2 = pg_c * (p2 / den)
    wk = [a * w1 + b * w2 for a, b in zip(t1, t2)]
    sel = [a + b for a, b in zip(t1, t2)]
    lower = first_hits(sel, 1.0)
    c_a = functools.reduce(add, [f * w for f, w in zip(lower, wk)])
    c_b = functools.reduce(add, [(s - f) * w for s, f, w in zip(sel, lower, wk)])
    pair = (sel[0] * sel[2] + 2.0 * sel[0] * sel[3] + 3.0 * sel[1] * sel[2]
            + 4.0 * sel[1] * sel[3] + 5.0 * sel[2] * sel[3])
    bucket = 6.0 * (gsel[1] + 2.0 * gsel[2] + 3.0 * gsel[3]) + pair

    ct_ref[...] = jnp.zeros(ct_ref.shape, F32)
    ct_ref[0:1, :] = c_a
    ct_ref[1:2, :] = c_b
    xs_ref[:, d:d + ROUTER_LANES] = ct_ref[...].T

    for b in range(N_BUCKET_ROWS):
        oh_ref[b:b + 1, :] = jnp.where(bucket == float(b), 1.0, 0.0)
    oh = oh_ref[...]
    upper = (lax.broadcasted_iota(jnp.int32, (tm, tm), 0) <= lax.broadcasted_iota(jnp.int32, (tm, tm), 1))
    cum = jnp.dot(oh.astype(BF16), jnp.where(upper, 1.0, 0.0).astype(BF16), preferred_element_type=F32)
    carry = carry_ref[...]
    rank = jnp.sum(oh * (cum - 1.0 + carry), axis=0, keepdims=True)
    carry_new = carry + jnp.sum(oh, axis=1, keepdims=True)
    carry_ref[...] = carry_new
    cnt_ref[...] = carry_new[:, 0:ROUTER_LANES]
    meta_ref[...] = jnp.zeros(meta_ref.shape, jnp.int32)
    meta_ref[0:1, :] = bucket.astype(jnp.int32)
    meta_ref[1:2, :] = rank.astype(jnp.int32)


def _outproj_router(h2d, yabd, yc, layer, prm):
    tkn, d = h2d.shape
    tm = min(TOK_TILE, tkn)
    assert tkn % tm == 0
    row = lambda w: pl.BlockSpec((tm, w), lambda i: (i, 0))
    full = lambda shape: pl.BlockSpec(shape, lambda i: (0,) * len(shape))
    stacked = [prm[n] for n in ("w_out", "ffn_norm", "router_w", "router_b")]
    return pl.pallas_call(
        functools.partial(_outproj_router_kernel, tm=tm),
        grid=(tkn // tm,),
        in_specs=[row(d), row(3 * W_GROUP), row(W_GROUP)] + [_layer_spec(a, layer) for a in stacked],
        out_specs=[row(d), row(d + ROUTER_LANES),
                   pl.BlockSpec((None, 8, tm), lambda i: (i, 0, 0)),
                   full((N_BUCKET_ROWS, ROUTER_LANES))],
        out_shape=[jax.ShapeDtypeStruct((tkn, d), F32),
                   jax.ShapeDtypeStruct((tkn, d + ROUTER_LANES), F32),
                   jax.ShapeDtypeStruct((tkn // tm, 8, tm), jnp.int32),
                   jax.ShapeDtypeStruct((N_BUCKET_ROWS, ROUTER_LANES), F32)],
        scratch_shapes=[pltpu.VMEM((ROUTER_LANES, tm), F32), pltpu.VMEM((ROUTER_LANES, tm), F32),
                        pltpu.VMEM((N_BUCKET_ROWS, tm), F32), pltpu.VMEM((N_BUCKET_ROWS, tm), F32)],
        compiler_params=_cparams(("arbitrary",)),
        name="outproj_router",
    )(h2d, yabd, yc, *stacked)


def _moe_kernel(vt_ref, vea_ref, veb_ref, vlo_ref, vhi_ref, vfl_ref,
                tok_ref, tokn_ref, xs_hbm, wga_ref, wgb_ref, wua_ref, wub_ref, wda_ref, wdb_ref,
                y_hbm, xbuf, ybuf, gsem, ssem, *, tm, n_tiles, n_visits):
    v = pl.program_id(0)
    tile = vt_ref[v]
    flags = vfl_ref[v]
    first = (flags & 1) != 0
    last = (flags & 2) != 0
    valid = (flags & 4) != 0
    slot = tile % 2
    d = y_hbm.shape[1]

    def row_gather(ids_ref, s, k, u):
        return pltpu.make_async_copy(xs_hbm.at[pl.ds(ids_ref[0, k * SUBLANES + u], 1), :],
                                     xbuf.at[s, k, pl.ds(u, 1), :], gsem.at[s])

    def row_scatter(ids_ref, s, k, u):
        return pltpu.make_async_copy(ybuf.at[s, k, pl.ds(u, 1), :],
                                     y_hbm.at[pl.ds(ids_ref[0, k * SUBLANES + u], 1), :], ssem.at[s])

    def start_all(mk, ids_ref, s):
        def body(k, c):
            for u in range(SUBLANES):
                mk(ids_ref, s, k, u).start(priority=u % 2)
            return c
        lax.fori_loop(0, tm // SUBLANES, body, 0)

    def wait_gather(s):
        pltpu.make_async_copy(xbuf.at[s], xbuf.at[s], gsem.at[s]).wait()

    def wait_scatter(s):
        pltpu.make_async_copy(ybuf.at[s], ybuf.at[s], ssem.at[s]).wait()

    @pl.when(v == 0)
    def _():
        start_all(row_gather, tok_ref, 0)

    @pl.when(jnp.logical_and(first, valid))
    def _():
        wait_gather(slot)

        @pl.when(tile + 1 < n_tiles)
        def _():
            start_all(row_gather, tokn_ref, 1 - slot)

        @pl.when(tile >= 2)
        def _():
            wait_scatter(slot)

    @pl.when(valid)
    def _():
        rows = tile * tm + lax.broadcasted_iota(jnp.int32, (tm, 1), 0)
        inb = jnp.where(jnp.logical_and(rows >= vlo_ref[v], rows < vhi_ref[v]), 1.0, 0.0)
        xt = xbuf[slot].reshape(tm, xs_hbm.shape[1])
        x = xt[:, 0:d].astype(BF16)
        c_a = xt[:, d:d + 1] * inb
        c_b = xt[:, d + 1:d + 2] * inb

        def expert(wg_ref, wu_ref, wd_ref, c):
            gate = jnp.dot(x, wg_ref[...], preferred_element_type=F32)
            up = jnp.dot(x, wu_ref[...], preferred_element_type=F32)
            hdn = gate * _sigmoid(gate) * up * c
            return jnp.dot(hdn.astype(BF16), wd_ref[...], preferred_element_type=F32)

        y = expert(wga_ref, wua_ref, wda_ref, c_a) + expert(wgb_ref, wub_ref, wdb_ref, c_b)

        y = y.reshape(tm // SUBLANES, SUBLANES, d)

        @pl.when(first)
        def _():
            ybuf[slot] = y

        @pl.when(jnp.logical_not(first))
        def _():
            ybuf[slot] = ybuf[slot] + y

    @pl.when(jnp.logical_and(last, valid))
    def _():
        start_all(row_scatter, tok_ref, slot)

    @pl.when(v == n_visits - 1)
    def _():
        wait_scatter(0)
        wait_scatter(1)


def _moe(xs, tok_of_slot, sched, layer, prm):
    tkn, dx = xs.shape
    d = dx - ROUTER_LANES
    tm = MOE_TILE
    assert tkn % tm == 0 and tkn // tm >= 2
    n_tiles = tkn // tm
    n_visits = sched[0].shape[0]
    tok3 = tok_of_slot.reshape(n_tiles, 1, tm)
    wspec = lambda shape, which: pl.BlockSpec(
        (None, None) + shape, lambda v, vt, vea, veb, vlo, vhi, vfl: (layer, (vea, veb)[which][v], 0, 0))
    grid_spec = pltpu.PrefetchScalarGridSpec(
        num_scalar_prefetch=6,
        grid=(n_visits,),
        in_specs=[
            pl.BlockSpec((None, 1, tm), lambda v, vt, *_: (vt[v], 0, 0), memory_space=pltpu.SMEM),
            pl.BlockSpec((None, 1, tm), lambda v, vt, *_: (jnp.minimum(vt[v] + 1, n_tiles - 1), 0, 0),
                         memory_space=pltpu.SMEM),
            pl.BlockSpec(memory_space=pl.ANY),
            wspec((d, D_EXPERT), 0), wspec((d, D_EXPERT), 1),
            wspec((d, D_EXPERT), 0), wspec((d, D_EXPERT), 1),
            wspec((D_EXPERT, d), 0), wspec((D_EXPERT, d), 1),
        ],
        out_specs=pl.BlockSpec(memory_space=pl.ANY),
        scratch_shapes=[pltpu.VMEM((2, tm // SUBLANES, SUBLANES, dx), F32),
                        pltpu.VMEM((2, tm // SUBLANES, SUBLANES, d), F32),
                        pltpu.SemaphoreType.DMA((2,)), pltpu.SemaphoreType.DMA((2,))],
    )
    return pl.pallas_call(
        functools.partial(_moe_kernel, tm=tm, n_tiles=n_tiles, n_visits=n_visits),
        grid_spec=grid_spec,
        out_shape=jax.ShapeDtypeStruct((tkn, d), F32),
        compiler_params=_cparams(("arbitrary",)),
        name="moe_experts",
    )(*sched, tok3, tok3, xs, prm["exp_wg"], prm["exp_wg"], prm["exp_wu"], prm["exp_wu"], prm["exp_wd"], prm["exp_wd"])


def _moe_schedule(bucket, rank, counts, tm):
    tkn = bucket.size
    n_tiles = tkn // tm
    n_visits = n_tiles + N_BUCKETS
    ends = jnp.cumsum(counts)
    offs = ends - counts
    slot = rank
    for bb in range(N_BUCKETS):
        slot = slot + jnp.where(bucket == bb, offs[bb], 0)
    _, tok_of_slot = lax.sort_key_val(slot.reshape(tkn), jnp.arange(tkn, dtype=jnp.int32))
    t_first = offs // tm
    t_last = jnp.maximum(ends - 1, 0) // tm
    nv = jnp.where(counts > 0, t_last - t_first + 1, 0)
    v_end = jnp.cumsum(nv)
    v_start = v_end - nv
    total = v_end[-1]
    v = jnp.arange(n_visits, dtype=jnp.int32)
    vc = jnp.minimum(v, total - 1)
    b = jnp.sum((v_end[None, :] <= vc[:, None]).astype(jnp.int32), axis=1)
    onehot = b[:, None] == jnp.arange(N_BUCKETS, dtype=jnp.int32)[None, :]
    pick = lambda table: jnp.sum(jnp.where(onehot, table[None, :], 0), axis=1)
    tile = (pick(t_first) + vc - pick(v_start)).astype(jnp.int32)
    valid = v < total
    prev_tile = jnp.concatenate([jnp.full((1,), -1, jnp.int32), tile[:-1]])
    next_tile = jnp.concatenate([tile[1:], jnp.full((1,), -1, jnp.int32)])
    first = tile != prev_tile
    last = jnp.logical_or(tile != next_tile, v == total - 1)
    flags = (first.astype(jnp.int32) + 2 * last.astype(jnp.int32) + 4) * valid.astype(jnp.int32)
    g, pr = b // 6, b % 6
    lo_e = (pr >= 3).astype(jnp.int32) + (pr >= 5).astype(jnp.int32)
    hi_e = pr + 1 - 2 * (pr >= 3).astype(jnp.int32) - (pr >= 5).astype(jnp.int32)
    ea = g * EXPERTS_PER_GROUP + lo_e
    eb = g * EXPERTS_PER_GROUP + hi_e
    sched = (tile, ea.astype(jnp.int32), eb.astype(jnp.int32),
             pick(offs).astype(jnp.int32), pick(ends).astype(jnp.int32), flags.astype(jnp.int32))
    return tok_of_slot, sched


def _ple_final_kernel(h_ref, y_ref, p_ref, g_ref, wg_ref, bg_ref, wp_ref, fg_ref, o_ref):
    out = _ple_update(h_ref[...], y_ref[...], p_ref[...], g_ref[...], wg_ref[...], bg_ref[...], wp_ref[...])
    o_ref[...] = _rms_rows(out, fg_ref[...])


def _ple_final(h2d, y2d, layer, prm):
    tkn, d = h2d.shape
    p4 = prm["p"]
    _, bsz, seq, pd = p4.shape
    tm = min(FINAL_TILE, seq)
    assert seq % tm == 0 and tkn == bsz * seq
    per_row = seq // tm
    stacked = [prm[n] for n in ("ple_norm", "ple_wg", "ple_bg", "ple_wp")]
    return pl.pallas_call(
        _ple_final_kernel,
        grid=(tkn // tm,),
        in_specs=[pl.BlockSpec((tm, d), lambda i: (i, 0)), pl.BlockSpec((tm, d), lambda i: (i, 0)),
                  pl.BlockSpec((None, None, tm, pd), lambda i: (layer, i // per_row, i % per_row, 0))]
                 + [_layer_spec(a, layer) for a in stacked]
                 + [pl.BlockSpec((1, d), lambda i: (0, 0))],
        out_specs=pl.BlockSpec((tm, d), lambda i: (i, 0)),
        out_shape=jax.ShapeDtypeStruct((tkn, d), F32),
        compiler_params=_cparams(("arbitrary",)),
        name="ple_final",
    )(h2d, y2d, p4, *stacked, prm["final_norm"])


def _rope_table(seq):
    pos = np.arange(seq, dtype=np.float64)
    inv = ROPE_THETA ** (-np.arange(0, DIFF_HEAD_DIM, 2, dtype=np.float64) / DIFF_HEAD_DIM)
    ang = pos[:, None] * inv[None, :]
    ang = np.concatenate([ang, ang], axis=-1)
    cos, sin = np.cos(ang), np.sin(ang)
    lo = np.arange(DIFF_HEAD_DIM) < DIFF_HEAD_DIM // 2
    sin_lo = np.where(lo, -sin, 0.0)
    sin_hi = np.where(lo, 0.0, sin)
    rep = 128 // DIFF_HEAD_DIM
    table = np.concatenate([np.tile(t, (1, rep)) for t in (cos, sin_lo, sin_hi)], axis=1)
    return jnp.asarray(table.astype(np.float32))


def _block_diag(w):
    nl, g, c, dd = w.shape
    out = jnp.zeros((nl, g * c, g * dd), w.dtype)
    for i in range(g):
        out = out.at[:, i * c:(i + 1) * c, i * dd:(i + 1) * dd].set(w[:, i])
    return out


def _router_weights(wg, bg, we, be):
    nl, d, _ = wg.shape
    w = jnp.zeros((nl, d, ROUTER_LANES), F32)
    w = w.at[:, :, 0:N_GROUPS].set(wg).at[:, :, 8:8 + N_EXPERTS].set(we)
    w_hi = w.astype(BF16)
    w_lo = (w - w_hi.astype(F32)).astype(BF16)
    wr = w_hi.at[:, :, ROUTER_LO:ROUTER_LO + 8 + N_EXPERTS].set(w_lo[:, :, 0:8 + N_EXPERTS])
    br = jnp.zeros((nl, 1, ROUTER_LANES), F32)
    br = br.at[:, 0, 0:N_GROUPS].set(bg).at[:, 0, 8:8 + N_EXPERTS].set(be)
    return wr, br


def kernel(x, p, mix_norm, w_in, conf_conv_w, conf_conv_b, conf_ln_g, conf_ln_b, pool_w, pool_b, pool_scale, diff_lam_q1, diff_lam_k1, diff_lam_q2, diff_lam_k2, diff_subln_g, sconv_w, w_out, ffn_norm, router_group_w, router_group_b, router_expert_w, router_expert_b, expert_w_gate, expert_w_up, expert_w_down, ple_norm, ple_gate_w, ple_gate_b, ple_proj, final_norm):
    bsz, seq, d = x.shape
    depth = w_in.shape[0]
    tkn = bsz * seq
    att_t = ATT_TILE * ATT_Q_BLOCKS
    rows = lambda v: v.reshape(depth, 1, -1).astype(F32)
    router_w, router_b = _router_weights(router_group_w, router_group_b, router_expert_w, router_expert_b)
    prm = dict(
        p=p, rope=_rope_table(seq),
        mix_norm=rows(mix_norm), w_in=w_in.astype(BF16),
        conf_w=conf_conv_w, conf_b=rows(conf_conv_b), ln_g=rows(conf_ln_g), ln_b=rows(conf_ln_b),
        pool_bd=_block_diag(pool_w).astype(BF16), pool_b=rows(pool_b), pool_s=rows(pool_scale), sconv_w=sconv_w,
        lam=jnp.stack([diff_lam_q1, diff_lam_k1, diff_lam_q2, diff_lam_k2], axis=1).astype(F32),
        subln_g=jnp.broadcast_to(diff_subln_g.astype(F32)[:, :, None], (depth, DIFF_V_DIM, att_t)),
        w_out=w_out.astype(BF16), ffn_norm=rows(ffn_norm), router_w=router_w, router_b=router_b,
        exp_wg=expert_w_gate.astype(BF16), exp_wu=expert_w_up.astype(BF16), exp_wd=expert_w_down.astype(BF16),
        ple_norm=rows(ple_norm), ple_wg=ple_gate_w.astype(BF16), ple_bg=rows(ple_gate_b),
        ple_wp=ple_proj.astype(BF16), final_norm=final_norm.reshape(1, -1).astype(F32),
    )

    h, pending = x, None
    for i in range(depth):
        lam_init = 0.8 - 0.6 * math.exp(-0.3 * i)
        if pending is None:
            yabd, qt, k, vt = _inproj_mix(h, i, prm)
        else:
            h, yabd, qt, k, vt = _inproj_mix(h, i, prm, ple=pending)
        yc = _diff_attn(qt, k, vt, i, prm, lam_init)
        h1, xs, meta, cnt = _outproj_router(
            h.reshape(tkn, d), yabd.reshape(tkn, 3 * W_GROUP), yc.reshape(tkn, W_GROUP), i, prm)
        tok_of_slot, sched = _moe_schedule(
            meta[:, 0, :], meta[:, 1, :], cnt[:N_BUCKETS, 0].astype(jnp.int32), MOE_TILE)
        y = _moe(xs, tok_of_slot, sched, i, prm)
        if i == depth - 1:
            return _ple_final(h1, y, i, prm).reshape(bsz, seq, d)
        h = h1.reshape(bsz, seq, d)
        pending = (y.reshape(bsz, seq, d), i)
```

```python
import functools
import math

import jax
import jax.numpy as jnp
import numpy as np
from jax import lax
from jax.experimental import pallas as pl
from jax.experimental.pallas import tpu as pltpu

F32 = jnp.float32
BF16 = jnp.bfloat16

EPS = 1e-6
ROPE_THETA = 10000.0
W_GROUP = 256
CONF_KERNEL = 31
CONF_HIST = 32
SUBLANES = 8
POOL_HIST = 16
SCONV_KERNEL = 3
SCONV_HIST = 8
DIFF_HEADS = 4
DIFF_HEAD_DIM = 32
DIFF_V_DIM = 64
V_AUG_ROWS = 80
N_GROUPS = 4
EXPERTS_PER_GROUP = 4
N_EXPERTS = 16
D_EXPERT = 256
ROUTER_LANES = 128
ROUTER_LO = 32
NEG_BIG = -1e30
LOG2E = 1.4426950408889634

VMEM_LIMIT = 48 * 1024 * 1024

SEQ_TILE = 512
ROW_CHUNK = 32
ATT_TILE = 256
ATT_Q_BLOCKS = 2
TOK_TILE = 1024
FINAL_TILE = 1024
MOE_TILE = 256
N_BUCKETS = 24
N_BUCKET_ROWS = 32


def _cparams(sem):
    return pltpu.CompilerParams(dimension_semantics=sem, vmem_limit_bytes=VMEM_LIMIT)


def _layer_spec(arr, layer):
    return pl.BlockSpec((None,) + arr.shape[1:], lambda *_: (layer,) + (0,) * (arr.ndim - 1))


def _rms_rows(x, g):
    ms = jnp.mean(x * x, axis=-1, keepdims=True)
    return x * lax.rsqrt(ms + EPS) * g


def _sigmoid(x):
    return 1.0 / (1.0 + jnp.exp(-x))


def _ple_update(h, y, p, g, wg, bg, wp):
    h = h + y
    xn = _rms_rows(h, g).astype(BF16)
    gate = _sigmoid(jnp.dot(xn, wg, preferred_element_type=F32) + bg)
    return h + gate * jnp.dot(p.astype(BF16), wp, preferred_element_type=F32)


def _inproj_mix_kernel(*refs, tm, fused_ple):
    if fused_ple:
        (h_ref, y_ref, p_ref, pg_ref, pwg_ref, pbg_ref, pwp_ref), refs = refs[:7], refs[7:]
    else:
        h_ref, refs = refs[0], refs[1:]
    (g_ref, w_ref, rope_ref, cw_ref, cb_ref, lg_ref, lb_ref, pw_ref, pb_ref, ps_ref, sw_ref), refs = refs[:11], refs[11:]
    if fused_ple:
        hout_ref, refs = refs[0], refs[1:]
    yabd_ref, qt_ref, k_ref, vt_ref, gbuf, pbuf, zbuf, gshift = refs
    s = pl.program_id(1)

    @pl.when(s == 0)
    def _():
        gbuf[0:CONF_HIST, :] = jnp.zeros((CONF_HIST, W_GROUP), F32)
        pbuf[0:POOL_HIST, :] = jnp.zeros((POOL_HIST, W_GROUP), F32)
        zbuf[0:SCONV_HIST, :] = jnp.zeros((SCONV_HIST, W_GROUP), F32)

    if fused_ple:
        h = _ple_update(h_ref[...], y_ref[...], p_ref[...], pg_ref[...], pwg_ref[...], pbg_ref[...], pwp_ref[...])
        hout_ref[...] = h
    else:
        h = h_ref[...]
    n = _rms_rows(h, g_ref[...]).astype(BF16)

    def proj(lo, hi):
        return jnp.dot(n, w_ref[:, lo:hi], preferred_element_type=F32)

    a = proj(0, 2 * W_GROUP)
    gbuf[CONF_HIST:CONF_HIST + tm, :] = a[:, :W_GROUP] * _sigmoid(a[:, W_GROUP:])
    span = tm + CONF_HIST - SUBLANES
    for sh in range(1, SUBLANES):
        gshift[sh - 1, 0:span, :] = gbuf[sh:sh + span, :]
    base = CONF_HIST - (CONF_KERNEL - 1)
    for c in range(tm // ROW_CHUNK):
        r0 = c * ROW_CHUNK
        acc = jnp.broadcast_to(cb_ref[...], (ROW_CHUNK, W_GROUP))
        for j in range(CONF_KERNEL):
            sh = (base + j) % SUBLANES
            q0 = r0 + base + j - sh
            win = gbuf[q0:q0 + ROW_CHUNK, :] if sh == 0 else gshift[sh - 1, q0:q0 + ROW_CHUNK, :]
            acc = acc + cw_ref[j:j + 1, :] * win
        mu = jnp.mean(acc, axis=-1, keepdims=True)
        d = acc - mu
        var = jnp.mean(d * d, axis=-1, keepdims=True)
        y = d * lax.rsqrt(var + EPS) * lg_ref[...] + lb_ref[...]
        yabd_ref[r0:r0 + ROW_CHUNK, 0:W_GROUP] = (y * _sigmoid(y)).astype(BF16)
    gbuf[0:CONF_HIST, :] = gbuf[tm:tm + CONF_HIST, :]

    pbuf[POOL_HIST:POOL_HIST + tm, :] = proj(2 * W_GROUP, 3 * W_GROUP)
    lane = lax.broadcasted_iota(jnp.int32, (ROW_CHUNK, 128), 1)
    first = lane < 64
    for c in range(tm // ROW_CHUNK):
        r0 = c * ROW_CHUNK
        tpos = (s * tm + r0 + 1 + lax.broadcasted_iota(jnp.int32, (ROW_CHUNK, 128), 0)).astype(F32)
        halves = []
        for half, (w_small, w_big) in enumerate(((2, 4), (8, 16))):
            l0 = half * 128

            def ld(j):
                return pbuf[r0 + POOL_HIST - j:r0 + POOL_HIST - j + ROW_CHUNK, l0:l0 + 128]

            cur = ld(0)
            run = cur
            for j in range(1, w_small):
                run = run + ld(j)
            small = run
            for j in range(w_small, w_big):
                run = run + ld(j)
            cnt = jnp.where(first, jnp.minimum(tpos, float(w_small)), jnp.minimum(tpos, float(w_big)))
            halves.append(jnp.where(first, small, run) / cnt - cur)
        pc = jnp.concatenate(halves, axis=1).astype(BF16)
        yb = (jnp.dot(pc, pw_ref[...], preferred_element_type=F32) + pb_ref[...]) * ps_ref[...]
        yabd_ref[r0:r0 + ROW_CHUNK, W_GROUP:2 * W_GROUP] = yb.astype(BF16)
    pbuf[0:POOL_HIST, :] = pbuf[tm:tm + POOL_HIST, :]

    sc = proj(6 * W_GROUP, 9 * W_GROUP)
    zbuf[SCONV_HIST:SCONV_HIST + tm, :] = sc[:, W_GROUP:2 * W_GROUP] * sc[:, 2 * W_GROUP:]
    conv = None
    for j in range(SCONV_KERNEL):
        off = SCONV_HIST - (SCONV_KERNEL - 1) + j
        term = sw_ref[j:j + 1, :] * zbuf[off:off + tm, :]
        conv = term if conv is None else conv + term
    yabd_ref[:, 2 * W_GROUP:3 * W_GROUP] = (sc[:, :W_GROUP] * conv).astype(BF16)
    zbuf[0:SCONV_HIST, :] = zbuf[tm:tm + SCONV_HIST, :]

    qk = proj(3 * W_GROUP, 5 * W_GROUP)
    tab = rope_ref[...]
    cos = jnp.concatenate([tab[:, 0:128]] * 4, axis=1)
    sin_lo = jnp.concatenate([tab[:, 128:256]] * 4, axis=1)
    sin_hi = jnp.concatenate([tab[:, 256:384]] * 4, axis=1)
    half = DIFF_HEAD_DIM // 2
    width = 2 * W_GROUP
    qk = qk * cos + pltpu.roll(qk, width - half, axis=1) * sin_lo + pltpu.roll(qk, half, axis=1) * sin_hi
    q = qk[:, :W_GROUP] * (DIFF_HEAD_DIM ** -0.5 * LOG2E)
    qt_ref[...] = q.T.astype(BF16)
    k_ref[...] = qk[:, W_GROUP:].astype(BF16)
    vt = proj(5 * W_GROUP, 6 * W_GROUP).T.astype(BF16)
    for hd in range(DIFF_HEADS):
        r0 = hd * V_AUG_ROWS
        vt_ref[r0:r0 + DIFF_V_DIM, :] = vt[hd * DIFF_V_DIM:(hd + 1) * DIFF_V_DIM, :]
        vt_ref[r0 + DIFF_V_DIM:r0 + V_AUG_ROWS, :] = jnp.ones((V_AUG_ROWS - DIFF_V_DIM, tm), BF16)


def _inproj_mix(h, layer, prm, ple=None):
    bsz, seq, d = h.shape
    tm = min(SEQ_TILE, seq)
    assert seq % tm == 0 and tm % ROW_CHUNK == 0 and tm >= CONF_HIST
    tok = lambda w: pl.BlockSpec((None, tm, w), lambda b, s: (b, s, 0))
    tok_t = lambda rows: pl.BlockSpec((None, rows, tm), lambda b, s: (b, 0, s))
    args, in_specs = [h], [tok(d)]
    out_specs, out_shape = [], []
    if ple is not None:
        y, prev = ple
        pd = prm["p"].shape[-1]
        stacked = [prm[n] for n in ("ple_norm", "ple_wg", "ple_bg", "ple_wp")]
        args += [y, prm["p"]] + stacked
        in_specs += [tok(d), pl.BlockSpec((None, None, tm, pd), lambda b, s: (prev, b, s, 0))]
        in_specs += [_layer_spec(a, prev) for a in stacked]
        out_specs.append(tok(d))
        out_shape.append(jax.ShapeDtypeStruct((bsz, seq, d), F32))
    stacked = [prm[n] for n in ("mix_norm", "w_in")]
    args += stacked + [prm["rope"]]
    in_specs += [_layer_spec(a, layer) for a in stacked] + [pl.BlockSpec((tm, 384), lambda b, s: (s, 0))]
    stacked = [prm[n] for n in ("conf_w", "conf_b", "ln_g", "ln_b", "pool_bd", "pool_b", "pool_s", "sconv_w")]
    args += stacked
    in_specs += [_layer_spec(a, layer) for a in stacked]
    out_specs += [tok(3 * W_GROUP), tok_t(W_GROUP), tok(W_GROUP), tok_t(DIFF_HEADS * V_AUG_ROWS)]
    out_shape += [
        jax.ShapeDtypeStruct((bsz, seq, 3 * W_GROUP), BF16),
        jax.ShapeDtypeStruct((bsz, W_GROUP, seq), BF16),
        jax.ShapeDtypeStruct((bsz, seq, W_GROUP), BF16),
        jax.ShapeDtypeStruct((bsz, DIFF_HEADS * V_AUG_ROWS, seq), BF16),
    ]
    return pl.pallas_call(
        functools.partial(_inproj_mix_kernel, tm=tm, fused_ple=ple is not None),
        grid=(bsz, seq // tm),
        in_specs=in_specs,
        out_specs=out_specs,
        out_shape=out_shape,
        scratch_shapes=[
            pltpu.VMEM((tm + CONF_HIST, W_GROUP), F32),
            pltpu.VMEM((tm + POOL_HIST, W_GROUP), F32),
            pltpu.VMEM((tm + SCONV_HIST, W_GROUP), F32),
            pltpu.VMEM((SUBLANES - 1, tm + CONF_HIST - SUBLANES, W_GROUP), F32),
        ],
        compiler_params=_cparams(("arbitrary", "arbitrary")),
        name="inproj_mix",
    )(*args)


def _diff_attn_kernel(qt_ref, k_ref, vt_ref, lam_ref, g_ref, o_ref,
                      qm_ref, s0_ref, s1_ref, p0_ref, p1_ref, al0_ref, al1_ref, m_ref, l_ref, acc_ref,
                      *, t, nq, lam_init):
    qi = pl.program_id(1)
    nhc = 2 * DIFF_HEADS
    tq = nq * t

    qm_ref[...] = jnp.zeros(qm_ref.shape, BF16)
    for j in range(nhc):
        lo = j * DIFF_HEAD_DIM
        qm_ref[lo:lo + DIFF_HEAD_DIM, j * tq:(j + 1) * tq] = qt_ref[lo:lo + DIFF_HEAD_DIM, :]
    m_ref[...] = jnp.full(m_ref.shape, NEG_BIG, F32)
    l_ref[...] = jnp.zeros(l_ref.shape, F32)
    acc_ref[...] = jnp.zeros(acc_ref.shape, F32)
    p1_ref[...] = jnp.zeros(p1_ref.shape, BF16)
    al1_ref[...] = jnp.ones(al1_ref.shape, F32)

    def scores(kt, s_ref):
        k0 = pl.multiple_of(kt * t, t)
        s_ref[...] = jnp.dot(k_ref[pl.ds(k0, t), :], qm_ref[...], preferred_element_type=F32)

    def softmax(s_ref, p_ref, al_ref, diag):
        if diag is not None:
            above = (lax.broadcasted_iota(jnp.int32, (t, t), 0) > lax.broadcasted_iota(jnp.int32, (t, t), 1))
        for j in range(nhc):
            for r in range(nq):
                cols = slice(j * tq + r * t, j * tq + (r + 1) * t)
                qs = slice(r * t, (r + 1) * t)
                if diag is not None and r < diag:
                    p_ref[:, cols] = jnp.zeros((t, t), BF16)
                    al_ref[j, :, qs] = jnp.ones((1, t), F32)
                    continue
                sc = s_ref[:, cols]
                if diag is not None and r == diag:
                    sc = jnp.where(above, NEG_BIG, sc)
                m_old = m_ref[j, :, qs]
                m_new = jnp.maximum(m_old, jnp.max(sc, axis=0, keepdims=True))
                al_ref[j, :, qs] = jnp.exp2(m_old - m_new)
                m_ref[j, :, qs] = m_new
                p_ref[:, cols] = jnp.exp2(sc - m_new).astype(BF16)

    def weighted_values(p_ref, al_ref, kt):
        k0 = pl.multiple_of(jnp.maximum(kt, 0) * t, t)
        for hd in range(DIFF_HEADS):
            vth = vt_ref[hd * V_AUG_ROWS:(hd + 1) * V_AUG_ROWS, pl.ds(k0, t)]
            o = jnp.dot(vth, p_ref[:, 2 * hd * tq:(2 * hd + 2) * tq], preferred_element_type=F32)
            for c in range(2):
                j = 2 * hd + c
                alpha = al_ref[j]
                acc_ref[j] = alpha * acc_ref[j] + o[0:DIFF_V_DIM, c * tq:(c + 1) * tq]
                l_ref[j] = alpha * l_ref[j] + o[DIFF_V_DIM:DIFF_V_DIM + 1, c * tq:(c + 1) * tq]

    scores(0, s0_ref)

    def pair(i, carry):
        kt = 2 * i
        scores(kt + 1, s1_ref)
        softmax(s0_ref, p0_ref, al0_ref, None)
        weighted_values(p1_ref, al1_ref, kt - 1)
        scores(kt + 2, s0_ref)
        softmax(s1_ref, p1_ref, al1_ref, None)
        weighted_values(p0_ref, al0_ref, kt)
        return carry

    lax.fori_loop(0, (nq // 2) * qi, pair, 0)

    kd = nq * qi
    bufs = ((s0_ref, p0_ref, al0_ref), (s1_ref, p1_ref, al1_ref))
    for dg in range(nq):
        s_cur, p_cur, al_cur = bufs[dg % 2]
        s_nxt, p_prev, al_prev = bufs[(dg + 1) % 2]
        if dg + 1 < nq:
            scores(kd + dg + 1, s_nxt)
        softmax(s_cur, p_cur, al_cur, dg)
        weighted_values(p_prev, al_prev, kd + dg - 1)
    weighted_values(*bufs[(nq - 1) % 2][1:], kd + nq - 1)

    lp = lam_ref[...]
    lam = (jnp.exp(jnp.sum(lp[0:1] * lp[1:2], axis=-1, keepdims=True))
           - jnp.exp(jnp.sum(lp[2:3] * lp[3:4], axis=-1, keepdims=True)) + lam_init)
    outs = []
    for hd in range(DIFF_HEADS):
        o1 = acc_ref[2 * hd] / l_ref[2 * hd]
        o2 = acc_ref[2 * hd + 1] / l_ref[2 * hd + 1]
        o = o1 - lam * o2
        ms = jnp.mean(o * o, axis=0, keepdims=True)
        outs.append(o * lax.rsqrt(ms + EPS) * g_ref[...] * (1.0 - lam_init))
    o_ref[...] = jnp.concatenate(outs, axis=0).T.astype(BF16)


def _diff_attn(qt, k, vt, layer, prm, lam_init):
    bsz, seq, _ = k.shape
    t, nq = ATT_TILE, ATT_Q_BLOCKS
    tq = nq * t
    assert seq % tq == 0 and nq % 2 == 0
    nhc = 2 * DIFF_HEADS
    return pl.pallas_call(
        functools.partial(_diff_attn_kernel, t=t, nq=nq, lam_init=lam_init),
        grid=(bsz, seq // tq),
        in_specs=[
            pl.BlockSpec((None, W_GROUP, tq), lambda b, q: (b, 0, q)),
            pl.BlockSpec((None, seq, W_GROUP), lambda b, q: (b, 0, 0)),
            pl.BlockSpec((None, DIFF_HEADS * V_AUG_ROWS, seq), lambda b, q: (b, 0, 0)),
            _layer_spec(prm["lam"], layer),
            _layer_spec(prm["subln_g"], layer),
        ],
        out_specs=pl.BlockSpec((None, tq, W_GROUP), lambda b, q: (b, q, 0)),
        out_shape=jax.ShapeDtypeStruct((bsz, seq, W_GROUP), BF16),
        scratch_shapes=[
            pltpu.VMEM((W_GROUP, nhc * tq), BF16),
            pltpu.VMEM((t, nhc * tq), F32), pltpu.VMEM((t, nhc * tq), F32),
            pltpu.VMEM((t, nhc * tq), BF16), pltpu.VMEM((t, nhc * tq), BF16),
            pltpu.VMEM((nhc, 1, tq), F32), pltpu.VMEM((nhc, 1, tq), F32),
            pltpu.VMEM((nhc, 1, tq), F32),
            pltpu.VMEM((nhc, 1, tq), F32),
            pltpu.VMEM((nhc, DIFF_V_DIM, tq), F32),
        ],
        compiler_params=_cparams(("arbitrary", "arbitrary")),
        name="diff_attn",
    )(qt, k, vt, prm["lam"], prm["subln_g"])


def _outproj_router_kernel(h_ref, yabd_ref, yc_ref, wo_ref, g_ref, wr_ref, br_ref,
                           h1_ref, xs_ref, meta_ref, cnt_ref, rt_ref, ct_ref, oh_ref, carry_ref, *, tm):
    d = h_ref.shape[1]

    @pl.when(pl.program_id(0) == 0)
    def _():
        carry_ref[...] = jnp.zeros(carry_ref.shape, F32)

    wg2 = 2 * W_GROUP
    h1 = (h_ref[...]
          + jnp.dot(yabd_ref[:, 0:wg2], wo_ref[0:wg2, :], preferred_element_type=F32)
          + jnp.dot(yc_ref[...], wo_ref[wg2:wg2 + W_GROUP, :], preferred_element_type=F32)
          + jnp.dot(yabd_ref[:, wg2:], wo_ref[wg2 + W_GROUP:, :], preferred_element_type=F32))
    h1_ref[...] = h1
    xn = _rms_rows(h1, g_ref[...])
    x_hi = xn.astype(BF16)
    x_lo = (xn - x_hi.astype(F32)).astype(BF16)
    xs_ref[:, 0:d] = xn
    r = (jnp.dot(x_hi, wr_ref[...], preferred_element_type=F32)
         + jnp.dot(x_lo, wr_ref[...], preferred_element_type=F32) + br_ref[...])
    rt_ref[...] = r.T

    def row(i):
        return rt_ref[i:i + 1, :] + rt_ref[ROUTER_LO + i:ROUTER_LO + i + 1, :]

    def first_hits(vals, target):
        hits, free = [], None
        for v in vals:
            f = jnp.where(v == target, 1.0, 0.0)
            hits.append(f if free is None else f * free)
            free = (1.0 - f) if free is None else free * (1.0 - f)
        return hits

    add = lambda a, b: a + b
    gl = [row(i) for i in range(N_GROUPS)]
    el = [row(8 + e) for e in range(N_EXPERTS)]
    gmax = functools.reduce(jnp.maximum, gl)
    gsel = first_hits(gl, gmax)
    pg_c = 1.0 / functools.reduce(add, [jnp.exp(v - gmax) for v in gl])
    elc = []
    for kk in range(EXPERTS_PER_GROUP):
        v = gsel[0] * el[kk]
        for g in range(1, N_GROUPS):
            v = v + gsel[g] * el[g * EXPERTS_PER_GROUP + kk]
        elc.append(v)
    emax = functools.reduce(jnp.maximum, elc)
    ex = [jnp.exp(v - emax) for v in elc]
    se = functools.reduce(add, ex)
    pe = [v / se for v in ex]
    p1 = functools.reduce(jnp.maximum, pe)
    t1 = first_hits(pe, p1)
    rest = [jnp.where(t > 0.0, -1.0, v) for t, v in zip(t1, pe)]
    p2 = functools.reduce(jnp.maximum, rest)
    t2 = first_hits(rest, p2)
    den = p1 + p2
    w1 = pg_c * (p1 / den)
    w2 = pg_c * (p2 / den)
    wk = [a * w1 + b * w2 for a, b in zip(t1, t2)]
    sel = [a + b for a, b in zip(t1, t2)]
    lower = first_hits(sel, 1.0)
    c_a = functools.reduce(add, [f * w for f, w in zip(lower, wk)])
    c_b = functools.reduce(add, [(s - f) * w for s, f, w in zip(sel, lower, wk)])
    pair = (sel[0] * sel[2] + 2.0 * sel[0] * sel[3] + 3.0 * sel[1] * sel[2]
            + 4.0 * sel[1] * sel[3] + 5.0 * sel[2] * sel[3])
    bucket = 6.0 * (gsel[1] + 2.0 * gsel[2] + 3.0 * gsel[3]) + pair

    ct_ref[...] = jnp.zeros(ct_ref.shape, F32)
    ct_ref[0:1, :] = c_a
    ct_ref[1:2, :] = c_b
    xs_ref[:, d:d + ROUTER_LANES] = ct_ref[...].T

    for b in range(N_BUCKET_ROWS):
        oh_ref[b:b + 1, :] = jnp.where(bucket == float(b), 1.0, 0.0)
    oh = oh_ref[...]
    upper = (lax.broadcasted_iota(jnp.int32, (tm, tm), 0) <= lax.broadcasted_iota(jnp.int32, (tm, tm), 1))
    cum = jnp.dot(oh.astype(BF16), jnp.where(upper, 1.0, 0.0).astype(BF16), preferred_element_type=F32)
    carry = carry_ref[...]
    rank = jnp.sum(oh * (cum - 1.0 + carry), axis=0, keepdims=True)
    carry_new = carry + jnp.sum(oh, axis=1, keepdims=True)
    carry_ref[...] = carry_new
    cnt_ref[...] = carry_new[:, 0:ROUTER_LANES]
    meta_ref[...] = jnp.zeros(meta_ref.shape, jnp.int32)
    meta_ref[0:1, :] = bucket.astype(jnp.int32)
    meta_ref[1:2, :] = rank.astype(jnp.int32)


def _outproj_router(h2d, yabd, yc, layer, prm):
    tkn, d = h2d.shape
    tm = min(TOK_TILE, tkn)
    assert tkn % tm == 0
    row = lambda w: pl.BlockSpec((tm, w), lambda i: (i, 0))
    full = lambda shape: pl.BlockSpec(shape, lambda i: (0,) * len(shape))
    stacked = [prm[n] for n in ("w_out", "ffn_norm", "router_w", "router_b")]
    return pl.pallas_call(
        functools.partial(_outproj_router_kernel, tm=tm),
        grid=(tkn // tm,),
        in_specs=[row(d), row(3 * W_GROUP), row(W_GROUP)] + [_layer_spec(a, layer) for a in stacked],
        out_specs=[row(d), row(d + ROUTER_LANES),
                   pl.BlockSpec((None, 8, tm), lambda i: (i, 0, 0)),
                   full((N_BUCKET_ROWS, ROUTER_LANES))],
        out_shape=[jax.ShapeDtypeStruct((tkn, d), F32),
                   jax.ShapeDtypeStruct((tkn, d + ROUTER_LANES), F32),
                   jax.ShapeDtypeStruct((tkn // tm, 8, tm), jnp.int32),
                   jax.ShapeDtypeStruct((N_BUCKET_ROWS, ROUTER_LANES), F32)],
        scratch_shapes=[pltpu.VMEM((ROUTER_LANES, tm), F32), pltpu.VMEM((ROUTER_LANES, tm), F32),
                        pltpu.VMEM((N_BUCKET_ROWS, tm), F32), pltpu.VMEM((N_BUCKET_ROWS, tm), F32)],
        compiler_params=_cparams(("arbitrary",)),
        name="outproj_router",
    )(h2d, yabd, yc, *stacked)


def _moe_kernel(vt_ref, vea_ref, veb_ref, vlo_ref, vhi_ref, vfl_ref,
                tok_ref, tokn_ref, xs_hbm, wga_ref, wgb_ref, wua_ref, wub_ref, wda_ref, wdb_ref,
                y_hbm, xbuf, ybuf, gsem, ssem, *, tm, n_tiles, n_visits):
    v = pl.program_id(0)
    tile = vt_ref[v]
    flags = vfl_ref[v]
    first = (flags & 1) != 0
    last = (flags & 2) != 0
    valid = (flags & 4) != 0
    slot = tile % 2
    d = y_hbm.shape[1]

    def row_gather(ids_ref, s, k, u):
        return pltpu.make_async_copy(xs_hbm.at[pl.ds(ids_ref[0, k * SUBLANES + u], 1), :],
                                     xbuf.at[s, k, pl.ds(u, 1), :], gsem.at[s])

    def row_scatter(ids_ref, s, k, u):
        return pltpu.make_async_copy(ybuf.at[s, k, pl.ds(u, 1), :],
                                     y_hbm.at[pl.ds(ids_ref[0, k * SUBLANES + u], 1), :], ssem.at[s])

    def start_all(mk, ids_ref, s):
        def body(k, c):
            for u in range(SUBLANES):
                mk(ids_ref, s, k, u).start(priority=u % 2)
            return c
        lax.fori_loop(0, tm // SUBLANES, body, 0)

    def wait_gather(s):
        pltpu.make_async_copy(xbuf.at[s], xbuf.at[s], gsem.at[s]).wait()

    def wait_scatter(s):
        pltpu.make_async_copy(ybuf.at[s], ybuf.at[s], ssem.at[s]).wait()

    @pl.when(v == 0)
    def _():
        start_all(row_gather, tok_ref, 0)

    @pl.when(jnp.logical_and(first, valid))
    def _():
        wait_gather(slot)

        @pl.when(tile + 1 < n_tiles)
        def _():
            start_all(row_gather, tokn_ref, 1 - slot)

        @pl.when(tile >= 2)
        def _():
            wait_scatter(slot)

    @pl.when(valid)
    def _():
        rows = tile * tm + lax.broadcasted_iota(jnp.int32, (tm, 1), 0)
        inb = jnp.where(jnp.logical_and(rows >= vlo_ref[v], rows < vhi_ref[v]), 1.0, 0.0)
        xt = xbuf[slot].reshape(tm, xs_hbm.shape[1])
        x = xt[:, 0:d].astype(BF16)
        c_a = xt[:, d:d + 1] * inb
        c_b = xt[:, d + 1:d + 2] * inb

        def expert(wg_ref, wu_ref, wd_ref, c):
            gate = jnp.dot(x, wg_ref[...], preferred_element_type=F32)
            up = jnp.dot(x, wu_ref[...], preferred_element_type=F32)
            hdn = gate * _sigmoid(gate) * up * c
            return jnp.dot(hdn.astype(BF16), wd_ref[...], preferred_element_type=F32)

        y = expert(wga_ref, wua_ref, wda_ref, c_a) + expert(wgb_ref, wub_ref, wdb_ref, c_b)

        y = y.reshape(tm // SUBLANES, SUBLANES, d)

        @pl.when(first)
        def _():
            ybuf[slot] = y

        @pl.when(jnp.logical_not(first))
        def _():
            ybuf[slot] = ybuf[slot] + y

    @pl.when(jnp.logical_and(last, valid))
    def _():
        start_all(row_scatter, tok_ref, slot)

    @pl.when(v == n_visits - 1)
    def _():
        wait_scatter(0)
        wait_scatter(1)


def _moe(xs, tok_of_slot, sched, layer, prm):
    tkn, dx = xs.shape
    d = dx - ROUTER_LANES
    tm = MOE_TILE
    assert tkn % tm == 0 and tkn // tm >= 2
    n_tiles = tkn // tm
    n_visits = sched[0].shape[0]
    tok3 = tok_of_slot.reshape(n_tiles, 1, tm)
    wspec = lambda shape, which: pl.BlockSpec(
        (None, None) + shape, lambda v, vt, vea, veb, vlo, vhi, vfl: (layer, (vea, veb)[which][v], 0, 0))
    grid_spec = pltpu.PrefetchScalarGridSpec(
        num_scalar_prefetch=6,
        grid=(n_visits,),
        in_specs=[
            pl.BlockSpec((None, 1, tm), lambda v, vt, *_: (vt[v], 0, 0), memory_space=pltpu.SMEM),
            pl.BlockSpec((None, 1, tm), lambda v, vt, *_: (jnp.minimum(vt[v] + 1, n_tiles - 1), 0, 0),
                         memory_space=pltpu.SMEM),
            pl.BlockSpec(memory_space=pl.ANY),
            wspec((d, D_EXPERT), 0), wspec((d, D_EXPERT), 1),
            wspec((d, D_EXPERT), 0), wspec((d, D_EXPERT), 1),
            wspec((D_EXPERT, d), 0), wspec((D_EXPERT, d), 1),
        ],
        out_specs=pl.BlockSpec(memory_space=pl.ANY),
        scratch_shapes=[pltpu.VMEM((2, tm // SUBLANES, SUBLANES, dx), F32),
                        pltpu.VMEM((2, tm // SUBLANES, SUBLANES, d), F32),
                        pltpu.SemaphoreType.DMA((2,)), pltpu.SemaphoreType.DMA((2,))],
    )
    return pl.pallas_call(
        functools.partial(_moe_kernel, tm=tm, n_tiles=n_tiles, n_visits=n_visits),
        grid_spec=grid_spec,
        out_shape=jax.ShapeDtypeStruct((tkn, d), F32),
        compiler_params=_cparams(("arbitrary",)),
        name="moe_experts",
    )(*sched, tok3, tok3, xs, prm["exp_wg"], prm["exp_wg"], prm["exp_wu"], prm["exp_wu"], prm["exp_wd"], prm["exp_wd"])


def _moe_schedule(bucket, rank, counts, tm):
    tkn = bucket.size
    n_tiles = tkn // tm
    n_visits = n_tiles + N_BUCKETS
    ends = jnp.cumsum(counts)
    offs = ends - counts
    slot = rank
    for bb in range(N_BUCKETS):
        slot = slot + jnp.where(bucket == bb, offs[bb], 0)
    _, tok_of_slot = lax.sort_key_val(slot.reshape(tkn), jnp.arange(tkn, dtype=jnp.int32))
    t_first = offs // tm
    t_last = jnp.maximum(ends - 1, 0) // tm
    nv = jnp.where(counts > 0, t_last - t_first + 1, 0)
    v_end = jnp.cumsum(nv)
    v_start = v_end - nv
    total = v_end[-1]
    v = jnp.arange(n_visits, dtype=jnp.int32)
    vc = jnp.minimum(v, total - 1)
    b = jnp.sum((v_end[None, :] <= vc[:, None]).astype(jnp.int32), axis=1)
    onehot = b[:, None] == jnp.arange(N_BUCKETS, dtype=jnp.int32)[None, :]
    pick = lambda table: jnp.sum(jnp.where(onehot, table[None, :], 0), axis=1)
    tile = (pick(t_first) + vc - pick(v_start)).astype(jnp.int32)
    valid = v < total
    prev_tile = jnp.concatenate([jnp.full((1,), -1, jnp.int32), tile[:-1]])
    next_tile = jnp.concatenate([tile[1:], jnp.full((1,), -1, jnp.int32)])
    first = tile != prev_tile
    last = jnp.logical_or(tile != next_tile, v == total - 1)
    flags = (first.astype(jnp.int32) + 2 * last.astype(jnp.int32) + 4) * valid.astype(jnp.int32)
    g, pr = b // 6, b % 6
    lo_e = (pr >= 3).astype(jnp.int32) + (pr >= 5).astype(jnp.int32)
    hi_e = pr + 1 - 2 * (pr >= 3).astype(jnp.int32) - (pr >= 5).astype(jnp.int32)
    ea = g * EXPERTS_PER_GROUP + lo_e
    eb = g * EXPERTS_PER_GROUP + hi_e
    sched = (tile, ea.astype(jnp.int32), eb.astype(jnp.int32),
             pick(offs).astype(jnp.int32), pick(ends).astype(jnp.int32), flags.astype(jnp.int32))
    return tok_of_slot, sched


def _ple_final_kernel(h_ref, y_ref, p_ref, g_ref, wg_ref, bg_ref, wp_ref, fg_ref, o_ref):
    out = _ple_update(h_ref[...], y_ref[...], p_ref[...], g_ref[...], wg_ref[...], bg_ref[...], wp_ref[...])
    o_ref[...] = _rms_rows(out, fg_ref[...])


def _ple_final(h2d, y2d, layer, prm):
    tkn, d = h2d.shape
    p4 = prm["p"]
    _, bsz, seq, pd = p4.shape
    tm = min(FINAL_TILE, seq)
    assert seq % tm == 0 and tkn == bsz * seq
    per_row = seq // tm
    stacked = [prm[n] for n in ("ple_norm", "ple_wg", "ple_bg", "ple_wp")]
    return pl.pallas_call(
        _ple_final_kernel,
        grid=(tkn // tm,),
        in_specs=[pl.BlockSpec((tm, d), lambda i: (i, 0)), pl.BlockSpec((tm, d), lambda i: (i, 0)),
                  pl.BlockSpec((None, None, tm, pd), lambda i: (layer, i // per_row, i % per_row, 0))]
                 + [_layer_spec(a, layer) for a in stacked]
                 + [pl.BlockSpec((1, d), lambda i: (0, 0))],
        out_specs=pl.BlockSpec((tm, d), lambda i: (i, 0)),
        out_shape=jax.ShapeDtypeStruct((tkn, d), F32),
        compiler_params=_cparams(("arbitrary",)),
        name="ple_final",
    )(h2d, y2d, p4, *stacked, prm["final_norm"])


def _rope_table(seq):
    pos = np.arange(seq, dtype=np.float64)
    inv = ROPE_THETA ** (-np.arange(0, DIFF_HEAD_DIM, 2, dtype=np.float64) / DIFF_HEAD_DIM)
    ang = pos[:, None] * inv[None, :]
    ang = np.concatenate([ang, ang], axis=-1)
    cos, sin = np.cos(ang), np.sin(ang)
    lo = np.arange(DIFF_HEAD_DIM) < DIFF_HEAD_DIM // 2
    sin_lo = np.where(lo, -sin, 0.0)
    sin_hi = np.where(lo, 0.0, sin)
    rep = 128 // DIFF_HEAD_DIM
    table = np.concatenate([np.tile(t, (1, rep)) for t in (cos, sin_lo, sin_hi)], axis=1)
    return jnp.asarray(table.astype(np.float32))


def _block_diag(w):
    nl, g, c, dd = w.shape
    out = jnp.zeros((nl, g * c, g * dd), w.dtype)
    for i in range(g):
        out = out.at[:, i * c:(i + 1) * c, i * dd:(i + 1) * dd].set(w[:, i])
    return out


def _router_weights(wg, bg, we, be):
    nl, d, _ = wg.shape
    w = jnp.zeros((nl, d, ROUTER_LANES), F32)
    w = w.at[:, :, 0:N_GROUPS].set(wg).at[:, :, 8:8 + N_EXPERTS].set(we)
    w_hi = w.astype(BF16)
    w_lo = (w - w_hi.astype(F32)).astype(BF16)
    wr = w_hi.at[:, :, ROUTER_LO:ROUTER_LO + 8 + N_EXPERTS].set(w_lo[:, :, 0:8 + N_EXPERTS])
    br = jnp.zeros((nl, 1, ROUTER_LANES), F32)
    br = br.at[:, 0, 0:N_GROUPS].set(bg).at[:, 0, 8:8 + N_EXPERTS].set(be)
    return wr, br


def kernel(x, p, mix_norm, w_in, conf_conv_w, conf_conv_b, conf_ln_g, conf_ln_b, pool_w, pool_b, pool_scale, diff_lam_q1, diff_lam_k1, diff_lam_q2, diff_lam_k2, diff_subln_g, sconv_w, w_out, ffn_norm, router_group_w, router_group_b, router_expert_w, router_expert_b, expert_w_gate, expert_w_up, expert_w_down, ple_norm, ple_gate_w, ple_gate_b, ple_proj, final_norm):
    bsz, seq, d = x.shape
    depth = w_in.shape[0]
    tkn = bsz * seq
    att_t = ATT_TILE * ATT_Q_BLOCKS
    rows = lambda v: v.reshape(depth, 1, -1).astype(F32)
    router_w, router_b = _router_weights(router_group_w, router_group_b, router_expert_w, router_expert_b)
    prm = dict(
        p=p, rope=_rope_table(seq),
        mix_norm=rows(mix_norm), w_in=w_in.astype(BF16),
        conf_w=conf_conv_w, conf_b=rows(conf_conv_b), ln_g=rows(conf_ln_g), ln_b=rows(conf_ln_b),
        pool_bd=_block_diag(pool_w).astype(BF16), pool_b=rows(pool_b), pool_s=rows(pool_scale), sconv_w=sconv_w,
        lam=jnp.stack([diff_lam_q1, diff_lam_k1, diff_lam_q2, diff_lam_k2], axis=1).astype(F32),
        subln_g=jnp.broadcast_to(diff_subln_g.astype(F32)[:, :, None], (depth, DIFF_V_DIM, att_t)),
        w_out=w_out.astype(BF16), ffn_norm=rows(ffn_norm), router_w=router_w, router_b=router_b,
        exp_wg=expert_w_gate.astype(BF16), exp_wu=expert_w_up.astype(BF16), exp_wd=expert_w_down.astype(BF16),
        ple_norm=rows(ple_norm), ple_wg=ple_gate_w.astype(BF16), ple_bg=rows(ple_gate_b),
        ple_wp=ple_proj.astype(BF16), final_norm=final_norm.reshape(1, -1).astype(F32),
    )

    h, pending = x, None
    for i in range(depth):
        lam_init = 0.8 - 0.6 * math.exp(-0.3 * i)
        if pending is None:
            yabd, qt, k, vt = _inproj_mix(h, i, prm)
        else:
            h, yabd, qt, k, vt = _inproj_mix(h, i, prm, ple=pending)
        yc = _diff_attn(qt, k, vt, i, prm, lam_init)
        h1, xs, meta, cnt = _outproj_router(
            h.reshape(tkn, d), yabd.reshape(tkn, 3 * W_GROUP), yc.reshape(tkn, W_GROUP), i, prm)
        tok_of_slot, sched = _moe_schedule(
            meta[:, 0, :], meta[:, 1, :], cnt[:N_BUCKETS, 0].astype(jnp.int32), MOE_TILE)
        y = _moe(xs, tok_of_slot, sched, i, prm)
        if i == depth - 1:
            return _ple_final(h1, y, i, prm).reshape(bsz, seq, d)
        h = h1.reshape(bsz, seq, d)
        pending = (y.reshape(bsz, seq, d), i)
```
